```python
import math
import jax, jax.numpy as jnp
from jax import lax
import numpy as np


D_MODEL = 1024
BATCH = 8
SEQ = 8192
DEPTH = 1
DEC_BATCH = 2
DEC_SEQ = 16384
PAST_LEN = 128

A_HEADS = 4
A_QK_DIM = 64
A_V_DIM = 2 * A_QK_DIM
A_WIDTH = A_HEADS * A_V_DIM
A_ROPE_DIM = A_QK_DIM // 4
B_HEADS = 4
B_NOPE_DIM = 128
B_ROPE_DIM = 64
B_V_DIM = 128
B_WIDTH = B_HEADS * B_V_DIM
Q_LORA = 384
KV_LORA = 256
MIX_WIDTH = A_WIDTH + B_WIDTH
ROPE_THETA = 500000.0
NORM_EPS = 1e-6
Q_BLOCK = 128

IN_SPLIT = (2 * A_HEADS * A_QK_DIM,
            2 * A_HEADS * A_QK_DIM,
            A_WIDTH,
            A_WIDTH,
            Q_LORA,
            KV_LORA,
            B_ROPE_DIM,
            B_WIDTH)
IN_COLS = 2 * A_HEADS * A_QK_DIM * 2 + 2 * A_WIDTH + Q_LORA + KV_LORA + B_ROPE_DIM + B_WIDTH

kernel_name = 'hymba_diff_mla_encoder'


def rmsnorm(x, g):
    xf = x.astype(jnp.float32)
    y = xf * lax.rsqrt(jnp.mean(xf * xf, axis=-1, keepdims=True) + NORM_EPS)
    return (y * g.astype(jnp.float32)).astype(x.dtype)


def split_cols(x, sizes):
    idx = []
    acc = 0
    for n in sizes[:-1]:
        acc += n
        idx.append(acc)
    return jnp.split(x, idx, axis=-1)


def rope_tables(seq, dim):
    inv_freq = jnp.float32(ROPE_THETA) ** (-jnp.arange(0, dim, 2, dtype=jnp.float32) / dim)
    ang = jnp.arange(seq, dtype=jnp.float32)[:, None] * inv_freq[None, :]
    return jnp.cos(ang), jnp.sin(ang)


def apply_rope(x, cos, sin):
    half = x.shape[-1] // 2
    x1, x2 = x[..., :half], x[..., half:]
    cos = cos.astype(x.dtype)
    sin = sin.astype(x.dtype)
    return jnp.concatenate([x1 * cos - x2 * sin, x2 * cos + x1 * sin], axis=-1)


def sweep_query_blocks(block_fn, queries):
    b, s = queries[0].shape[:2]
    nb = s // Q_BLOCK
    blocks = tuple(jnp.moveaxis(q.reshape((b, nb, Q_BLOCK) + q.shape[2:]), 1, 0) for q in queries)
    out = lax.map(lambda qb: block_fn(*qb), blocks)
    out = jnp.moveaxis(out, 0, 1)
    return out.reshape((b, s) + out.shape[3:])


def diff_attention(q1, q2, k1, k2, v, lam):
    scale = A_QK_DIM ** -0.5

    def block(q1b, q2b):
        s1 = jnp.einsum('bqhd,bkhd->bhqk', q1b, k1).astype(jnp.float32) * scale
        s2 = jnp.einsum('bqhd,bkhd->bhqk', q2b, k2).astype(jnp.float32) * scale
        w = jax.nn.softmax(s1, axis=-1) - lam * jax.nn.softmax(s2, axis=-1)
        return jnp.einsum('bhqk,bkhe->bqhe', w.astype(v.dtype), v)

    return sweep_query_blocks(block, (q1, q2))


def mla_attention(qn, qr, kn, kr, v):
    scale = (B_NOPE_DIM + B_ROPE_DIM) ** -0.5

    def block(qnb, qrb):
        s = (jnp.einsum('bqhd,bkhd->bhqk', qnb, kn)
             + jnp.einsum('bqhr,bkr->bhqk', qrb, kr)).astype(jnp.float32) * scale
        p = jax.nn.softmax(s, axis=-1)
        return jnp.einsum('bhqk,bkhe->bqhe', p.astype(v.dtype), v)

    return sweep_query_blocks(block, (qn, qr))


def encoder_layer(x, c, layer, w_ada, b_ada, g_pre, w_in, lambda_q1, lambda_k1, lambda_q2, lambda_k2,
                  g_subln, g_cq, w_uq, g_ckv, w_ukv, w_out, g_post):
    b, s, _ = x.shape
    lambda_init = 0.8 - 0.6 * math.exp(-0.3 * layer)

    mod = jax.nn.silu(c) @ w_ada + b_ada
    shift, scale, gate = jnp.split(mod, 3, axis=-1)
    h = rmsnorm(x, g_pre) * (1 + scale[:, None, :]) + shift[:, None, :]

    proj = h @ w_in
    aq, ak, av, ag, cq, ckv, kr, bg = split_cols(proj, IN_SPLIT)

    cos_a, sin_a = rope_tables(s, A_ROPE_DIM)
    cos_a4, sin_a4 = cos_a[:, None, None, :], sin_a[:, None, None, :]
    aq = aq.reshape(b, s, A_HEADS, 2, A_QK_DIM)
    ak = ak.reshape(b, s, A_HEADS, 2, A_QK_DIM)
    aq = jnp.concatenate([apply_rope(aq[..., :A_ROPE_DIM], cos_a4, sin_a4), aq[..., A_ROPE_DIM:]], axis=-1)
    ak = jnp.concatenate([apply_rope(ak[..., :A_ROPE_DIM], cos_a4, sin_a4), ak[..., A_ROPE_DIM:]], axis=-1)
    av = av.reshape(b, s, A_HEADS, A_V_DIM)
    lam = (jnp.exp(jnp.sum(lambda_q1.astype(jnp.float32) * lambda_k1.astype(jnp.float32)))
           - jnp.exp(jnp.sum(lambda_q2.astype(jnp.float32) * lambda_k2.astype(jnp.float32)))
           + lambda_init)
    oa = diff_attention(aq[..., 0, :], aq[..., 1, :], ak[..., 0, :], ak[..., 1, :], av, lam)
    oa = rmsnorm(oa, g_subln) * (1.0 - lambda_init)
    ya = oa.reshape(b, s, A_WIDTH) * jax.nn.silu(ag)

    cos_b, sin_b = rope_tables(s, B_ROPE_DIM)
    q = (rmsnorm(cq, g_cq) @ w_uq).reshape(b, s, B_HEADS, B_NOPE_DIM + B_ROPE_DIM)
    qn, qr = q[..., :B_NOPE_DIM], q[..., B_NOPE_DIM:]
    qr = apply_rope(qr, cos_b[:, None, :], sin_b[:, None, :])
    kv = (rmsnorm(ckv, g_ckv) @ w_ukv).reshape(b, s, B_HEADS, B_NOPE_DIM + B_V_DIM)
    kn, vb = kv[..., :B_NOPE_DIM], kv[..., B_NOPE_DIM:]
    kr = apply_rope(kr, cos_b, sin_b)
    ob = mla_attention(qn, qr, kn, kr, vb)
    yb = ob.reshape(b, s, B_WIDTH) * jax.nn.silu(bg)

    out = jnp.concatenate([ya, yb], axis=-1) @ w_out
    return x + gate[:, None, :] * rmsnorm(out, g_post)


def setup_inputs(seed: int = 0) -> dict:
    key = jax.random.key(seed)
    ks = jax.random.split(key, 24)
    f32 = jnp.float32
    nrm = lambda k, shape, s: jax.random.normal(k, shape, f32) * s
    gain = lambda k, shape: 1.0 + 0.02 * jax.random.normal(k, shape, f32)
    L = DEPTH
    return {
        'x_prompt': nrm(ks[0], (BATCH, SEQ, D_MODEL), 1.0),
        'x_sample': nrm(ks[1], (DEC_BATCH, DEC_SEQ, D_MODEL), 1.0),
        'c_prompt': nrm(ks[2], (BATCH, D_MODEL), 1.0),
        'c_sample': nrm(ks[3], (DEC_BATCH, D_MODEL), 1.0),
        'w_ada': nrm(ks[4], (L, D_MODEL, 3 * D_MODEL), 0.3 * D_MODEL ** -0.5),
        'b_ada': nrm(ks[5], (L, 3 * D_MODEL), 0.01),
        'g_pre': gain(ks[6], (L, D_MODEL)),
        'w_in': nrm(ks[7], (L, D_MODEL, IN_COLS), D_MODEL ** -0.5),
        'lambda_q1': nrm(ks[8], (L, A_QK_DIM), 0.1),
        'lambda_k1': nrm(ks[9], (L, A_QK_DIM), 0.1),
        'lambda_q2': nrm(ks[10], (L, A_QK_DIM), 0.1),
        'lambda_k2': nrm(ks[11], (L, A_QK_DIM), 0.1),
        'g_subln': gain(ks[12], (L, A_V_DIM)),
        'g_cq': gain(ks[13], (L, Q_LORA)),
        'w_uq': nrm(ks[14], (L, Q_LORA, B_HEADS * (B_NOPE_DIM + B_ROPE_DIM)), Q_LORA ** -0.5),
        'g_ckv': gain(ks[15], (L, KV_LORA)),
        'w_ukv': nrm(ks[16], (L, KV_LORA, B_HEADS * (B_NOPE_DIM + B_V_DIM)), KV_LORA ** -0.5),
        'w_out': nrm(ks[17], (L, MIX_WIDTH, D_MODEL), MIX_WIDTH ** -0.5),
        'g_post': gain(ks[18], (L, D_MODEL)),
    }


def reference(x_prompt, x_sample, c_prompt, c_sample, w_ada, b_ada, g_pre, w_in, lambda_q1, lambda_k1,
              lambda_q2, lambda_k2, g_subln, g_cq, w_uq, g_ckv, w_ukv, w_out, g_post):
    y_prompt = x_prompt
    y_sample = x_sample
    for layer in range(DEPTH):
        lp = (w_ada[layer], b_ada[layer], g_pre[layer], w_in[layer], lambda_q1[layer], lambda_k1[layer],
              lambda_q2[layer], lambda_k2[layer], g_subln[layer], g_cq[layer], w_uq[layer], g_ckv[layer],
              w_ukv[layer], w_out[layer], g_post[layer])
        y_prompt = encoder_layer(y_prompt, c_prompt, layer, *lp)
        y_sample = encoder_layer(y_sample, c_sample, layer, *lp)
    return (y_prompt, y_sample)
```

```python
import functools
import math

import jax
import jax.numpy as jnp
from jax import lax
from jax.experimental import pallas as pl
from jax.experimental.pallas import tpu as pltpu

D_MODEL = 1024
A_HEADS = 4
A_QK_DIM = 64
A_V_DIM = 128
A_WIDTH = A_HEADS * A_V_DIM
A_ROPE_DIM = A_QK_DIM // 4
B_HEADS = 4
B_NOPE_DIM = 128
B_ROPE_DIM = 64
B_V_DIM = 128
B_WIDTH = B_HEADS * B_V_DIM
B_QK_PAD = 256
Q_LORA = 384
KV_LORA = 256
ROPE_THETA = 500000.0
NORM_EPS = 1e-6
LOG2E = 1.4426950408889634

LANES = 128
VMEM_LIMIT = 56 * 1024 * 1024

ROW_TILE = 512
Q_TILE = 256
KV_TILE = 512

SEG_AQ = 0
SEG_AK = 512
SEG_AV = 1024
SEG_AG = 1536
SEG_CQ = 2048
SEG_CKV = 2432
SEG_BG = 2688
SEG_KR = 3200
IN_COLS_PAD = 3328


def _silu(x):
    return x * (1.0 / (1.0 + jnp.exp(-x)))


def _rms(x, g):
    return x * lax.rsqrt(jnp.mean(x * x, axis=-1, keepdims=True) + NORM_EPS) * g


def _bdot(a, b):
    return jnp.dot(a, b, preferred_element_type=jnp.float32)


def _mod_kernel(c_ref, w_ref, b_ref, o_ref):
    c = _silu(c_ref[...]).astype(jnp.bfloat16)
    o_ref[...] = _bdot(c, w_ref[...].astype(jnp.bfloat16)) + b_ref[...]


def _adaln_mod(c_all, w_ada, b_ada):
    rows = c_all.shape[0]
    n_blk = 1024
    return pl.pallas_call(
        _mod_kernel,
        out_shape=jax.ShapeDtypeStruct((rows, 3 * D_MODEL), jnp.float32),
        grid=(3 * D_MODEL // n_blk,),
        in_specs=[
            pl.BlockSpec((rows, D_MODEL), lambda j: (0, 0)),
            pl.BlockSpec((D_MODEL, n_blk), lambda j: (0, j)),
            pl.BlockSpec((1, n_blk), lambda j: (0, j)),
        ],
        out_specs=pl.BlockSpec((rows, n_blk), lambda j: (0, j)),
        compiler_params=pltpu.CompilerParams(
            dimension_semantics=("arbitrary",), vmem_limit_bytes=VMEM_LIMIT),
        name="adaln_mod",
    )(c_all, w_ada, b_ada.reshape(1, -1))


def _rope(x, c, s_lo, s_hi, half):
    return (x * c + pltpu.roll(x, LANES - half, 1) * s_lo
            + pltpu.roll(x, half, 1) * s_hi)


def _in_proj_kernel(x_ref, mod_ref, gpre_ref, win_ref, gcq_ref, wuq_ref, gckv_ref, wukv_ref,
                    ca_ref, sa_lo_ref, sa_hi_ref, cb_ref, sb_lo_ref, sb_hi_ref,
                    qa_ref, ka_ref, va_ref, qb_ref, kb_ref, vb_ref, sg_ref,
                    *, scale_a, scale_b):
    x = x_ref[0]
    mod = mod_ref[0]
    shift = mod[:, 0:D_MODEL]
    scale = mod[:, D_MODEL:2 * D_MODEL]
    h = (_rms(x, gpre_ref[...]) * (1.0 + scale) + shift).astype(jnp.bfloat16)

    def proj(lo, width):
        return _bdot(h, win_ref[:, lo:lo + width])

    ca, sa_lo, sa_hi = ca_ref[...], sa_lo_ref[...], sa_hi_ref[...]
    cb, sb_lo, sb_hi = cb_ref[...], sb_lo_ref[...], sb_hi_ref[...]
    half_a = A_ROPE_DIM // 2
    half_b = B_ROPE_DIM // 2

    aq = proj(SEG_AQ, 512)
    ak = proj(SEG_AK, 512)
    av = proj(SEG_AV, 512)
    for hd in range(A_HEADS):
        sl = slice(hd * LANES, (hd + 1) * LANES)
        qa_ref[0, hd] = (_rope(aq[:, sl], ca, sa_lo, sa_hi, half_a) * scale_a).astype(jnp.bfloat16)
        ka_ref[0, hd] = _rope(ak[:, sl], ca, sa_lo, sa_hi, half_a).astype(jnp.bfloat16)
        va_ref[0, hd] = av[:, sl].astype(jnp.bfloat16)

    sg_ref[0, :, 0:A_WIDTH] = _silu(proj(SEG_AG, A_WIDTH))
    sg_ref[0, :, A_WIDTH:A_WIDTH + B_WIDTH] = _silu(proj(SEG_BG, B_WIDTH))

    cq = _rms(proj(SEG_CQ, Q_LORA), gcq_ref[...]).astype(jnp.bfloat16)
    q = _bdot(cq, wuq_ref[...])
    ckv = _rms(proj(SEG_CKV, KV_LORA), gckv_ref[...]).astype(jnp.bfloat16)
    kv = _bdot(ckv, wukv_ref[...])
    kr = _rope(proj(SEG_KR, LANES), cb, sb_lo, sb_hi, half_b).astype(jnp.bfloat16)
    for hd in range(B_HEADS):
        qn = q[:, hd * LANES:(hd + 1) * LANES]
        qr = q[:, (B_HEADS + hd) * LANES:(B_HEADS + hd + 1) * LANES]
        qb_ref[0, hd, :, 0:LANES] = (qn * scale_b).astype(jnp.bfloat16)
        qb_ref[0, hd, :, LANES:2 * LANES] = (
            _rope(qr, cb, sb_lo, sb_hi, half_b) * scale_b).astype(jnp.bfloat16)
        kb_ref[0, hd, :, 0:LANES] = kv[:, 2 * hd * LANES:(2 * hd + 1) * LANES].astype(jnp.bfloat16)
        kb_ref[0, hd, :, LANES:2 * LANES] = kr
        vb_ref[0, hd] = kv[:, (2 * hd + 1) * LANES:(2 * hd + 2) * LANES].astype(jnp.bfloat16)


def _in_proj(x, mod, gpre, win, gcq, wuq, gckv, wukv, tabs):
    b, s, _ = x.shape
    tr = ROW_TILE
    const = lambda shape: pl.BlockSpec(shape, lambda i, j: (0,) * len(shape))
    tab = pl.BlockSpec((tr, LANES), lambda i, j: (j, 0))
    head = lambda d: pl.BlockSpec((1, A_HEADS, tr, d), lambda i, j: (i, 0, j, 0))
    bf = jnp.bfloat16
    kern = functools.partial(_in_proj_kernel,
                             scale_a=A_QK_DIM ** -0.5 * LOG2E,
                             scale_b=(B_NOPE_DIM + B_ROPE_DIM) ** -0.5 * LOG2E)
    return pl.pallas_call(
        kern,
        out_shape=(
            jax.ShapeDtypeStruct((b, A_HEADS, s, LANES), bf),
            jax.ShapeDtypeStruct((b, A_HEADS, s, LANES), bf),
            jax.ShapeDtypeStruct((b, A_HEADS, s, A_V_DIM), bf),
            jax.ShapeDtypeStruct((b, B_HEADS, s, B_QK_PAD), bf),
            jax.ShapeDtypeStruct((b, B_HEADS, s, B_QK_PAD), bf),
            jax.ShapeDtypeStruct((b, B_HEADS, s, B_V_DIM), bf),
            jax.ShapeDtypeStruct((b, s, A_WIDTH + B_WIDTH), jnp.float32),
        ),
        grid=(b, s // tr),
        in_specs=[
            pl.BlockSpec((1, tr, D_MODEL), lambda i, j: (i, j, 0)),
            pl.BlockSpec((1, 1, 3 * D_MODEL), lambda i, j: (i, 0, 0)),
            const((1, D_MODEL)),
            const((D_MODEL, IN_COLS_PAD)),
            const((1, Q_LORA)),
            const((Q_LORA, 2 * B_HEADS * LANES)),
            const((1, KV_LORA)),
            const((KV_LORA, B_HEADS * (B_NOPE_DIM + B_V_DIM))),
            tab, tab, tab, tab, tab, tab,
        ],
        out_specs=(head(LANES), head(LANES), head(A_V_DIM), head(B_QK_PAD), head(B_QK_PAD),
                   head(B_V_DIM),
                   pl.BlockSpec((1, tr, A_WIDTH + B_WIDTH), lambda i, j: (i, j, 0))),
        compiler_params=pltpu.CompilerParams(
            dimension_semantics=("arbitrary", "arbitrary"), vmem_limit_bytes=VMEM_LIMIT),
        name="in_proj",
    )(x, mod, gpre, win, gcq, wuq, gckv, wukv, *tabs)


def _qk(q, k):
    return lax.dot_general(q, k, (((1,), (1,)), ((), ())), preferred_element_type=jnp.float32)


def _online_step(s, v, m_ref, l_ref, acc_ref):
    m_old = m_ref[...]
    m_new = jnp.maximum(m_old, jnp.max(s, axis=-1, keepdims=True))
    alpha = jnp.exp2(m_old - m_new)
    p = jnp.exp2(s - m_new)
    l_ref[...] = alpha * l_ref[...] + jnp.sum(p, axis=-1, keepdims=True)
    acc_ref[...] = alpha * acc_ref[...] + _bdot(p.astype(jnp.bfloat16), v)
    m_ref[...] = m_new


def _init_stats(m_ref, l_ref, acc_ref):
    m_ref[...] = jnp.full(m_ref.shape, -jnp.inf, jnp.float32)
    l_ref[...] = jnp.zeros(l_ref.shape, jnp.float32)
    acc_ref[...] = jnp.zeros(acc_ref.shape, jnp.float32)


def _attn_a_kernel(lam_ref, gsub_ref, q_ref, k_ref, v_ref, sg_ref, o_ref,
                   m1_ref, l1_ref, acc1_ref, m2_ref, l2_ref, acc2_ref, *, lambda_init):
    n_kv = k_ref.shape[2] // KV_TILE
    q = q_ref[0, 0]
    lane = lax.broadcasted_iota(jnp.int32, q.shape, 1)
    zero = jnp.zeros_like(q)
    q1 = jnp.where(lane < A_QK_DIM, q, zero)
    q2 = jnp.where(lane >= A_QK_DIM, q, zero)
    _init_stats(m1_ref, l1_ref, acc1_ref)
    _init_stats(m2_ref, l2_ref, acc2_ref)

    def body(j, carry):
        off = pl.multiple_of(j * KV_TILE, KV_TILE)
        k = k_ref[0, 0, pl.ds(off, KV_TILE), :]
        v = v_ref[0, 0, pl.ds(off, KV_TILE), :]
        _online_step(_qk(q1, k), v, m1_ref, l1_ref, acc1_ref)
        _online_step(_qk(q2, k), v, m2_ref, l2_ref, acc2_ref)
        return carry

    lax.fori_loop(0, n_kv, body, 0)

    lam_v = lam_ref[...]
    lam = (jnp.exp(jnp.sum(lam_v[0:1] * lam_v[1:2], axis=-1, keepdims=True))
           - jnp.exp(jnp.sum(lam_v[2:3] * lam_v[3:4], axis=-1, keepdims=True))
           + lambda_init)
    o = acc1_ref[...] / l1_ref[...] - lam * (acc2_ref[...] / l2_ref[...])
    o = _rms(o, gsub_ref[...]) * (1.0 - lambda_init)
    o_ref[0] = (o * sg_ref[0]).astype(o_ref.dtype)


def _attn_b_kernel(q_ref, k_ref, v_ref, sg_ref, o_ref, m_ref, l_ref, acc_ref):
    n_kv = k_ref.shape[2] // KV_TILE
    q = q_ref[0, 0]
    _init_stats(m_ref, l_ref, acc_ref)

    def body(j, carry):
        off = pl.multiple_of(j * KV_TILE, KV_TILE)
        k = k_ref[0, 0, pl.ds(off, KV_TILE), :]
        v = v_ref[0, 0, pl.ds(off, KV_TILE), :]
        _online_step(_qk(q, k), v, m_ref, l_ref, acc_ref)
        return carry

    lax.fori_loop(0, n_kv, body, 0)
    o = acc_ref[...] / l_ref[...]
    o_ref[0] = (o * sg_ref[0]).astype(o_ref.dtype)


def _attn_specs(s, dq, dv, col0):
    tq = Q_TILE
    return dict(
        q=pl.BlockSpec((1, 1, tq, dq), lambda b, h, i: (b, h, i, 0)),
        k=pl.BlockSpec((1, 1, s, dq), lambda b, h, i: (b, h, 0, 0)),
        v=pl.BlockSpec((1, 1, s, dv), lambda b, h, i: (b, h, 0, 0)),
        sg=pl.BlockSpec((1, tq, dv), lambda b, h, i: (b, i, col0 + h)),
        o=pl.BlockSpec((1, tq, dv), lambda b, h, i: (b, i, h)),
    )


def _stat_scratch(dv):
    return [pltpu.VMEM((Q_TILE, 1), jnp.float32), pltpu.VMEM((Q_TILE, 1), jnp.float32),
            pltpu.VMEM((Q_TILE, dv), jnp.float32)]


def _attn_a(lam_vecs, gsub, qa, ka, va, sg, lambda_init):
    b, nh, s, _ = qa.shape
    sp = _attn_specs(s, LANES, A_V_DIM, 0)
    return pl.pallas_call(
        functools.partial(_attn_a_kernel, lambda_init=lambda_init),
        out_shape=jax.ShapeDtypeStruct((b, s, A_WIDTH), jnp.bfloat16),
        grid=(b, nh, s // Q_TILE),
        in_specs=[
            pl.BlockSpec((4, A_QK_DIM), lambda b_, h, i: (0, 0)),
            pl.BlockSpec((1, A_V_DIM), lambda b_, h, i: (0, 0)),
            sp["q"], sp["k"], sp["v"], sp["sg"],
        ],
        out_specs=sp["o"],
        scratch_shapes=_stat_scratch(A_V_DIM) + _stat_scratch(A_V_DIM),
        compiler_params=pltpu.CompilerParams(
            dimension_semantics=("arbitrary", "arbitrary", "arbitrary"),
            vmem_limit_bytes=VMEM_LIMIT),
        name="attn_a",
    )(lam_vecs, gsub, qa, ka, va, sg)


def _attn_b(qb, kb, vb, sg):
    b, nh, s, _ = qb.shape
    sp = _attn_specs(s, B_QK_PAD, B_V_DIM, A_HEADS)
    return pl.pallas_call(
        _attn_b_kernel,
        out_shape=jax.ShapeDtypeStruct((b, s, B_WIDTH), jnp.bfloat16),
        grid=(b, nh, s // Q_TILE),
        in_specs=[sp["q"], sp["k"], sp["v"], sp["sg"]],
        out_specs=sp["o"],
        scratch_shapes=_stat_scratch(B_V_DIM),
        compiler_params=pltpu.CompilerParams(
            dimension_semantics=("arbitrary", "arbitrary", "arbitrary"),
            vmem_limit_bytes=VMEM_LIMIT),
        name="attn_b",
    )(qb, kb, vb, sg)


def _out_proj_kernel(ya_ref, yb_ref, x_ref, mod_ref, wout_ref, gpost_ref, o_ref):
    out = (_bdot(ya_ref[0], wout_ref[0:A_WIDTH, :])
           + _bdot(yb_ref[0], wout_ref[A_WIDTH:A_WIDTH + B_WIDTH, :]))
    gate = mod_ref[0][:, 2 * D_MODEL:3 * D_MODEL]
    o_ref[0] = x_ref[0] + gate * _rms(out, gpost_ref[...])


def _out_proj(ya, yb, x, mod, wout, gpost):
    b, s, _ = x.shape
    tr = ROW_TILE
    return pl.pallas_call(
        _out_proj_kernel,
        out_shape=jax.ShapeDtypeStruct(x.shape, x.dtype),
        grid=(b, s // tr),
        in_specs=[
            pl.BlockSpec((1, tr, A_WIDTH), lambda i, j: (i, j, 0)),
            pl.BlockSpec((1, tr, B_WIDTH), lambda i, j: (i, j, 0)),
            pl.BlockSpec((1, tr, D_MODEL), lambda i, j: (i, j, 0)),
            pl.BlockSpec((1, 1, 3 * D_MODEL), lambda i, j: (i, 0, 0)),
            pl.BlockSpec((A_WIDTH + B_WIDTH, D_MODEL), lambda i, j: (0, 0)),
            pl.BlockSpec((1, D_MODEL), lambda i, j: (0, 0)),
        ],
        out_specs=pl.BlockSpec((1, tr, D_MODEL), lambda i, j: (i, j, 0)),
        compiler_params=pltpu.CompilerParams(
            dimension_semantics=("arbitrary", "arbitrary"), vmem_limit_bytes=VMEM_LIMIT),
        name="out_proj",
    )(ya, yb, x, mod, wout, gpost)


def _rope_tables(seq, dim):
    half = dim // 2
    inv_freq = jnp.float32(ROPE_THETA) ** (-jnp.arange(0, dim, 2, dtype=jnp.float32) / dim)
    ang = jnp.arange(seq, dtype=jnp.float32)[:, None] * inv_freq[None, :]
    cos, sin = jnp.cos(ang), jnp.sin(ang)
    pad = 64 - dim
    ones = jnp.ones((seq, pad), jnp.float32)
    zeros = jnp.zeros((seq, pad), jnp.float32)
    zh = jnp.zeros((seq, half), jnp.float32)
    c = jnp.concatenate([cos, cos, ones], axis=1)
    s_lo = jnp.concatenate([-sin, zh, zeros], axis=1)
    s_hi = jnp.concatenate([zh, sin, zeros], axis=1)
    return tuple(jnp.concatenate([t, t], axis=1) for t in (c, s_lo, s_hi))


def _layout_w_in(w_in):
    aq, ak, av, ag, cq, ckv, kr, bg = jnp.split(
        w_in, [512, 1024, 1536, 2048, 2432, 2688, 2752], axis=1)
    kr = jnp.pad(kr, ((0, 0), (0, LANES - B_ROPE_DIM)))
    return jnp.concatenate([aq, ak, av, ag, cq, ckv, bg, kr], axis=1).astype(jnp.bfloat16)


def _layout_w_uq(w_uq):
    w = w_uq.reshape(Q_LORA, B_HEADS, B_NOPE_DIM + B_ROPE_DIM)
    nope = w[:, :, :B_NOPE_DIM].reshape(Q_LORA, B_HEADS * B_NOPE_DIM)
    rope = jnp.pad(w[:, :, B_NOPE_DIM:], ((0, 0), (0, 0), (0, LANES - B_ROPE_DIM)))
    return jnp.concatenate([nope, rope.reshape(Q_LORA, B_HEADS * LANES)], axis=1).astype(jnp.bfloat16)


def _encoder_layer(x, mod, layer, tabs, g_pre, w_in, lam_vecs, g_subln, g_cq, w_uq, g_ckv, w_ukv,
                   w_out, g_post):
    lambda_init = 0.8 - 0.6 * math.exp(-0.3 * layer)
    qa, ka, va, qb, kb, vb, sg = _in_proj(x, mod, g_pre, w_in, g_cq, w_uq, g_ckv, w_ukv, tabs)
    ya = _attn_a(lam_vecs, g_subln, qa, ka, va, sg, lambda_init)
    yb = _attn_b(qb, kb, vb, sg)
    return _out_proj(ya, yb, x, mod, w_out, g_post)


def kernel(x_prompt, x_sample, c_prompt, c_sample, w_ada, b_ada, g_pre, w_in, lambda_q1, lambda_k1,
           lambda_q2, lambda_k2, g_subln, g_cq, w_uq, g_ckv, w_ukv, w_out, g_post):
    depth = w_ada.shape[0]
    nb_p, nb_s = c_prompt.shape[0], c_sample.shape[0]
    pad_rows = -(nb_p + nb_s) % 8
    c_all = jnp.concatenate(
        [c_prompt, c_sample, jnp.zeros((pad_rows, D_MODEL), jnp.float32)], axis=0)
    tabs_p = _rope_tables(x_prompt.shape[1], A_ROPE_DIM) + _rope_tables(x_prompt.shape[1], B_ROPE_DIM)
    tabs_s = _rope_tables(x_sample.shape[1], A_ROPE_DIM) + _rope_tables(x_sample.shape[1], B_ROPE_DIM)
    y_p, y_s = x_prompt, x_sample
    for layer in range(depth):
        mod = _adaln_mod(c_all, w_ada[layer], b_ada[layer])
        mod_p = mod[:nb_p].reshape(nb_p, 1, 3 * D_MODEL)
        mod_s = mod[nb_p:nb_p + nb_s].reshape(nb_s, 1, 3 * D_MODEL)
        lam_vecs = jnp.stack([lambda_q1[layer], lambda_k1[layer], lambda_q2[layer], lambda_k2[layer]])
        params = (g_pre[layer].reshape(1, -1), _layout_w_in(w_in[layer]), lam_vecs,
                  g_subln[layer].reshape(1, -1), g_cq[layer].reshape(1, -1),
                  _layout_w_uq(w_uq[layer]), g_ckv[layer].reshape(1, -1),
                  w_ukv[layer].astype(jnp.bfloat16), w_out[layer].astype(jnp.bfloat16),
                  g_post[layer].reshape(1, -1))
        y_p = _encoder_layer(y_p, mod_p, layer, tabs_p, *params)
        y_s = _encoder_layer(y_s, mod_s, layer, tabs_s, *params)
    return (y_p, y_s)
```

```python
import functools
import math

import jax
import jax.numpy as jnp
from jax import lax
from jax.experimental import pallas as pl
from jax.experimental.pallas import tpu as pltpu

D_MODEL = 1024
A_HEADS = 4
A_QK_DIM = 64
A_V_DIM = 128
A_WIDTH = A_HEADS * A_V_DIM
A_ROPE_DIM = A_QK_DIM // 4
B_HEADS = 4
B_NOPE_DIM = 128
B_ROPE_DIM = 64
B_V_DIM = 128
B_WIDTH = B_HEADS * B_V_DIM
B_QK_PAD = 256
Q_LORA = 384
KV_LORA = 256
ROPE_THETA = 500000.0
NORM_EPS = 1e-6
LOG2E = 1.4426950408889634

LANES = 128
SUBLANES = 8
VMEM_LIMIT = 56 * 1024 * 1024

ROW_TILE = 512
KV_TILE = ROW_TILE
KV_UNROLL = 2
Q_TILE_A = 256
Q_TILE_B = 512
NORM_CHUNK = 1024

L_MIN = 2.0 ** -80

SEG_AQ = 0
SEG_AK = 512
SEG_AV = 1024
SEG_AG = 1536
SEG_CQ = 2048
SEG_CKV = 2432
SEG_BG = 2688
SEG_KR = 3200
IN_COLS_PAD = 3328


def _silu(x):
    return x * (1.0 / (1.0 + jnp.exp(-x)))


def _rms(x, g):
    return x * lax.rsqrt(jnp.mean(x * x, axis=-1, keepdims=True) + NORM_EPS) * g


def _bdot(a, b):
    return jnp.dot(a, b, preferred_element_type=jnp.float32)


def _dot_nt(a, b):
    return lax.dot_general(a, b, (((1,), (1,)), ((), ())), preferred_element_type=jnp.float32)


def _mod_kernel(c_ref, w_ref, b_ref, o_ref):
    c = _silu(c_ref[...]).astype(jnp.bfloat16)
    o_ref[...] = _bdot(c, w_ref[...].astype(jnp.bfloat16)) + b_ref[...]


def _adaln_mod(c_all, w_ada, b_ada):
    rows = c_all.shape[0]
    n_blk = 1024
    return pl.pallas_call(
        _mod_kernel,
        out_shape=jax.ShapeDtypeStruct((rows, 3 * D_MODEL), jnp.float32),
        grid=(3 * D_MODEL // n_blk,),
        in_specs=[
            pl.BlockSpec((rows, D_MODEL), lambda j: (0, 0)),
            pl.BlockSpec((D_MODEL, n_blk), lambda j: (0, j)),
            pl.BlockSpec((1, n_blk), lambda j: (0, j)),
        ],
        out_specs=pl.BlockSpec((rows, n_blk), lambda j: (0, j)),
        compiler_params=pltpu.CompilerParams(
            dimension_semantics=("arbitrary",), vmem_limit_bytes=VMEM_LIMIT),
        name="adaln_mod",
    )(c_all, w_ada, b_ada.reshape(1, -1))


def _rope(x, c, s_lo, s_hi, half):
    return (x * c + pltpu.roll(x, LANES - half, 1) * s_lo
            + pltpu.roll(x, half, 1) * s_hi)


def _in_proj_kernel(x_ref, mod_ref, gpre_ref, win_ref, gcq_ref, wuq_ref, gckv_ref, wukv_ref,
                    ca_ref, sa_lo_ref, sa_hi_ref, cb_ref, sb_lo_ref, sb_hi_ref,
                    qa_ref, ka_ref, va_ref, qb_ref, kb_ref, vb_ref, sg_ref,
                    *, scale_a, scale_b):
    x = x_ref[0]
    mod = mod_ref[0]
    shift = mod[:, 0:D_MODEL]
    scale = mod[:, D_MODEL:2 * D_MODEL]
    h = (_rms(x, gpre_ref[...]) * (1.0 + scale) + shift).astype(jnp.bfloat16)

    def proj(lo, width):
        return _bdot(h, win_ref[:, lo:lo + width])

    ca, sa_lo, sa_hi = ca_ref[...], sa_lo_ref[...], sa_hi_ref[...]
    cb, sb_lo, sb_hi = cb_ref[...], sb_lo_ref[...], sb_hi_ref[...]
    half_a = A_ROPE_DIM // 2
    half_b = B_ROPE_DIM // 2

    aq = proj(SEG_AQ, 512)
    ak = proj(SEG_AK, 512)
    av = proj(SEG_AV, 512)
    for hd in range(A_HEADS):
        sl = slice(hd * LANES, (hd + 1) * LANES)
        qa_ref[0, hd] = (_rope(aq[:, sl], ca, sa_lo, sa_hi, half_a) * scale_a).astype(jnp.bfloat16)
        ka_ref[0, hd] = _rope(ak[:, sl], ca, sa_lo, sa_hi, half_a).astype(jnp.bfloat16)
        va_ref[0, hd, 0] = av[:, sl].T.astype(jnp.bfloat16)

    sg_ref[0, :, 0:A_WIDTH] = _silu(proj(SEG_AG, A_WIDTH))
    sg_ref[0, :, A_WIDTH:A_WIDTH + B_WIDTH] = _silu(proj(SEG_BG, B_WIDTH))

    cq = _rms(proj(SEG_CQ, Q_LORA), gcq_ref[...]).astype(jnp.bfloat16)
    q = _bdot(cq, wuq_ref[...])
    ckv = _rms(proj(SEG_CKV, KV_LORA), gckv_ref[...]).astype(jnp.bfloat16)
    kv = _bdot(ckv, wukv_ref[...])
    kr = _rope(proj(SEG_KR, LANES), cb, sb_lo, sb_hi, half_b).astype(jnp.bfloat16)
    for hd in range(B_HEADS):
        qn = q[:, hd * LANES:(hd + 1) * LANES]
        qr = q[:, (B_HEADS + hd) * LANES:(B_HEADS + hd + 1) * LANES]
        qb_ref[0, hd, :, 0:LANES] = (qn * scale_b).astype(jnp.bfloat16)
        qb_ref[0, hd, :, LANES:2 * LANES] = (
            _rope(qr, cb, sb_lo, sb_hi, half_b) * scale_b).astype(jnp.bfloat16)
        kb_ref[0, hd, :, 0:LANES] = kv[:, 2 * hd * LANES:(2 * hd + 1) * LANES].astype(jnp.bfloat16)
        kb_ref[0, hd, :, LANES:2 * LANES] = kr
        vb_ref[0, hd, 0] = kv[:, (2 * hd + 1) * LANES:(2 * hd + 2) * LANES].T.astype(jnp.bfloat16)


def _in_proj(x, mod, gpre, win, gcq, wuq, gckv, wukv, tabs):
    b, s, _ = x.shape
    tr = ROW_TILE
    const = lambda shape: pl.BlockSpec(shape, lambda i, j: (0,) * len(shape))
    tab = pl.BlockSpec((tr, LANES), lambda i, j: (j, 0))
    head = lambda d: pl.BlockSpec((1, A_HEADS, tr, d), lambda i, j: (i, 0, j, 0))
    head_t = lambda d: pl.BlockSpec((1, A_HEADS, 1, d, tr), lambda i, j: (i, 0, j, 0, 0))
    bf = jnp.bfloat16
    kern = functools.partial(_in_proj_kernel,
                             scale_a=A_QK_DIM ** -0.5 * LOG2E,
                             scale_b=(B_NOPE_DIM + B_ROPE_DIM) ** -0.5 * LOG2E)
    return pl.pallas_call(
        kern,
        out_shape=(
            jax.ShapeDtypeStruct((b, A_HEADS, s, LANES), bf),
            jax.ShapeDtypeStruct((b, A_HEADS, s, LANES), bf),
            jax.ShapeDtypeStruct((b, A_HEADS, s // tr, A_V_DIM, tr), bf),
            jax.ShapeDtypeStruct((b, B_HEADS, s, B_QK_PAD), bf),
            jax.ShapeDtypeStruct((b, B_HEADS, s, B_QK_PAD), bf),
            jax.ShapeDtypeStruct((b, B_HEADS, s // tr, B_V_DIM, tr), bf),
            jax.ShapeDtypeStruct((b, s, A_WIDTH + B_WIDTH), jnp.float32),
        ),
        grid=(b, s // tr),
        in_specs=[
            pl.BlockSpec((1, tr, D_MODEL), lambda i, j: (i, j, 0)),
            pl.BlockSpec((1, 1, 3 * D_MODEL), lambda i, j: (i, 0, 0)),
            const((1, D_MODEL)),
            const((D_MODEL, IN_COLS_PAD)),
            const((1, Q_LORA)),
            const((Q_LORA, 2 * B_HEADS * LANES)),
            const((1, KV_LORA)),
            const((KV_LORA, B_HEADS * (B_NOPE_DIM + B_V_DIM))),
            tab, tab, tab, tab, tab, tab,
        ],
        out_specs=(head(LANES), head(LANES), head_t(A_V_DIM), head(B_QK_PAD), head(B_QK_PAD),
                   head_t(B_V_DIM),
                   pl.BlockSpec((1, tr, A_WIDTH + B_WIDTH), lambda i, j: (i, j, 0))),
        compiler_params=pltpu.CompilerParams(
            dimension_semantics=("arbitrary", "arbitrary"), vmem_limit_bytes=VMEM_LIMIT),
        name="in_proj",
    )(x, mod, gpre, win, gcq, wuq, gckv, wukv, *tabs)


def _key_tile(k_ref, t):
    return k_ref[0, 0, pl.ds(pl.multiple_of(t * KV_TILE, KV_TILE), KV_TILE), :]


def _key_norm_max(k_ref, lane_spans):
    s = k_ref.shape[2]
    d = k_ref.shape[3]

    def body(c, carry):
        kk = k_ref[0, 0, pl.ds(pl.multiple_of(c * NORM_CHUNK, NORM_CHUNK), NORM_CHUNK), :]
        ksq = kk.astype(jnp.float32)
        ksq = ksq * ksq
        lane = lax.broadcasted_iota(jnp.int32, ksq.shape, 1)
        out = []
        for (lo, hi), cur in zip(lane_spans, carry):
            part = ksq if (lo, hi) == (0, d) else jnp.where((lane >= lo) & (lane < hi), ksq, 0.0)
            n2 = jnp.sum(part, axis=1, keepdims=True)
            out.append(jnp.maximum(cur, jnp.max(n2, axis=0, keepdims=True)))
        return tuple(out)

    init = tuple(jnp.zeros((1, 1), jnp.float32) for _ in lane_spans)
    res = lax.fori_loop(0, s // NORM_CHUNK, body, init)
    return [jnp.broadcast_to(r, (SUBLANES, LANES)) for r in res]


def _col_bound(qeff, kmax_tiles, cols_per_map):
    qsq = qeff.astype(jnp.float32)
    qsq = (qsq * qsq).astype(jnp.bfloat16)
    ones = jnp.ones((2 * SUBLANES, qeff.shape[1]), jnp.bfloat16)
    q2 = _dot_nt(ones, qsq)[0:SUBLANES]
    kmax = jnp.concatenate(
        [t for t in kmax_tiles for _ in range(cols_per_map // LANES)], axis=1)
    return jnp.sqrt(q2 * kmax)


def _bound_shifted_pass(qeff, bound, k_ref, vt_ref, l_ref, acc_ref):
    n = qeff.shape[0]
    n_kv = k_ref.shape[2] // KV_TILE
    l_ref[...] = jnp.zeros(l_ref.shape, jnp.float32)
    acc_ref[...] = jnp.zeros(acc_ref.shape, jnp.float32)

    def body(j, carry):
        lsum = l_ref[...]
        ps, vs = [], []
        for u in range(KV_UNROLL):
            t = j * KV_UNROLL + u
            s = _dot_nt(_key_tile(k_ref, t), qeff)
            p = jnp.exp2(s.reshape(KV_TILE // SUBLANES, SUBLANES, n) - bound)
            lsum = lsum + jnp.sum(p, axis=0)
            ps.append(p.reshape(KV_TILE, n).astype(jnp.bfloat16))
            vs.append(vt_ref[0, 0, t])
        l_ref[...] = lsum
        acc_ref[...] += _bdot(jnp.concatenate(vs, axis=1), jnp.concatenate(ps, axis=0))
        return carry

    lax.fori_loop(0, n_kv // KV_UNROLL, body, 0)


def _running_max_pass(qeff, k_ref, vt_ref, m_ref, l_ref, acc_ref):
    n = qeff.shape[0]
    n_kv = k_ref.shape[2] // KV_TILE
    m_ref[...] = jnp.full(m_ref.shape, -jnp.inf, jnp.float32)
    l_ref[...] = jnp.zeros(l_ref.shape, jnp.float32)
    acc_ref[...] = jnp.zeros(acc_ref.shape, jnp.float32)

    def body(t, carry):
        s = _dot_nt(_key_tile(k_ref, t), qeff).reshape(KV_TILE // SUBLANES, SUBLANES, n)
        m_old = m_ref[...]
        m_col = jnp.max(jnp.max(s, axis=0), axis=0, keepdims=True)
        m_new = jnp.maximum(m_old, m_col)
        alpha = jnp.exp2(m_old - m_new)
        p = jnp.exp2(s - m_new)
        l_ref[...] = alpha * l_ref[...] + jnp.sum(p, axis=0)
        pv = _bdot(vt_ref[0, 0, t], p.reshape(KV_TILE, n).astype(jnp.bfloat16))
        acc_ref[...] = alpha[0:1] * acc_ref[...] + pv
        m_ref[...] = m_new
        return carry

    lax.fori_loop(0, n_kv, body, 0)


def _softmax_pv(qeff, lane_spans, cols_per_map, k_ref, vt_ref, kmax_ref, m_ref, l_ref, acc_ref):
    @pl.when(pl.program_id(2) == 0)
    def _():
        for idx, tile in enumerate(_key_norm_max(k_ref, lane_spans)):
            kmax_ref[idx] = tile

    bound = _col_bound(qeff, [kmax_ref[idx] for idx in range(len(lane_spans))], cols_per_map)
    _bound_shifted_pass(qeff, bound, k_ref, vt_ref, l_ref, acc_ref)
    l_min = jnp.min(jnp.sum(l_ref[...], axis=0, keepdims=True))

    @pl.when(jnp.logical_not(l_min >= L_MIN))
    def _():
        _running_max_pass(qeff, k_ref, vt_ref, m_ref, l_ref, acc_ref)

    return acc_ref[...], jnp.sum(l_ref[...], axis=0, keepdims=True)


def _attn_a_kernel(lam_ref, gsub_ref, q_ref, k_ref, vt_ref, sg_ref, o_ref,
                   kmax_ref, m_ref, l_ref, acc_ref, *, lambda_init):
    q = q_ref[0, 0]
    tq = q.shape[0]
    lane = lax.broadcasted_iota(jnp.int32, q.shape, 1)
    zero = jnp.zeros_like(q)
    qeff = jnp.concatenate(
        [jnp.where(lane < A_QK_DIM, q, zero), jnp.where(lane >= A_QK_DIM, q, zero)], axis=0)
    spans = ((0, A_QK_DIM), (A_QK_DIM, 2 * A_QK_DIM))
    acc, l = _softmax_pv(qeff, spans, tq, k_ref, vt_ref, kmax_ref, m_ref, l_ref, acc_ref)
    ot = acc * (1.0 / l)

    lam_v = lam_ref[...]
    lam = (jnp.exp(jnp.sum(lam_v[0:1] * lam_v[1:2], axis=-1, keepdims=True))
           - jnp.exp(jnp.sum(lam_v[2:3] * lam_v[3:4], axis=-1, keepdims=True))
           + lambda_init)
    o = (ot[:, 0:tq] - lam * ot[:, tq:2 * tq]).T
    o = _rms(o, gsub_ref[...]) * (1.0 - lambda_init)
    o_ref[0] = (o * sg_ref[0]).astype(o_ref.dtype)


def _attn_b_kernel(q_ref, k_ref, vt_ref, sg_ref, o_ref, kmax_ref, m_ref, l_ref, acc_ref):
    qeff = q_ref[0, 0]
    tq, d = qeff.shape
    acc, l = _softmax_pv(qeff, ((0, d),), tq, k_ref, vt_ref, kmax_ref, m_ref, l_ref, acc_ref)
    o = (acc * (1.0 / l)).T
    o_ref[0] = (o * sg_ref[0]).astype(o_ref.dtype)


def _attn_call(kern, name, extra_in, extra_specs, q, k, vt, sg, tq, n_maps, col0):
    b, nh, s, dq = q.shape
    dv = vt.shape[3]
    n = n_maps * tq
    return pl.pallas_call(
        kern,
        out_shape=jax.ShapeDtypeStruct((b, s, nh * dv), jnp.bfloat16),
        grid=(b, nh, s // tq),
        in_specs=extra_specs + [
            pl.BlockSpec((1, 1, tq, dq), lambda b_, h, i: (b_, h, i, 0)),
            pl.BlockSpec((1, 1, s, dq), lambda b_, h, i: (b_, h, 0, 0)),
            pl.BlockSpec((1, 1, s // KV_TILE, dv, KV_TILE), lambda b_, h, i: (b_, h, 0, 0, 0)),
            pl.BlockSpec((1, tq, dv), lambda b_, h, i: (b_, i, col0 + h)),
        ],
        out_specs=pl.BlockSpec((1, tq, dv), lambda b_, h, i: (b_, i, h)),
        scratch_shapes=[
            pltpu.VMEM((n_maps, SUBLANES, LANES), jnp.float32),
            pltpu.VMEM((SUBLANES, n), jnp.float32),
            pltpu.VMEM((SUBLANES, n), jnp.float32),
            pltpu.VMEM((dv, n), jnp.float32),
        ],
        compiler_params=pltpu.CompilerParams(
            dimension_semantics=("arbitrary", "arbitrary", "arbitrary"),
            vmem_limit_bytes=VMEM_LIMIT),
        name=name,
    )(*extra_in, q, k, vt, sg)


def _attn_a(lam_vecs, gsub, qa, ka, vta, sg, lambda_init):
    const = lambda shape: pl.BlockSpec(shape, lambda b_, h, i: (0, 0))
    return _attn_call(functools.partial(_attn_a_kernel, lambda_init=lambda_init), "attn_a",
                      (lam_vecs, gsub), [const((4, A_QK_DIM)), const((1, A_V_DIM))],
                      qa, ka, vta, sg, Q_TILE_A, 2, 0)


def _attn_b(qb, kb, vtb, sg):
    return _attn_call(_attn_b_kernel, "attn_b", (), [], qb, kb, vtb, sg, Q_TILE_B, 1, A_HEADS)


def _out_proj_kernel(ya_ref, yb_ref, x_ref, mod_ref, wout_ref, gpost_ref, o_ref):
    out = (_bdot(ya_ref[0], wout_ref[0:A_WIDTH, :])
           + _bdot(yb_ref[0], wout_ref[A_WIDTH:A_WIDTH + B_WIDTH, :]))
    gate = mod_ref[0][:, 2 * D_MODEL:3 * D_MODEL]
    o_ref[0] = x_ref[0] + gate * _rms(out, gpost_ref[...])


def _out_proj(ya, yb, x, mod, wout, gpost):
    b, s, _ = x.shape
    tr = ROW_TILE
    return pl.pallas_call(
        _out_proj_kernel,
        out_shape=jax.ShapeDtypeStruct(x.shape, x.dtype),
        grid=(b, s // tr),
        in_specs=[
            pl.BlockSpec((1, tr, A_WIDTH), lambda i, j: (i, j, 0)),
            pl.BlockSpec((1, tr, B_WIDTH), lambda i, j: (i, j, 0)),
            pl.BlockSpec((1, tr, D_MODEL), lambda i, j: (i, j, 0)),
            pl.BlockSpec((1, 1, 3 * D_MODEL), lambda i, j: (i, 0, 0)),
            pl.BlockSpec((A_WIDTH + B_WIDTH, D_MODEL), lambda i, j: (0, 0)),
            pl.BlockSpec((1, D_MODEL), lambda i, j: (0, 0)),
        ],
        out_specs=pl.BlockSpec((1, tr, D_MODEL), lambda i, j: (i, j, 0)),
        compiler_params=pltpu.CompilerParams(
            dimension_semantics=("arbitrary", "arbitrary"), vmem_limit_bytes=VMEM_LIMIT),
        name="out_proj",
    )(ya, yb, x, mod, wout, gpost)


def _rope_tables(seq, dim):
    half = dim // 2
    inv_freq = jnp.float32(ROPE_THETA) ** (-jnp.arange(0, dim, 2, dtype=jnp.float32) / dim)
    ang = jnp.arange(seq, dtype=jnp.float32)[:, None] * inv_freq[None, :]
    cos, sin = jnp.cos(ang), jnp.sin(ang)
    pad = 64 - dim
    ones = jnp.ones((seq, pad), jnp.float32)
    zeros = jnp.zeros((seq, pad), jnp.float32)
    zh = jnp.zeros((seq, half), jnp.float32)
    c = jnp.concatenate([cos, cos, ones], axis=1)
    s_lo = jnp.concatenate([-sin, zh, zeros], axis=1)
    s_hi = jnp.concatenate([zh, sin, zeros], axis=1)
    return tuple(jnp.concatenate([t, t], axis=1) for t in (c, s_lo, s_hi))


def _layout_w_in(w_in):
    aq, ak, av, ag, cq, ckv, kr, bg = jnp.split(
        w_in, [512, 1024, 1536, 2048, 2432, 2688, 2752], axis=1)
    kr = jnp.pad(kr, ((0, 0), (0, LANES - B_ROPE_DIM)))
    return jnp.concatenate([aq, ak, av, ag, cq, ckv, bg, kr], axis=1).astype(jnp.bfloat16)


def _layout_w_uq(w_uq):
    w = w_uq.reshape(Q_LORA, B_HEADS, B_NOPE_DIM + B_ROPE_DIM)
    nope = w[:, :, :B_NOPE_DIM].reshape(Q_LORA, B_HEADS * B_NOPE_DIM)
    rope = jnp.pad(w[:, :, B_NOPE_DIM:], ((0, 0), (0, 0), (0, LANES - B_ROPE_DIM)))
    return jnp.concatenate([nope, rope.reshape(Q_LORA, B_HEADS * LANES)], axis=1).astype(jnp.bfloat16)


def _encoder_layer(x, mod, layer, tabs, g_pre, w_in, lam_vecs, g_subln, g_cq, w_uq, g_ckv, w_ukv,
                   w_out, g_post):
    lambda_init = 0.8 - 0.6 * math.exp(-0.3 * layer)
    qa, ka, vta, qb, kb, vtb, sg = _in_proj(x, mod, g_pre, w_in, g_cq, w_uq, g_ckv, w_ukv, tabs)
    ya = _attn_a(lam_vecs, g_subln, qa, ka, vta, sg, lambda_init)
    yb = _attn_b(qb, kb, vtb, sg)
    return _out_proj(ya, yb, x, mod, w_out, g_post)


def kernel(x_prompt, x_sample, c_prompt, c_sample, w_ada, b_ada, g_pre, w_in, lambda_q1, lambda_k1,
           lambda_q2, lambda_k2, g_subln, g_cq, w_uq, g_ckv, w_ukv, w_out, g_post):
    depth = w_ada.shape[0]
    nb_p, nb_s = c_prompt.shape[0], c_sample.shape[0]
    pad_rows = -(nb_p + nb_s) % 8
    c_all = jnp.concatenate(
        [c_prompt, c_sample, jnp.zeros((pad_rows, D_MODEL), jnp.float32)], axis=0)
    tabs_p = _rope_tables(x_prompt.shape[1], A_ROPE_DIM) + _rope_tables(x_prompt.shape[1], B_ROPE_DIM)
    tabs_s = _rope_tables(x_sample.shape[1], A_ROPE_DIM) + _rope_tables(x_sample.shape[1], B_ROPE_DIM)
    y_p, y_s = x_prompt, x_sample
    for layer in range(depth):
        mod = _adaln_mod(c_all, w_ada[layer], b_ada[layer])
        mod_p = mod[:nb_p].reshape(nb_p, 1, 3 * D_MODEL)
        mod_s = mod[nb_p:nb_p + nb_s].reshape(nb_s, 1, 3 * D_MODEL)
        lam_vecs = jnp.stack([lambda_q1[layer], lambda_k1[layer], lambda_q2[layer], lambda_k2[layer]])
        params = (g_pre[layer].reshape(1, -1), _layout_w_in(w_in[layer]), lam_vecs,
                  g_subln[layer].reshape(1, -1), g_cq[layer].reshape(1, -1),
                  _layout_w_uq(w_uq[layer]), g_ckv[layer].reshape(1, -1),
                  w_ukv[layer].astype(jnp.bfloat16), w_out[layer].astype(jnp.bfloat16),
                  g_post[layer].reshape(1, -1))
        y_p = _encoder_layer(y_p, mod_p, layer, tabs_p, *params)
        y_s = _encoder_layer(y_s, mod_s, layer, tabs_s, *params)
    return (y_p, y_s)
```

```python
import functools
import math

import jax
import jax.numpy as jnp
from jax import lax
from jax.experimental import pallas as pl
from jax.experimental.pallas import tpu as pltpu

D_MODEL = 1024
A_HEADS = 4
A_QK_DIM = 64
A_V_DIM = 128
A_WIDTH = A_HEADS * A_V_DIM
A_ROPE_DIM = A_QK_DIM // 4
B_HEADS = 4
B_NOPE_DIM = 128
B_ROPE_DIM = 64
B_V_DIM = 128
B_WIDTH = B_HEADS * B_V_DIM
B_QK_PAD = 256
Q_LORA = 384
KV_LORA = 256
ROPE_THETA = 500000.0
NORM_EPS = 1e-6
LOG2E = 1.4426950408889634

LANES = 128
SUBLANES = 8
VMEM_LIMIT = 56 * 1024 * 1024

ROW_TILE = 512
KV_TILE = ROW_TILE
KV_UNROLL = 2
Q_TILE_A = 256
Q_TILE_B = 512
NORM_CHUNK = 1024

L_MIN = 2.0 ** -80
L_MAX = 2.0 ** 100
BOUND_MARGIN = 1.0 + 2.0 ** -5

SEG_AQ = 0
SEG_AK = 512
SEG_AV = 1024
SEG_AG = 1536
SEG_CQ = 2048
SEG_CKV = 2432
SEG_BG = 2688
SEG_KR = 3200
IN_COLS_PAD = 3328


def _silu(x):
    return x * (1.0 / (1.0 + jnp.exp(-x)))


def _rms(x, g):
    return x * lax.rsqrt(jnp.mean(x * x, axis=-1, keepdims=True) + NORM_EPS) * g


def _bdot(a, b):
    return jnp.dot(a, b, preferred_element_type=jnp.float32)


def _dot_nt(a, b):
    return lax.dot_general(a, b, (((1,), (1,)), ((), ())), preferred_element_type=jnp.float32)


def _mod_kernel(c_ref, w_ref, b_ref, o_ref):
    c = _silu(c_ref[...]).astype(jnp.bfloat16)
    o_ref[...] = _bdot(c, w_ref[...].astype(jnp.bfloat16)) + b_ref[...]


def _adaln_mod(c_all, w_ada, b_ada):
    rows = c_all.shape[0]
    n_blk = 1024
    return pl.pallas_call(
        _mod_kernel,
        out_shape=jax.ShapeDtypeStruct((rows, 3 * D_MODEL), jnp.float32),
        grid=(3 * D_MODEL // n_blk,),
        in_specs=[
            pl.BlockSpec((rows, D_MODEL), lambda j: (0, 0)),
            pl.BlockSpec((D_MODEL, n_blk), lambda j: (0, j)),
            pl.BlockSpec((1, n_blk), lambda j: (0, j)),
        ],
        out_specs=pl.BlockSpec((rows, n_blk), lambda j: (0, j)),
        compiler_params=pltpu.CompilerParams(
            dimension_semantics=("arbitrary",), vmem_limit_bytes=VMEM_LIMIT),
        name="adaln_mod",
    )(c_all, w_ada, b_ada.reshape(1, -1))


def _rope(x, c, s_lo, s_hi, half):
    return (x * c + pltpu.roll(x, LANES - half, 1) * s_lo
            + pltpu.roll(x, half, 1) * s_hi)


def _in_proj_kernel(x_ref, mod_ref, gpre_ref, win_ref, gcq_ref, wuq_ref, gckv_ref, wukv_ref,
                    ca_ref, sa_lo_ref, sa_hi_ref, cb_ref, sb_lo_ref, sb_hi_ref,
                    qa_ref, ka_ref, va_ref, qb_ref, kb_ref, vb_ref, sg_ref,
                    *, scale_a, scale_b):
    x = x_ref[0]
    mod = mod_ref[0]
    shift = mod[:, 0:D_MODEL]
    scale = mod[:, D_MODEL:2 * D_MODEL]
    h = (_rms(x, gpre_ref[...]) * (1.0 + scale) + shift).astype(jnp.bfloat16)

    def proj(lo, width):
        return _bdot(h, win_ref[:, lo:lo + width])

    ca, sa_lo, sa_hi = ca_ref[...], sa_lo_ref[...], sa_hi_ref[...]
    cb, sb_lo, sb_hi = cb_ref[...], sb_lo_ref[...], sb_hi_ref[...]
    half_a = A_ROPE_DIM // 2
    half_b = B_ROPE_DIM // 2

    aq = proj(SEG_AQ, 512)
    ak = proj(SEG_AK, 512)
    av = proj(SEG_AV, 512)
    for hd in range(A_HEADS):
        sl = slice(hd * LANES, (hd + 1) * LANES)
        qa_ref[0, hd] = (_rope(aq[:, sl], ca, sa_lo, sa_hi, half_a) * scale_a).astype(jnp.bfloat16)
        ka_ref[0, hd] = _rope(ak[:, sl], ca, sa_lo, sa_hi, half_a).astype(jnp.bfloat16)
        va_ref[0, hd, 0] = av[:, sl].T.astype(jnp.bfloat16)

    sg_ref[0, :, 0:A_WIDTH] = _silu(proj(SEG_AG, A_WIDTH))
    sg_ref[0, :, A_WIDTH:A_WIDTH + B_WIDTH] = _silu(proj(SEG_BG, B_WIDTH))

    cq = _rms(proj(SEG_CQ, Q_LORA), gcq_ref[...]).astype(jnp.bfloat16)
    q = _bdot(cq, wuq_ref[...])
    ckv = _rms(proj(SEG_CKV, KV_LORA), gckv_ref[...]).astype(jnp.bfloat16)
    kv = _bdot(ckv, wukv_ref[...])
    kr = _rope(proj(SEG_KR, LANES), cb, sb_lo, sb_hi, half_b).astype(jnp.bfloat16)
    for hd in range(B_HEADS):
        qn = q[:, hd * LANES:(hd + 1) * LANES]
        qr = q[:, (B_HEADS + hd) * LANES:(B_HEADS + hd + 1) * LANES]
        qb_ref[0, hd, :, 0:LANES] = (qn * scale_b).astype(jnp.bfloat16)
        qb_ref[0, hd, :, LANES:2 * LANES] = (
            _rope(qr, cb, sb_lo, sb_hi, half_b) * scale_b).astype(jnp.bfloat16)
        kb_ref[0, hd, :, 0:LANES] = kv[:, 2 * hd * LANES:(2 * hd + 1) * LANES].astype(jnp.bfloat16)
        kb_ref[0, hd, :, LANES:2 * LANES] = kr
        vb_ref[0, hd, 0] = kv[:, (2 * hd + 1) * LANES:(2 * hd + 2) * LANES].T.astype(jnp.bfloat16)


def _in_proj(x, mod, gpre, win, gcq, wuq, gckv, wukv, tabs):
    b, s, _ = x.shape
    tr = ROW_TILE
    const = lambda shape: pl.BlockSpec(shape, lambda i, j: (0,) * len(shape))
    tab = pl.BlockSpec((tr, LANES), lambda i, j: (j, 0))
    head = lambda d: pl.BlockSpec((1, A_HEADS, tr, d), lambda i, j: (i, 0, j, 0))
    head_t = lambda d: pl.BlockSpec((1, A_HEADS, 1, d, tr), lambda i, j: (i, 0, j, 0, 0))
    bf = jnp.bfloat16
    kern = functools.partial(_in_proj_kernel,
                             scale_a=A_QK_DIM ** -0.5 * LOG2E,
                             scale_b=(B_NOPE_DIM + B_ROPE_DIM) ** -0.5 * LOG2E)
    return pl.pallas_call(
        kern,
        out_shape=(
            jax.ShapeDtypeStruct((b, A_HEADS, s, LANES), bf),
            jax.ShapeDtypeStruct((b, A_HEADS, s, LANES), bf),
            jax.ShapeDtypeStruct((b, A_HEADS, s // tr, A_V_DIM, tr), bf),
            jax.ShapeDtypeStruct((b, B_HEADS, s, B_QK_PAD), bf),
            jax.ShapeDtypeStruct((b, B_HEADS, s, B_QK_PAD), bf),
            jax.ShapeDtypeStruct((b, B_HEADS, s // tr, B_V_DIM, tr), bf),
            jax.ShapeDtypeStruct((b, s, A_WIDTH + B_WIDTH), jnp.float32),
        ),
        grid=(b, s // tr),
        in_specs=[
            pl.BlockSpec((1, tr, D_MODEL), lambda i, j: (i, j, 0)),
            pl.BlockSpec((1, 1, 3 * D_MODEL), lambda i, j: (i, 0, 0)),
            const((1, D_MODEL)),
            const((D_MODEL, IN_COLS_PAD)),
            const((1, Q_LORA)),
            const((Q_LORA, 2 * B_HEADS * LANES)),
            const((1, KV_LORA)),
            const((KV_LORA, B_HEADS * (B_NOPE_DIM + B_V_DIM))),
            tab, tab, tab, tab, tab, tab,
        ],
        out_specs=(head(LANES), head(LANES), head_t(A_V_DIM), head(B_QK_PAD), head(B_QK_PAD),
                   head_t(B_V_DIM),
                   pl.BlockSpec((1, tr, A_WIDTH + B_WIDTH), lambda i, j: (i, j, 0))),
        compiler_params=pltpu.CompilerParams(
            dimension_semantics=("arbitrary", "arbitrary"), vmem_limit_bytes=VMEM_LIMIT),
        name="in_proj",
    )(x, mod, gpre, win, gcq, wuq, gckv, wukv, *tabs)


def _key_tile(k_ref, t):
    return k_ref[0, 0, pl.ds(pl.multiple_of(t * KV_TILE, KV_TILE), KV_TILE), :]


def _key_norm_max(k_ref, lane_spans):
    s = k_ref.shape[2]
    d = k_ref.shape[3]

    def body(c, carry):
        kk = k_ref[0, 0, pl.ds(pl.multiple_of(c * NORM_CHUNK, NORM_CHUNK), NORM_CHUNK), :]
        ksq = kk.astype(jnp.float32)
        ksq = ksq * ksq
        lane = lax.broadcasted_iota(jnp.int32, ksq.shape, 1)
        out = []
        for (lo, hi), cur in zip(lane_spans, carry):
            part = ksq if (lo, hi) == (0, d) else jnp.where((lane >= lo) & (lane < hi), ksq, 0.0)
            n2 = jnp.sum(part, axis=1, keepdims=True)
            out.append(jnp.maximum(cur, jnp.max(n2, axis=0, keepdims=True)))
        return tuple(out)

    init = tuple(jnp.zeros((1, 1), jnp.float32) for _ in lane_spans)
    res = lax.fori_loop(0, s // NORM_CHUNK, body, init)
    return [jnp.broadcast_to(r, (SUBLANES, LANES)) for r in res]


def _col_bound(qeff, kmax_tiles, cols_per_map):
    qsq = qeff.astype(jnp.float32)
    qsq = (qsq * qsq).astype(jnp.bfloat16)
    ones = jnp.ones((2 * SUBLANES, qeff.shape[1]), jnp.bfloat16)
    q2 = _dot_nt(ones, qsq)[0:SUBLANES]
    kmax = jnp.concatenate(
        [t for t in kmax_tiles for _ in range(cols_per_map // LANES)], axis=1)
    return jnp.sqrt(q2 * kmax) * BOUND_MARGIN


def _bound_shifted_pass(qeff, bound, k_ref, vt_ref, pa_ref, pb_ref, l_ref, acc_ref):
    n = qeff.shape[0]
    n_grp = k_ref.shape[2] // (KV_TILE * KV_UNROLL)
    assert n_grp % 2 == 0 and n_grp >= 2

    def numerators(g, p_ref):
        lsum = l_ref[...]
        for u in range(KV_UNROLL):
            s = _dot_nt(_key_tile(k_ref, g * KV_UNROLL + u), qeff)
            p = jnp.exp2(s.reshape(KV_TILE // SUBLANES, SUBLANES, n) - bound)
            lsum = lsum + jnp.sum(p, axis=0)
            p_ref[u * KV_TILE:(u + 1) * KV_TILE, :] = p.reshape(KV_TILE, n).astype(jnp.bfloat16)
        l_ref[...] = lsum

    def weighted_values(g, p_ref):
        vt = jnp.concatenate([vt_ref[0, 0, g * KV_UNROLL + u] for u in range(KV_UNROLL)], axis=1)
        acc_ref[...] += _bdot(vt, p_ref[...])

    l_ref[...] = jnp.zeros(l_ref.shape, jnp.float32)
    acc_ref[...] = jnp.zeros(acc_ref.shape, jnp.float32)
    numerators(0, pa_ref)

    def body(i, carry):
        g = 2 * i
        numerators(g + 1, pb_ref)
        weighted_values(g, pa_ref)
        numerators(g + 2, pa_ref)
        weighted_values(g + 1, pb_ref)
        return carry

    lax.fori_loop(0, n_grp // 2 - 1, body, 0)
    numerators(n_grp - 1, pb_ref)
    weighted_values(n_grp - 2, pa_ref)
    weighted_values(n_grp - 1, pb_ref)


def _running_max_pass(qeff, k_ref, vt_ref, m_ref, l_ref, acc_ref):
    n = qeff.shape[0]
    n_kv = k_ref.shape[2] // KV_TILE
    m_ref[...] = jnp.full(m_ref.shape, -jnp.inf, jnp.float32)
    l_ref[...] = jnp.zeros(l_ref.shape, jnp.float32)
    acc_ref[...] = jnp.zeros(acc_ref.shape, jnp.float32)

    def body(t, carry):
        s = _dot_nt(_key_tile(k_ref, t), qeff).reshape(KV_TILE // SUBLANES, SUBLANES, n)
        m_old = m_ref[...]
        m_col = jnp.max(jnp.max(s, axis=0), axis=0, keepdims=True)
        m_new = jnp.maximum(m_old, m_col)
        alpha = jnp.exp2(m_old - m_new)
        p = jnp.exp2(s - m_new)
        l_ref[...] = alpha * l_ref[...] + jnp.sum(p, axis=0)
        pv = _bdot(vt_ref[0, 0, t], p.reshape(KV_TILE, n).astype(jnp.bfloat16))
        acc_ref[...] = alpha[0:1] * acc_ref[...] + pv
        m_ref[...] = m_new
        return carry

    lax.fori_loop(0, n_kv, body, 0)


def _softmax_pv(qeff, lane_spans, cols_per_map, k_ref, vt_ref, kmax_ref, pa_ref, pb_ref,
                m_ref, l_ref, acc_ref):
    @pl.when(pl.program_id(2) == 0)
    def _():
        for idx, tile in enumerate(_key_norm_max(k_ref, lane_spans)):
            kmax_ref[idx] = tile

    bound = _col_bound(qeff, [kmax_ref[idx] for idx in range(len(lane_spans))], cols_per_map)
    _bound_shifted_pass(qeff, bound, k_ref, vt_ref, pa_ref, pb_ref, l_ref, acc_ref)
    l_col = jnp.sum(l_ref[...], axis=0, keepdims=True)
    usable = (jnp.min(l_col) >= L_MIN) & (jnp.max(l_col) <= L_MAX)

    @pl.when(jnp.logical_not(usable))
    def _():
        _running_max_pass(qeff, k_ref, vt_ref, m_ref, l_ref, acc_ref)

    return acc_ref[...], jnp.sum(l_ref[...], axis=0, keepdims=True)


def _attn_a_kernel(lam_ref, gsub_ref, q_ref, k_ref, vt_ref, sg_ref, o_ref,
                   kmax_ref, pa_ref, pb_ref, m_ref, l_ref, acc_ref, *, lambda_init):
    q = q_ref[0, 0]
    tq = q.shape[0]
    lane = lax.broadcasted_iota(jnp.int32, q.shape, 1)
    zero = jnp.zeros_like(q)
    qeff = jnp.concatenate(
        [jnp.where(lane < A_QK_DIM, q, zero), jnp.where(lane >= A_QK_DIM, q, zero)], axis=0)
    spans = ((0, A_QK_DIM), (A_QK_DIM, 2 * A_QK_DIM))
    acc, l = _softmax_pv(qeff, spans, tq, k_ref, vt_ref, kmax_ref, pa_ref, pb_ref,
                         m_ref, l_ref, acc_ref)
    ot = acc * (1.0 / l)

    lam_v = lam_ref[...]
    lam = (jnp.exp(jnp.sum(lam_v[0:1] * lam_v[1:2], axis=-1, keepdims=True))
           - jnp.exp(jnp.sum(lam_v[2:3] * lam_v[3:4], axis=-1, keepdims=True))
           + lambda_init)
    o = (ot[:, 0:tq] - lam * ot[:, tq:2 * tq]).T
    o = _rms(o, gsub_ref[...]) * (1.0 - lambda_init)
    o_ref[0] = (o * sg_ref[0]).astype(o_ref.dtype)


def _attn_b_kernel(q_ref, k_ref, vt_ref, sg_ref, o_ref,
                   kmax_ref, pa_ref, pb_ref, m_ref, l_ref, acc_ref):
    qeff = q_ref[0, 0]
    tq, d = qeff.shape
    acc, l = _softmax_pv(qeff, ((0, d),), tq, k_ref, vt_ref, kmax_ref, pa_ref, pb_ref,
                         m_ref, l_ref, acc_ref)
    o = (acc * (1.0 / l)).T
    o_ref[0] = (o * sg_ref[0]).astype(o_ref.dtype)


def _attn_call(kern, name, extra_in, extra_specs, q, k, vt, sg, tq, n_maps, col0):
    b, nh, s, dq = q.shape
    dv = vt.shape[3]
    n = n_maps * tq
    return pl.pallas_call(
        kern,
        out_shape=jax.ShapeDtypeStruct((b, s, nh * dv), jnp.bfloat16),
        grid=(b, nh, s // tq),
        in_specs=extra_specs + [
            pl.BlockSpec((1, 1, tq, dq), lambda b_, h, i: (b_, h, i, 0)),
            pl.BlockSpec((1, 1, s, dq), lambda b_, h, i: (b_, h, 0, 0)),
            pl.BlockSpec((1, 1, s // KV_TILE, dv, KV_TILE), lambda b_, h, i: (b_, h, 0, 0, 0)),
            pl.BlockSpec((1, tq, dv), lambda b_, h, i: (b_, i, col0 + h)),
        ],
        out_specs=pl.BlockSpec((1, tq, dv), lambda b_, h, i: (b_, i, h)),
        scratch_shapes=[
            pltpu.VMEM((n_maps, SUBLANES, LANES), jnp.float32),
            pltpu.VMEM((KV_UNROLL * KV_TILE, n), jnp.bfloat16),
            pltpu.VMEM((KV_UNROLL * KV_TILE, n), jnp.bfloat16),
            pltpu.VMEM((SUBLANES, n), jnp.float32),
            pltpu.VMEM((SUBLANES, n), jnp.float32),
            pltpu.VMEM((dv, n), jnp.float32),
        ],
        compiler_params=pltpu.CompilerParams(
            dimension_semantics=("arbitrary", "arbitrary", "arbitrary"),
            vmem_limit_bytes=VMEM_LIMIT),
        name=name,
    )(*extra_in, q, k, vt, sg)


def _attn_a(lam_vecs, gsub, qa, ka, vta, sg, lambda_init):
    const = lambda shape: pl.BlockSpec(shape, lambda b_, h, i: (0, 0))
    return _attn_call(functools.partial(_attn_a_kernel, lambda_init=lambda_init), "attn_a",
                      (lam_vecs, gsub), [const((4, A_QK_DIM)), const((1, A_V_DIM))],
                      qa, ka, vta, sg, Q_TILE_A, 2, 0)


def _attn_b(qb, kb, vtb, sg):
    return _attn_call(_attn_b_kernel, "attn_b", (), [], qb, kb, vtb, sg, Q_TILE_B, 1, A_HEADS)


def _out_proj_kernel(ya_ref, yb_ref, x_ref, mod_ref, wout_ref, gpost_ref, o_ref):
    out = (_bdot(ya_ref[0], wout_ref[0:A_WIDTH, :])
           + _bdot(yb_ref[0], wout_ref[A_WIDTH:A_WIDTH + B_WIDTH, :]))
    gate = mod_ref[0][:, 2 * D_MODEL:3 * D_MODEL]
    o_ref[0] = x_ref[0] + gate * _rms(out, gpost_ref[...])


def _out_proj(ya, yb, x, mod, wout, gpost):
    b, s, _ = x.shape
    tr = ROW_TILE
    return pl.pallas_call(
        _out_proj_kernel,
        out_shape=jax.ShapeDtypeStruct(x.shape, x.dtype),
        grid=(b, s // tr),
        in_specs=[
            pl.BlockSpec((1, tr, A_WIDTH), lambda i, j: (i, j, 0)),
            pl.BlockSpec((1, tr, B_WIDTH), lambda i, j: (i, j, 0)),
            pl.BlockSpec((1, tr, D_MODEL), lambda i, j: (i, j, 0)),
            pl.BlockSpec((1, 1, 3 * D_MODEL), lambda i, j: (i, 0, 0)),
            pl.BlockSpec((A_WIDTH + B_WIDTH, D_MODEL), lambda i, j: (0, 0)),
            pl.BlockSpec((1, D_MODEL), lambda i, j: (0, 0)),
        ],
        out_specs=pl.BlockSpec((1, tr, D_MODEL), lambda i, j: (i, j, 0)),
        compiler_params=pltpu.CompilerParams(
            dimension_semantics=("arbitrary", "arbitrary"), vmem_limit_bytes=VMEM_LIMIT),
        name="out_proj",
    )(ya, yb, x, mod, wout, gpost)


def _rope_tables(seq, dim):
    half = dim // 2
    inv_freq = jnp.float32(ROPE_THETA) ** (-jnp.arange(0, dim, 2, dtype=jnp.float32) / dim)
    ang = jnp.arange(seq, dtype=jnp.float32)[:, None] * inv_freq[None, :]
    cos, sin = jnp.cos(ang), jnp.sin(ang)
    pad = 64 - dim
    ones = jnp.ones((seq, pad), jnp.float32)
    zeros = jnp.zeros((seq, pad), jnp.float32)
    zh = jnp.zeros((seq, half), jnp.float32)
    c = jnp.concatenate([cos, cos, ones], axis=1)
    s_lo = jnp.concatenate([-sin, zh, zeros], axis=1)
    s_hi = jnp.concatenate([zh, sin, zeros], axis=1)
    return tuple(jnp.concatenate([t, t], axis=1) for t in (c, s_lo, s_hi))


def _layout_w_in(w_in):
    aq, ak, av, ag, cq, ckv, kr, bg = jnp.split(
        w_in, [512, 1024, 1536, 2048, 2432, 2688, 2752], axis=1)
    kr = jnp.pad(kr, ((0, 0), (0, LANES - B_ROPE_DIM)))
    return jnp.concatenate([aq, ak, av, ag, cq, ckv, bg, kr], axis=1).astype(jnp.bfloat16)


def _layout_w_uq(w_uq):
    w = w_uq.reshape(Q_LORA, B_HEADS, B_NOPE_DIM + B_ROPE_DIM)
    nope = w[:, :, :B_NOPE_DIM].reshape(Q_LORA, B_HEADS * B_NOPE_DIM)
    rope = jnp.pad(w[:, :, B_NOPE_DIM:], ((0, 0), (0, 0), (0, LANES - B_ROPE_DIM)))
    return jnp.concatenate([nope, rope.reshape(Q_LORA, B_HEADS * LANES)], axis=1).astype(jnp.bfloat16)


def _encoder_layer(x, mod, layer, tabs, g_pre, w_in, lam_vecs, g_subln, g_cq, w_uq, g_ckv, w_ukv,
                   w_out, g_post):
    lambda_init = 0.8 - 0.6 * math.exp(-0.3 * layer)
    qa, ka, vta, qb, kb, vtb, sg = _in_proj(x, mod, g_pre, w_in, g_cq, w_uq, g_ckv, w_ukv, tabs)
    ya = _attn_a(lam_vecs, g_subln, qa, ka, vta, sg, lambda_init)
    yb = _attn_b(qb, kb, vtb, sg)
    return _out_proj(ya, yb, x, mod, w_out, g_post)


def kernel(x_prompt, x_sample, c_prompt, c_sample, w_ada, b_ada, g_pre, w_in, lambda_q1, lambda_k1,
           lambda_q2, lambda_k2, g_subln, g_cq, w_uq, g_ckv, w_ukv, w_out, g_post):
    depth = w_ada.shape[0]
    nb_p, nb_s = c_prompt.shape[0], c_sample.shape[0]
    pad_rows = -(nb_p + nb_s) % 8
    c_all = jnp.concatenate(
        [c_prompt, c_sample, jnp.zeros((pad_rows, D_MODEL), jnp.float32)], axis=0)
    tabs_p = _rope_tables(x_prompt.shape[1], A_ROPE_DIM) + _rope_tables(x_prompt.shape[1], B_ROPE_DIM)
    tabs_s = _rope_tables(x_sample.shape[1], A_ROPE_DIM) + _rope_tables(x_sample.shape[1], B_ROPE_DIM)
    y_p, y_s = x_prompt, x_sample
    for layer in range(depth):
        mod = _adaln_mod(c_all, w_ada[layer], b_ada[layer])
        mod_p = mod[:nb_p].reshape(nb_p, 1, 3 * D_MODEL)
        mod_s = mod[nb_p:nb_p + nb_s].reshape(nb_s, 1, 3 * D_MODEL)
        lam_vecs = jnp.stack([lambda_q1[layer], lambda_k1[layer], lambda_q2[layer], lambda_k2[layer]])
        params = (g_pre[layer].reshape(1, -1), _layout_w_in(w_in[layer]), lam_vecs,
                  g_subln[layer].reshape(1, -1), g_cq[layer].reshape(1, -1),
                  _layout_w_uq(w_uq[layer]), g_ckv[layer].reshape(1, -1),
                  w_ukv[layer].astype(jnp.bfloat16), w_out[layer].astype(jnp.bfloat16),
                  g_post[layer].reshape(1, -1))
        y_p = _encoder_layer(y_p, mod_p, layer, tabs_p, *params)
        y_s = _encoder_layer(y_s, mod_s, layer, tabs_s, *params)
    return (y_p, y_s)
```

```python
import functools
import math

import jax
import jax.numpy as jnp
from jax import lax
from jax.experimental import pallas as pl
from jax.experimental.pallas import tpu as pltpu

D_MODEL = 1024
A_HEADS = 4
A_QK_DIM = 64
A_V_DIM = 128
A_WIDTH = A_HEADS * A_V_DIM
A_ROPE_DIM = A_QK_DIM // 4
B_HEADS = 4
B_NOPE_DIM = 128
B_ROPE_DIM = 64
B_V_DIM = 128
B_WIDTH = B_HEADS * B_V_DIM
B_QK_PAD = 256
Q_LORA = 384
KV_LORA = 256
ROPE_THETA = 500000.0
NORM_EPS = 1e-6
LOG2E = 1.4426950408889634

LANES = 128
SUBLANES = 8
VMEM_LIMIT = 56 * 1024 * 1024

ROW_TILE = 512
KV_TILE = ROW_TILE
KV_UNROLL = 2
Q_TILE_A = 256
Q_TILE_B = 512
Q_CHUNK = 2048
PIPE_UNROLL = 8
FINISH_UNROLL = 2
NORM_CHUNK = 1024

L_MIN = 2.0 ** -80
L_MAX = 2.0 ** 100
BOUND_MARGIN = 1.0 + 2.0 ** -5

SEG_AQ = 0
SEG_AK = 512
SEG_AV = 1024
SEG_AG = 1536
SEG_CQ = 2048
SEG_CKV = 2432
SEG_BG = 2688
SEG_KR = 3200
IN_COLS_PAD = 3328


def _silu(x):
    return x * (1.0 / (1.0 + jnp.exp(-x)))


def _rms(x, g):
    return x * lax.rsqrt(jnp.mean(x * x, axis=-1, keepdims=True) + NORM_EPS) * g


def _bdot(a, b):
    return jnp.dot(a, b, preferred_element_type=jnp.float32)


def _dot_nt(a, b):
    return lax.dot_general(a, b, (((1,), (1,)), ((), ())), preferred_element_type=jnp.float32)


def _mod_kernel(c_ref, w_ref, b_ref, o_ref):
    c = _silu(c_ref[...]).astype(jnp.bfloat16)
    o_ref[...] = _bdot(c, w_ref[...].astype(jnp.bfloat16)) + b_ref[...]


def _adaln_mod(c_all, w_ada, b_ada):
    rows = c_all.shape[0]
    n_blk = 1024
    return pl.pallas_call(
        _mod_kernel,
        out_shape=jax.ShapeDtypeStruct((rows, 3 * D_MODEL), jnp.float32),
        grid=(3 * D_MODEL // n_blk,),
        in_specs=[
            pl.BlockSpec((rows, D_MODEL), lambda j: (0, 0)),
            pl.BlockSpec((D_MODEL, n_blk), lambda j: (0, j)),
            pl.BlockSpec((1, n_blk), lambda j: (0, j)),
        ],
        out_specs=pl.BlockSpec((rows, n_blk), lambda j: (0, j)),
        compiler_params=pltpu.CompilerParams(
            dimension_semantics=("arbitrary",), vmem_limit_bytes=VMEM_LIMIT),
        name="adaln_mod",
    )(c_all, w_ada, b_ada.reshape(1, -1))


def _rope(x, c, s_lo, s_hi, half):
    return (x * c + pltpu.roll(x, LANES - half, 1) * s_lo
            + pltpu.roll(x, half, 1) * s_hi)


def _in_proj_kernel(x_ref, mod_ref, gpre_ref, win_ref, gcq_ref, wuq_ref, gckv_ref, wukv_ref,
                    ca_ref, sa_lo_ref, sa_hi_ref, cb_ref, sb_lo_ref, sb_hi_ref,
                    qa_ref, ka_ref, va_ref, qb_ref, kb_ref, vb_ref, sg_ref,
                    *, scale_a, scale_b):
    x = x_ref[0]
    mod = mod_ref[0]
    shift = mod[:, 0:D_MODEL]
    scale = mod[:, D_MODEL:2 * D_MODEL]
    h = (_rms(x, gpre_ref[...]) * (1.0 + scale) + shift).astype(jnp.bfloat16)

    def proj(lo, width):
        return _bdot(h, win_ref[:, lo:lo + width])

    ca, sa_lo, sa_hi = ca_ref[...], sa_lo_ref[...], sa_hi_ref[...]
    cb, sb_lo, sb_hi = cb_ref[...], sb_lo_ref[...], sb_hi_ref[...]
    half_a = A_ROPE_DIM // 2
    half_b = B_ROPE_DIM // 2

    aq = proj(SEG_AQ, 512)
    ak = proj(SEG_AK, 512)
    av = proj(SEG_AV, 512)
    for hd in range(A_HEADS):
        sl = slice(hd * LANES, (hd + 1) * LANES)
        qa_ref[0, hd] = (_rope(aq[:, sl], ca, sa_lo, sa_hi, half_a) * scale_a).astype(jnp.bfloat16)
        ka_ref[0, hd] = _rope(ak[:, sl], ca, sa_lo, sa_hi, half_a).astype(jnp.bfloat16)
        va_ref[0, hd, 0] = av[:, sl].T.astype(jnp.bfloat16)

    sg_ref[0, :, 0:A_WIDTH] = _silu(proj(SEG_AG, A_WIDTH))
    sg_ref[0, :, A_WIDTH:A_WIDTH + B_WIDTH] = _silu(proj(SEG_BG, B_WIDTH))

    cq = _rms(proj(SEG_CQ, Q_LORA), gcq_ref[...]).astype(jnp.bfloat16)
    q = _bdot(cq, wuq_ref[...])
    ckv = _rms(proj(SEG_CKV, KV_LORA), gckv_ref[...]).astype(jnp.bfloat16)
    kv = _bdot(ckv, wukv_ref[...])
    kr = _rope(proj(SEG_KR, LANES), cb, sb_lo, sb_hi, half_b).astype(jnp.bfloat16)
    for hd in range(B_HEADS):
        qn = q[:, hd * LANES:(hd + 1) * LANES]
        qr = q[:, (B_HEADS + hd) * LANES:(B_HEADS + hd + 1) * LANES]
        qb_ref[0, hd, :, 0:LANES] = (qn * scale_b).astype(jnp.bfloat16)
        qb_ref[0, hd, :, LANES:2 * LANES] = (
            _rope(qr, cb, sb_lo, sb_hi, half_b) * scale_b).astype(jnp.bfloat16)
        kb_ref[0, hd, :, 0:LANES] = kv[:, 2 * hd * LANES:(2 * hd + 1) * LANES].astype(jnp.bfloat16)
        kb_ref[0, hd, :, LANES:2 * LANES] = kr
        vb_ref[0, hd, 0] = kv[:, (2 * hd + 1) * LANES:(2 * hd + 2) * LANES].T.astype(jnp.bfloat16)


def _in_proj(x, mod, gpre, win, gcq, wuq, gckv, wukv, tabs):
    b, s, _ = x.shape
    tr = ROW_TILE
    const = lambda shape: pl.BlockSpec(shape, lambda i, j: (0,) * len(shape))
    tab = pl.BlockSpec((tr, LANES), lambda i, j: (j, 0))
    head = lambda d: pl.BlockSpec((1, A_HEADS, tr, d), lambda i, j: (i, 0, j, 0))
    head_t = lambda d: pl.BlockSpec((1, A_HEADS, 1, d, tr), lambda i, j: (i, 0, j, 0, 0))
    bf = jnp.bfloat16
    kern = functools.partial(_in_proj_kernel,
                             scale_a=A_QK_DIM ** -0.5 * LOG2E,
                             scale_b=(B_NOPE_DIM + B_ROPE_DIM) ** -0.5 * LOG2E)
    return pl.pallas_call(
        kern,
        out_shape=(
            jax.ShapeDtypeStruct((b, A_HEADS, s, LANES), bf),
            jax.ShapeDtypeStruct((b, A_HEADS, s, LANES), bf),
            jax.ShapeDtypeStruct((b, A_HEADS, s // tr, A_V_DIM, tr), bf),
            jax.ShapeDtypeStruct((b, B_HEADS, s, B_QK_PAD), bf),
            jax.ShapeDtypeStruct((b, B_HEADS, s, B_QK_PAD), bf),
            jax.ShapeDtypeStruct((b, B_HEADS, s // tr, B_V_DIM, tr), bf),
            jax.ShapeDtypeStruct((b, s, A_WIDTH + B_WIDTH), jnp.float32),
        ),
        grid=(b, s // tr),
        in_specs=[
            pl.BlockSpec((1, tr, D_MODEL), lambda i, j: (i, j, 0)),
            pl.BlockSpec((1, 1, 3 * D_MODEL), lambda i, j: (i, 0, 0)),
            const((1, D_MODEL)),
            const((D_MODEL, IN_COLS_PAD)),
            const((1, Q_LORA)),
            const((Q_LORA, 2 * B_HEADS * LANES)),
            const((1, KV_LORA)),
            const((KV_LORA, B_HEADS * (B_NOPE_DIM + B_V_DIM))),
            tab, tab, tab, tab, tab, tab,
        ],
        out_specs=(head(LANES), head(LANES), head_t(A_V_DIM), head(B_QK_PAD), head(B_QK_PAD),
                   head_t(B_V_DIM),
                   pl.BlockSpec((1, tr, A_WIDTH + B_WIDTH), lambda i, j: (i, j, 0))),
        compiler_params=pltpu.CompilerParams(
            dimension_semantics=("arbitrary", "arbitrary"), vmem_limit_bytes=VMEM_LIMIT),
        name="in_proj",
    )(x, mod, gpre, win, gcq, wuq, gckv, wukv, *tabs)


def _key_tile(k_ref, t):
    return k_ref[0, 0, pl.ds(pl.multiple_of(t * KV_TILE, KV_TILE), KV_TILE), :]


def _key_norm_max(k_ref, lane_spans):
    s = k_ref.shape[2]
    d = k_ref.shape[3]

    def body(c, carry):
        kk = k_ref[0, 0, pl.ds(pl.multiple_of(c * NORM_CHUNK, NORM_CHUNK), NORM_CHUNK), :]
        ksq = kk.astype(jnp.float32)
        ksq = ksq * ksq
        lane = lax.broadcasted_iota(jnp.int32, ksq.shape, 1)
        out = []
        for (lo, hi), cur in zip(lane_spans, carry):
            part = ksq if (lo, hi) == (0, d) else jnp.where((lane >= lo) & (lane < hi), ksq, 0.0)
            n2 = jnp.sum(part, axis=1, keepdims=True)
            out.append(jnp.maximum(cur, jnp.max(n2, axis=0, keepdims=True)))
        return tuple(out)

    init = tuple(jnp.zeros((1, 1), jnp.float32) for _ in lane_spans)
    res = lax.fori_loop(0, s // NORM_CHUNK, body, init)
    return [jnp.broadcast_to(r, (SUBLANES, LANES)) for r in res]


def _col_bound(qeff, kmax_tiles, cols_per_map):
    qsq = qeff.astype(jnp.float32)
    qsq = (qsq * qsq).astype(jnp.bfloat16)
    ones = jnp.ones((2 * SUBLANES, qeff.shape[1]), jnp.bfloat16)
    q2 = _dot_nt(ones, qsq)[0:SUBLANES]
    kmax = jnp.concatenate(
        [t for t in kmax_tiles for _ in range(cols_per_map // LANES)], axis=1)
    return jnp.sqrt(q2 * kmax) * BOUND_MARGIN


def _running_max_pass(qeff, k_ref, vt_ref, m_ref, l_ref, acc_ref):
    n = qeff.shape[0]
    n_kv = k_ref.shape[2] // KV_TILE
    m_ref[...] = jnp.full(m_ref.shape, -jnp.inf, jnp.float32)
    l_ref[...] = jnp.zeros(l_ref.shape, jnp.float32)
    acc_ref[...] = jnp.zeros(acc_ref.shape, jnp.float32)

    def body(t, carry):
        s = _dot_nt(_key_tile(k_ref, t), qeff).reshape(KV_TILE // SUBLANES, SUBLANES, n)
        m_old = m_ref[...]
        m_col = jnp.max(jnp.max(s, axis=0), axis=0, keepdims=True)
        m_new = jnp.maximum(m_old, m_col)
        alpha = jnp.exp2(m_old - m_new)
        p = jnp.exp2(s - m_new)
        l_ref[...] = alpha * l_ref[...] + jnp.sum(p, axis=0)
        pv = _bdot(vt_ref[0, 0, t], p.reshape(KV_TILE, n).astype(jnp.bfloat16))
        acc_ref[...] = alpha[0:1] * acc_ref[...] + pv
        m_ref[...] = m_new
        return carry

    lax.fori_loop(0, n_kv, body, 0)


def _attention_chunk(q_tile, finish, lane_spans, tq, k_ref, vt_ref,
                     kmax_ref, bound_ref, pa_ref, pb_ref, m_ref, l_ref, acc_ref):
    nq, _, n = bound_ref.shape
    n_grp = k_ref.shape[2] // (KV_TILE * KV_UNROLL)
    total = n_grp * nq
    log_nq = nq.bit_length() - 1
    assert nq == 1 << log_nq and total % 2 == 0 and total >= 2

    @pl.when(pl.program_id(2) == 0)
    def _():
        for idx, tile in enumerate(_key_norm_max(k_ref, lane_spans)):
            kmax_ref[idx] = tile

    kmax_tiles = [kmax_ref[idx] for idx in range(len(lane_spans))]
    for i in range(nq):
        bound_ref[i] = _col_bound(q_tile(i), kmax_tiles, tq)
    l_ref[...] = jnp.zeros(l_ref.shape, jnp.float32)
    acc_ref[...] = jnp.zeros(acc_ref.shape, jnp.float32)

    def numerators(step, p_ref):
        g, i = step >> log_nq, step & (nq - 1)
        qeff = q_tile(i)
        bound = bound_ref[i]
        lsum = l_ref[i]
        for u in range(KV_UNROLL):
            s = _dot_nt(_key_tile(k_ref, g * KV_UNROLL + u), qeff)
            p = jnp.exp2(s.reshape(KV_TILE // SUBLANES, SUBLANES, n) - bound)
            lsum = lsum + jnp.sum(p, axis=0)
            p_ref[u * KV_TILE:(u + 1) * KV_TILE, :] = p.reshape(KV_TILE, n).astype(jnp.bfloat16)
        l_ref[i] = lsum

    def weighted_values(step, p_ref):
        g, i = step >> log_nq, step & (nq - 1)
        vt = jnp.concatenate([vt_ref[0, 0, g * KV_UNROLL + u] for u in range(KV_UNROLL)], axis=1)
        acc_ref[i] += _bdot(vt, p_ref[...])

    p_refs = (pa_ref, pb_ref)

    def pipeline_step(step, parity):
        numerators(step + 1, p_refs[1 - parity])
        weighted_values(step, p_refs[parity])

    numerators(0, pa_ref)
    n_body = (total - 1) // PIPE_UNROLL

    def body(t, carry):
        for h in range(PIPE_UNROLL):
            pipeline_step(PIPE_UNROLL * t + h, h % 2)
        return carry

    lax.fori_loop(0, n_body, body, 0)
    for step in range(n_body * PIPE_UNROLL, total - 1):
        pipeline_step(step, step % 2)
    weighted_values(total - 1, p_refs[(total - 1) % 2])

    def recompute_if_unusable(i, carry):
        l_col = jnp.sum(l_ref[i], axis=0, keepdims=True)
        usable = (jnp.min(l_col) >= L_MIN) & (jnp.max(l_col) <= L_MAX)

        @pl.when(jnp.logical_not(usable))
        def _():
            _running_max_pass(q_tile(i), k_ref, vt_ref, m_ref, l_ref.at[i], acc_ref.at[i])

        return carry

    lax.fori_loop(0, nq, recompute_if_unusable, 0)

    def finish_tiles(t, carry):
        for h in range(FINISH_UNROLL):
            i = FINISH_UNROLL * t + h
            finish(i, acc_ref[i], jnp.sum(l_ref[i], axis=0, keepdims=True))
        return carry

    assert nq % FINISH_UNROLL == 0
    lax.fori_loop(0, nq // FINISH_UNROLL, finish_tiles, 0)


def _rows(i, tq):
    return pl.ds(pl.multiple_of(i * tq, tq), tq)


def _attn_a_kernel(lam_ref, gsub_ref, q_ref, k_ref, vt_ref, sg_ref, o_ref,
                   kmax_ref, bound_ref, pa_ref, pb_ref, m_ref, l_ref, acc_ref, *, lambda_init):
    tq = Q_TILE_A

    def q_tile(i):
        q = q_ref[0, 0, _rows(i, tq), :]
        lane = lax.broadcasted_iota(jnp.int32, q.shape, 1)
        zero = jnp.zeros_like(q)
        return jnp.concatenate(
            [jnp.where(lane < A_QK_DIM, q, zero), jnp.where(lane >= A_QK_DIM, q, zero)], axis=0)

    lam_v = lam_ref[...]
    lam = (jnp.exp(jnp.sum(lam_v[0:1] * lam_v[1:2], axis=-1, keepdims=True))
           - jnp.exp(jnp.sum(lam_v[2:3] * lam_v[3:4], axis=-1, keepdims=True))
           + lambda_init)

    def finish(i, acc, l):
        ot = acc * (1.0 / l)
        o = (ot[:, 0:tq] - lam * ot[:, tq:2 * tq]).T
        o = _rms(o, gsub_ref[...]) * (1.0 - lambda_init)
        o_ref[0, _rows(i, tq), :] = (o * sg_ref[0, _rows(i, tq), :]).astype(o_ref.dtype)

    spans = ((0, A_QK_DIM), (A_QK_DIM, 2 * A_QK_DIM))
    _attention_chunk(q_tile, finish, spans, tq, k_ref, vt_ref,
                     kmax_ref, bound_ref, pa_ref, pb_ref, m_ref, l_ref, acc_ref)


def _attn_b_kernel(q_ref, k_ref, vt_ref, sg_ref, o_ref,
                   kmax_ref, bound_ref, pa_ref, pb_ref, m_ref, l_ref, acc_ref):
    tq = Q_TILE_B

    def q_tile(i):
        return q_ref[0, 0, _rows(i, tq), :]

    def finish(i, acc, l):
        o = (acc * (1.0 / l)).T
        o_ref[0, _rows(i, tq), :] = (o * sg_ref[0, _rows(i, tq), :]).astype(o_ref.dtype)

    _attention_chunk(q_tile, finish, ((0, q_ref.shape[3]),), tq, k_ref, vt_ref,
                     kmax_ref, bound_ref, pa_ref, pb_ref, m_ref, l_ref, acc_ref)


def _attn_call(kern, name, extra_in, extra_specs, q, k, vt, sg, tq, n_maps, col0):
    b, nh, s, dq = q.shape
    dv = vt.shape[3]
    n = n_maps * tq
    qc = Q_CHUNK
    nq = qc // tq
    return pl.pallas_call(
        kern,
        out_shape=jax.ShapeDtypeStruct((b, s, nh * dv), jnp.bfloat16),
        grid=(b, nh, s // qc),
        in_specs=extra_specs + [
            pl.BlockSpec((1, 1, qc, dq), lambda b_, h, c: (b_, h, c, 0)),
            pl.BlockSpec((1, 1, s, dq), lambda b_, h, c: (b_, h, 0, 0)),
            pl.BlockSpec((1, 1, s // KV_TILE, dv, KV_TILE), lambda b_, h, c: (b_, h, 0, 0, 0)),
            pl.BlockSpec((1, qc, dv), lambda b_, h, c: (b_, c, col0 + h)),
        ],
        out_specs=pl.BlockSpec((1, qc, dv), lambda b_, h, c: (b_, c, h)),
        scratch_shapes=[
            pltpu.VMEM((n_maps, SUBLANES, LANES), jnp.float32),
            pltpu.VMEM((nq, SUBLANES, n), jnp.float32),
            pltpu.VMEM((KV_UNROLL * KV_TILE, n), jnp.bfloat16),
            pltpu.VMEM((KV_UNROLL * KV_TILE, n), jnp.bfloat16),
            pltpu.VMEM((SUBLANES, n), jnp.float32),
            pltpu.VMEM((nq, SUBLANES, n), jnp.float32),
            pltpu.VMEM((nq, dv, n), jnp.float32),
        ],
        compiler_params=pltpu.CompilerParams(
            dimension_semantics=("arbitrary", "arbitrary", "arbitrary"),
            vmem_limit_bytes=VMEM_LIMIT),
        name=name,
    )(*extra_in, q, k, vt, sg)


def _attn_a(lam_vecs, gsub, qa, ka, vta, sg, lambda_init):
    const = lambda shape: pl.BlockSpec(shape, lambda b_, h, c: (0, 0))
    return _attn_call(functools.partial(_attn_a_kernel, lambda_init=lambda_init), "attn_a",
                      (lam_vecs, gsub), [const((4, A_QK_DIM)), const((1, A_V_DIM))],
                      qa, ka, vta, sg, Q_TILE_A, 2, 0)


def _attn_b(qb, kb, vtb, sg):
    return _attn_call(_attn_b_kernel, "attn_b", (), [], qb, kb, vtb, sg, Q_TILE_B, 1, A_HEADS)


def _out_proj_kernel(ya_ref, yb_ref, x_ref, mod_ref, wout_ref, gpost_ref, o_ref):
    out = (_bdot(ya_ref[0], wout_ref[0:A_WIDTH, :])
           + _bdot(yb_ref[0], wout_ref[A_WIDTH:A_WIDTH + B_WIDTH, :]))
    gate = mod_ref[0][:, 2 * D_MODEL:3 * D_MODEL]
    o_ref[0] = x_ref[0] + gate * _rms(out, gpost_ref[...])


def _out_proj(ya, yb, x, mod, wout, gpost):
    b, s, _ = x.shape
    tr = ROW_TILE
    return pl.pallas_call(
        _out_proj_kernel,
        out_shape=jax.ShapeDtypeStruct(x.shape, x.dtype),
        grid=(b, s // tr),
        in_specs=[
            pl.BlockSpec((1, tr, A_WIDTH), lambda i, j: (i, j, 0)),
            pl.BlockSpec((1, tr, B_WIDTH), lambda i, j: (i, j, 0)),
            pl.BlockSpec((1, tr, D_MODEL), lambda i, j: (i, j, 0)),
            pl.BlockSpec((1, 1, 3 * D_MODEL), lambda i, j: (i, 0, 0)),
            pl.BlockSpec((A_WIDTH + B_WIDTH, D_MODEL), lambda i, j: (0, 0)),
            pl.BlockSpec((1, D_MODEL), lambda i, j: (0, 0)),
        ],
        out_specs=pl.BlockSpec((1, tr, D_MODEL), lambda i, j: (i, j, 0)),
        compiler_params=pltpu.CompilerParams(
            dimension_semantics=("arbitrary", "arbitrary"), vmem_limit_bytes=VMEM_LIMIT),
        name="out_proj",
    )(ya, yb, x, mod, wout, gpost)


def _rope_tables(seq, dim):
    half = dim // 2
    inv_freq = jnp.float32(ROPE_THETA) ** (-jnp.arange(0, dim, 2, dtype=jnp.float32) / dim)
    ang = jnp.arange(seq, dtype=jnp.float32)[:, None] * inv_freq[None, :]
    cos, sin = jnp.cos(ang), jnp.sin(ang)
    pad = 64 - dim
    ones = jnp.ones((seq, pad), jnp.float32)
    zeros = jnp.zeros((seq, pad), jnp.float32)
    zh = jnp.zeros((seq, half), jnp.float32)
    c = jnp.concatenate([cos, cos, ones], axis=1)
    s_lo = jnp.concatenate([-sin, zh, zeros], axis=1)
    s_hi = jnp.concatenate([zh, sin, zeros], axis=1)
    return tuple(jnp.concatenate([t, t], axis=1) for t in (c, s_lo, s_hi))


def _layout_w_in(w_in):
    aq, ak, av, ag, cq, ckv, kr, bg = jnp.split(
        w_in, [512, 1024, 1536, 2048, 2432, 2688, 2752], axis=1)
    kr = jnp.pad(kr, ((0, 0), (0, LANES - B_ROPE_DIM)))
    return jnp.concatenate([aq, ak, av, ag, cq, ckv, bg, kr], axis=1).astype(jnp.bfloat16)


def _layout_w_uq(w_uq):
    w = w_uq.reshape(Q_LORA, B_HEADS, B_NOPE_DIM + B_ROPE_DIM)
    nope = w[:, :, :B_NOPE_DIM].reshape(Q_LORA, B_HEADS * B_NOPE_DIM)
    rope = jnp.pad(w[:, :, B_NOPE_DIM:], ((0, 0), (0, 0), (0, LANES - B_ROPE_DIM)))
    return jnp.concatenate([nope, rope.reshape(Q_LORA, B_HEADS * LANES)], axis=1).astype(jnp.bfloat16)


def _encoder_layer(x, mod, layer, tabs, g_pre, w_in, lam_vecs, g_subln, g_cq, w_uq, g_ckv, w_ukv,
                   w_out, g_post):
    lambda_init = 0.8 - 0.6 * math.exp(-0.3 * layer)
    qa, ka, vta, qb, kb, vtb, sg = _in_proj(x, mod, g_pre, w_in, g_cq, w_uq, g_ckv, w_ukv, tabs)
    ya = _attn_a(lam_vecs, g_subln, qa, ka, vta, sg, lambda_init)
    yb = _attn_b(qb, kb, vtb, sg)
    return _out_proj(ya, yb, x, mod, w_out, g_post)


def kernel(x_prompt, x_sample, c_prompt, c_sample, w_ada, b_ada, g_pre, w_in, lambda_q1, lambda_k1,
           lambda_q2, lambda_k2, g_subln, g_cq, w_uq, g_ckv, w_ukv, w_out, g_post):
    depth = w_ada.shape[0]
    nb_p, nb_s = c_prompt.shape[0], c_sample.shape[0]
    pad_rows = -(nb_p + nb_s) % 8
    c_all = jnp.concatenate(
        [c_prompt, c_sample, jnp.zeros((pad_rows, D_MODEL), jnp.float32)], axis=0)
    tabs_p = _rope_tables(x_prompt.shape[1], A_ROPE_DIM) + _rope_tables(x_prompt.shape[1], B_ROPE_DIM)
    tabs_s = _rope_tables(x_sample.shape[1], A_ROPE_DIM) + _rope_tables(x_sample.shape[1], B_ROPE_DIM)
    y_p, y_s = x_prompt, x_sample
    for layer in range(depth):
        mod = _adaln_mod(c_all, w_ada[layer], b_ada[layer])
        mod_p = mod[:nb_p].reshape(nb_p, 1, 3 * D_MODEL)
        mod_s = mod[nb_p:nb_p + nb_s].reshape(nb_s, 1, 3 * D_MODEL)
        lam_vecs = jnp.stack([lambda_q1[layer], lambda_k1[layer], lambda_q2[layer], lambda_k2[layer]])
        params = (g_pre[layer].reshape(1, -1), _layout_w_in(w_in[layer]), lam_vecs,
                  g_subln[layer].reshape(1, -1), g_cq[layer].reshape(1, -1),
                  _layout_w_uq(w_uq[layer]), g_ckv[layer].reshape(1, -1),
                  w_ukv[layer].astype(jnp.bfloat16), w_out[layer].astype(jnp.bfloat16),
                  g_post[layer].reshape(1, -1))
        y_p = _encoder_layer(y_p, mod_p, layer, tabs_p, *params)
        y_s = _encoder_layer(y_s, mod_s, layer, tabs_s, *params)
    return (y_p, y_s)
```

```python
import functools
import math

import jax
import jax.numpy as jnp
from jax import lax
from jax.experimental import pallas as pl
from jax.experimental.pallas import tpu as pltpu

D_MODEL = 1024
A_HEADS = 4
A_QK_DIM = 64
A_V_DIM = 128
A_WIDTH = A_HEADS * A_V_DIM
A_ROPE_DIM = A_QK_DIM // 4
B_HEADS = 4
B_NOPE_DIM = 128
B_ROPE_DIM = 64
B_V_DIM = 128
B_WIDTH = B_HEADS * B_V_DIM
B_QK_PAD = 256
Q_LORA = 384
KV_LORA = 256
ROPE_THETA = 500000.0
NORM_EPS = 1e-6
LOG2E = 1.4426950408889634

LANES = 128
SUBLANES = 8
VMEM_LIMIT = 56 * 1024 * 1024

ROW_TILE = 512
KV_TILE = ROW_TILE
KV_UNROLL = 2
Q_TILE_A = 256
Q_TILE_B = 512
Q_CHUNK = 4096
PIPE_UNROLL = 8
FINISH_UNROLL = 2
NORM_CHUNK = 1024

L_MIN = 2.0 ** -80
L_MAX = 2.0 ** 100
BOUND_MARGIN = 1.0 + 2.0 ** -5

SEG_AQ = 0
SEG_AK = 512
SEG_AV = 1024
SEG_AG = 1536
SEG_CQ = 2048
SEG_CKV = 2432
SEG_BG = 2688
SEG_KR = 3200
IN_COLS_PAD = 3328


def _silu(x):
    return x * (1.0 / (1.0 + jnp.exp(-x)))


def _rms(x, g):
    return x * lax.rsqrt(jnp.mean(x * x, axis=-1, keepdims=True) + NORM_EPS) * g


def _bdot(a, b):
    return jnp.dot(a, b, preferred_element_type=jnp.float32)


def _mod_kernel(c_ref, w_ref, b_ref, o_ref):
    c = _silu(c_ref[...]).astype(jnp.bfloat16)
    o_ref[...] = _bdot(c, w_ref[...].astype(jnp.bfloat16)) + b_ref[...]


def _adaln_mod(c_all, w_ada, b_ada):
    rows = c_all.shape[0]
    n_blk = 1024
    return pl.pallas_call(
        _mod_kernel,
        out_shape=jax.ShapeDtypeStruct((rows, 3 * D_MODEL), jnp.float32),
        grid=(3 * D_MODEL // n_blk,),
        in_specs=[
            pl.BlockSpec((rows, D_MODEL), lambda j: (0, 0)),
            pl.BlockSpec((D_MODEL, n_blk), lambda j: (0, j)),
            pl.BlockSpec((1, n_blk), lambda j: (0, j)),
        ],
        out_specs=pl.BlockSpec((rows, n_blk), lambda j: (0, j)),
        compiler_params=pltpu.CompilerParams(
            dimension_semantics=("arbitrary",), vmem_limit_bytes=VMEM_LIMIT),
        name="adaln_mod",
    )(c_all, w_ada, b_ada.reshape(1, -1))


def _rope(x, c, s_lo, s_hi, half):
    return (x * c + pltpu.roll(x, LANES - half, 1) * s_lo
            + pltpu.roll(x, half, 1) * s_hi)


def _in_proj_kernel(x_ref, mod_ref, gpre_ref, win_ref, gcq_ref, wuq_ref, gckv_ref, wukv_ref,
                    ca_ref, sa_lo_ref, sa_hi_ref, cb_ref, sb_lo_ref, sb_hi_ref,
                    qa_ref, ka_ref, va_ref, qb_ref, kb_ref, vb_ref, sg_ref,
                    *, scale_a, scale_b):
    x = x_ref[0]
    mod = mod_ref[0]
    shift = mod[:, 0:D_MODEL]
    scale = mod[:, D_MODEL:2 * D_MODEL]
    h = (_rms(x, gpre_ref[...]) * (1.0 + scale) + shift).astype(jnp.bfloat16)

    def proj(lo, width):
        return _bdot(h, win_ref[:, lo:lo + width])

    ca, sa_lo, sa_hi = ca_ref[...], sa_lo_ref[...], sa_hi_ref[...]
    cb, sb_lo, sb_hi = cb_ref[...], sb_lo_ref[...], sb_hi_ref[...]
    half_a = A_ROPE_DIM // 2
    half_b = B_ROPE_DIM // 2

    aq = proj(SEG_AQ, 512)
    ak = proj(SEG_AK, 512)
    av = proj(SEG_AV, 512)
    for hd in range(A_HEADS):
        sl = slice(hd * LANES, (hd + 1) * LANES)
        qt = (_rope(aq[:, sl], ca, sa_lo, sa_hi, half_a) * scale_a).T.astype(jnp.bfloat16)
        for blk in range(ROW_TILE // Q_TILE_A):
            qa_ref[0, hd, blk] = qt[:, blk * Q_TILE_A:(blk + 1) * Q_TILE_A]
        ka_ref[0, hd] = _rope(ak[:, sl], ca, sa_lo, sa_hi, half_a).astype(jnp.bfloat16)
        va_ref[0, hd, 0] = av[:, sl].T.astype(jnp.bfloat16)

    sg_ref[0, :, 0:A_WIDTH] = _silu(proj(SEG_AG, A_WIDTH))
    sg_ref[0, :, A_WIDTH:A_WIDTH + B_WIDTH] = _silu(proj(SEG_BG, B_WIDTH))

    cq = _rms(proj(SEG_CQ, Q_LORA), gcq_ref[...]).astype(jnp.bfloat16)
    q = _bdot(cq, wuq_ref[...])
    ckv = _rms(proj(SEG_CKV, KV_LORA), gckv_ref[...]).astype(jnp.bfloat16)
    kv = _bdot(ckv, wukv_ref[...])
    kr = _rope(proj(SEG_KR, LANES), cb, sb_lo, sb_hi, half_b).astype(jnp.bfloat16)
    for hd in range(B_HEADS):
        qn = q[:, hd * LANES:(hd + 1) * LANES]
        qr = q[:, (B_HEADS + hd) * LANES:(B_HEADS + hd + 1) * LANES]
        qb_ref[0, hd, 0, 0:LANES, :] = (qn * scale_b).T.astype(jnp.bfloat16)
        qb_ref[0, hd, 0, LANES:2 * LANES, :] = (
            _rope(qr, cb, sb_lo, sb_hi, half_b) * scale_b).T.astype(jnp.bfloat16)
        kb_ref[0, hd, :, 0:LANES] = kv[:, 2 * hd * LANES:(2 * hd + 1) * LANES].astype(jnp.bfloat16)
        kb_ref[0, hd, :, LANES:2 * LANES] = kr
        vb_ref[0, hd, 0] = kv[:, (2 * hd + 1) * LANES:(2 * hd + 2) * LANES].T.astype(jnp.bfloat16)


def _in_proj(x, mod, gpre, win, gcq, wuq, gckv, wukv, tabs):
    b, s, _ = x.shape
    tr = ROW_TILE
    const = lambda shape: pl.BlockSpec(shape, lambda i, j: (0,) * len(shape))
    tab = pl.BlockSpec((tr, LANES), lambda i, j: (j, 0))
    head = lambda d: pl.BlockSpec((1, A_HEADS, tr, d), lambda i, j: (i, 0, j, 0))
    head_t = lambda d: pl.BlockSpec((1, A_HEADS, 1, d, tr), lambda i, j: (i, 0, j, 0, 0))
    qa_blocks = tr // Q_TILE_A
    assert tr == Q_TILE_B and tr == qa_blocks * Q_TILE_A
    bf = jnp.bfloat16
    kern = functools.partial(_in_proj_kernel,
                             scale_a=A_QK_DIM ** -0.5 * LOG2E,
                             scale_b=(B_NOPE_DIM + B_ROPE_DIM) ** -0.5 * LOG2E)
    return pl.pallas_call(
        kern,
        out_shape=(
            jax.ShapeDtypeStruct((b, A_HEADS, s // Q_TILE_A, LANES, Q_TILE_A), bf),
            jax.ShapeDtypeStruct((b, A_HEADS, s, LANES), bf),
            jax.ShapeDtypeStruct((b, A_HEADS, s // tr, A_V_DIM, tr), bf),
            jax.ShapeDtypeStruct((b, B_HEADS, s // tr, B_QK_PAD, tr), bf),
            jax.ShapeDtypeStruct((b, B_HEADS, s, B_QK_PAD), bf),
            jax.ShapeDtypeStruct((b, B_HEADS, s // tr, B_V_DIM, tr), bf),
            jax.ShapeDtypeStruct((b, s, A_WIDTH + B_WIDTH), jnp.float32),
        ),
        grid=(b, s // tr),
        in_specs=[
            pl.BlockSpec((1, tr, D_MODEL), lambda i, j: (i, j, 0)),
            pl.BlockSpec((1, 1, 3 * D_MODEL), lambda i, j: (i, 0, 0)),
            const((1, D_MODEL)),
            const((D_MODEL, IN_COLS_PAD)),
            const((1, Q_LORA)),
            const((Q_LORA, 2 * B_HEADS * LANES)),
            const((1, KV_LORA)),
            const((KV_LORA, B_HEADS * (B_NOPE_DIM + B_V_DIM))),
            tab, tab, tab, tab, tab, tab,
        ],
        out_specs=(pl.BlockSpec((1, A_HEADS, qa_blocks, LANES, Q_TILE_A),
                                lambda i, j: (i, 0, j, 0, 0)),
                   head(LANES), head_t(A_V_DIM), head_t(B_QK_PAD), head(B_QK_PAD),
                   head_t(B_V_DIM),
                   pl.BlockSpec((1, tr, A_WIDTH + B_WIDTH), lambda i, j: (i, j, 0))),
        compiler_params=pltpu.CompilerParams(
            dimension_semantics=("arbitrary", "arbitrary"), vmem_limit_bytes=VMEM_LIMIT),
        name="in_proj",
    )(x, mod, gpre, win, gcq, wuq, gckv, wukv, *tabs)


def _key_tile(k_ref, t):
    return k_ref[0, 0, pl.ds(pl.multiple_of(t * KV_TILE, KV_TILE), KV_TILE), :]


def _key_norm_max(k_ref, lane_spans):
    s = k_ref.shape[2]
    d = k_ref.shape[3]

    def body(c, carry):
        kk = k_ref[0, 0, pl.ds(pl.multiple_of(c * NORM_CHUNK, NORM_CHUNK), NORM_CHUNK), :]
        ksq = kk.astype(jnp.float32)
        ksq = ksq * ksq
        lane = lax.broadcasted_iota(jnp.int32, ksq.shape, 1)
        out = []
        for (lo, hi), cur in zip(lane_spans, carry):
            part = ksq if (lo, hi) == (0, d) else jnp.where((lane >= lo) & (lane < hi), ksq, 0.0)
            n2 = jnp.sum(part, axis=1, keepdims=True)
            out.append(jnp.maximum(cur, jnp.max(n2, axis=0, keepdims=True)))
        return tuple(out)

    init = tuple(jnp.zeros((1, 1), jnp.float32) for _ in lane_spans)
    res = lax.fori_loop(0, s // NORM_CHUNK, body, init)
    return [jnp.broadcast_to(r, (SUBLANES, LANES)) for r in res]


def _col_bound(qt, kmax_tiles, cols_per_map):
    qsq = qt.astype(jnp.float32)
    qsq = (qsq * qsq).astype(jnp.bfloat16)
    ones = jnp.ones((2 * SUBLANES, qt.shape[0]), jnp.bfloat16)
    q2 = _bdot(ones, qsq)[0:SUBLANES]
    kmax = jnp.concatenate(
        [t for t in kmax_tiles for _ in range(cols_per_map // LANES)], axis=1)
    return jnp.sqrt(q2 * kmax) * BOUND_MARGIN


def _running_max_pass(qt, k_ref, vt_ref, m_ref, l_ref, acc_ref):
    n = qt.shape[1]
    n_kv = k_ref.shape[2] // KV_TILE
    m_ref[...] = jnp.full(m_ref.shape, -jnp.inf, jnp.float32)
    l_ref[...] = jnp.zeros(l_ref.shape, jnp.float32)
    acc_ref[...] = jnp.zeros(acc_ref.shape, jnp.float32)

    def body(t, carry):
        s = _bdot(_key_tile(k_ref, t), qt).reshape(KV_TILE // SUBLANES, SUBLANES, n)
        m_old = m_ref[...]
        m_col = jnp.max(jnp.max(s, axis=0), axis=0, keepdims=True)
        m_new = jnp.maximum(m_old, m_col)
        alpha = jnp.exp2(m_old - m_new)
        p = jnp.exp2(s - m_new)
        l_ref[...] = alpha * l_ref[...] + jnp.sum(p, axis=0)
        pv = _bdot(vt_ref[0, 0, t], p.reshape(KV_TILE, n).astype(jnp.bfloat16))
        acc_ref[...] = alpha[0:1] * acc_ref[...] + pv
        m_ref[...] = m_new
        return carry

    lax.fori_loop(0, n_kv, body, 0)


def _attention_chunk(q_tile, finish, lane_spans, tq, k_ref, vt_ref,
                     kmax_ref, bound_ref, pa_ref, pb_ref, m_ref, l_ref, acc_ref):
    nq, _, n = bound_ref.shape
    n_grp = k_ref.shape[2] // (KV_TILE * KV_UNROLL)
    total = n_grp * nq
    log_nq = nq.bit_length() - 1
    assert nq == 1 << log_nq and total % 2 == 0 and total >= 2

    @pl.when(pl.program_id(2) == 0)
    def _():
        for idx, tile in enumerate(_key_norm_max(k_ref, lane_spans)):
            kmax_ref[idx] = tile

    kmax_tiles = [kmax_ref[idx] for idx in range(len(lane_spans))]
    for i in range(nq):
        bound_ref[i] = _col_bound(q_tile(i), kmax_tiles, tq)
    l_ref[...] = jnp.zeros(l_ref.shape, jnp.float32)
    acc_ref[...] = jnp.zeros(acc_ref.shape, jnp.float32)

    def numerators(step, p_ref):
        g, i = step >> log_nq, step & (nq - 1)
        qt = q_tile(i)
        bound = bound_ref[i]
        lsum = l_ref[i]
        for u in range(KV_UNROLL):
            s = _bdot(_key_tile(k_ref, g * KV_UNROLL + u), qt)
            p = jnp.exp2(s.reshape(KV_TILE // SUBLANES, SUBLANES, n) - bound)
            lsum = lsum + jnp.sum(p, axis=0)
            p_ref[u * KV_TILE:(u + 1) * KV_TILE, :] = p.reshape(KV_TILE, n).astype(jnp.bfloat16)
        l_ref[i] = lsum

    def weighted_values(step, p_ref):
        g, i = step >> log_nq, step & (nq - 1)
        vt = jnp.concatenate([vt_ref[0, 0, g * KV_UNROLL + u] for u in range(KV_UNROLL)], axis=1)
        acc_ref[i] += _bdot(vt, p_ref[...])

    p_refs = (pa_ref, pb_ref)

    def pipeline_step(step, parity):
        numerators(step + 1, p_refs[1 - parity])
        weighted_values(step, p_refs[parity])

    numerators(0, pa_ref)
    n_body = (total - 1) // PIPE_UNROLL

    def body(t, carry):
        for h in range(PIPE_UNROLL):
            pipeline_step(PIPE_UNROLL * t + h, h % 2)
        return carry

    lax.fori_loop(0, n_body, body, 0)
    for step in range(n_body * PIPE_UNROLL, total - 1):
        pipeline_step(step, step % 2)
    weighted_values(total - 1, p_refs[(total - 1) % 2])

    def recompute_if_unusable(i, carry):
        l_col = jnp.sum(l_ref[i], axis=0, keepdims=True)
        usable = (jnp.min(l_col) >= L_MIN) & (jnp.max(l_col) <= L_MAX)

        @pl.when(jnp.logical_not(usable))
        def _():
            _running_max_pass(q_tile(i), k_ref, vt_ref, m_ref, l_ref.at[i], acc_ref.at[i])

        return carry

    lax.fori_loop(0, nq, recompute_if_unusable, 0)

    def finish_tiles(t, carry):
        for h in range(FINISH_UNROLL):
            i = FINISH_UNROLL * t + h
            finish(i, acc_ref[i], jnp.sum(l_ref[i], axis=0, keepdims=True))
        return carry

    assert nq % FINISH_UNROLL == 0
    lax.fori_loop(0, nq // FINISH_UNROLL, finish_tiles, 0)


def _rows(i, tq):
    return pl.ds(pl.multiple_of(i * tq, tq), tq)


def _attn_a_kernel(lam_ref, gsub_ref, q_ref, k_ref, vt_ref, sg_ref, o_ref,
                   kmax_ref, bound_ref, pa_ref, pb_ref, m_ref, l_ref, acc_ref, *, lambda_init):
    tq = Q_TILE_A

    def q_tile(i):
        qt = q_ref[0, 0, i]
        row = lax.broadcasted_iota(jnp.int32, qt.shape, 0)
        zero = jnp.zeros_like(qt)
        return jnp.concatenate(
            [jnp.where(row < A_QK_DIM, qt, zero), jnp.where(row >= A_QK_DIM, qt, zero)], axis=1)

    lam_v = lam_ref[...]
    lam = (jnp.exp(jnp.sum(lam_v[0:1] * lam_v[1:2], axis=-1, keepdims=True))
           - jnp.exp(jnp.sum(lam_v[2:3] * lam_v[3:4], axis=-1, keepdims=True))
           + lambda_init)

    def finish(i, acc, l):
        ot = acc * (1.0 / l)
        o = (ot[:, 0:tq] - lam * ot[:, tq:2 * tq]).T
        o = _rms(o, gsub_ref[...]) * (1.0 - lambda_init)
        o_ref[0, _rows(i, tq), :] = (o * sg_ref[0, _rows(i, tq), :]).astype(o_ref.dtype)

    spans = ((0, A_QK_DIM), (A_QK_DIM, 2 * A_QK_DIM))
    _attention_chunk(q_tile, finish, spans, tq, k_ref, vt_ref,
                     kmax_ref, bound_ref, pa_ref, pb_ref, m_ref, l_ref, acc_ref)


def _attn_b_kernel(q_ref, k_ref, vt_ref, sg_ref, o_ref,
                   kmax_ref, bound_ref, pa_ref, pb_ref, m_ref, l_ref, acc_ref):
    tq = Q_TILE_B

    def q_tile(i):
        return q_ref[0, 0, i]

    def finish(i, acc, l):
        o = (acc * (1.0 / l)).T
        o_ref[0, _rows(i, tq), :] = (o * sg_ref[0, _rows(i, tq), :]).astype(o_ref.dtype)

    _attention_chunk(q_tile, finish, ((0, k_ref.shape[3]),), tq, k_ref, vt_ref,
                     kmax_ref, bound_ref, pa_ref, pb_ref, m_ref, l_ref, acc_ref)


def _attn_call(kern, name, extra_in, extra_specs, q, k, vt, sg, tq, n_maps, col0):
    b, nh, s, dq = k.shape
    dv = vt.shape[3]
    n = n_maps * tq
    qc = Q_CHUNK
    nq = qc // tq
    assert q.shape == (b, nh, s // tq, dq, tq)
    return pl.pallas_call(
        kern,
        out_shape=jax.ShapeDtypeStruct((b, s, nh * dv), jnp.bfloat16),
        grid=(b, nh, s // qc),
        in_specs=extra_specs + [
            pl.BlockSpec((1, 1, nq, dq, tq), lambda b_, h, c: (b_, h, c, 0, 0)),
            pl.BlockSpec((1, 1, s, dq), lambda b_, h, c: (b_, h, 0, 0)),
            pl.BlockSpec((1, 1, s // KV_TILE, dv, KV_TILE), lambda b_, h, c: (b_, h, 0, 0, 0)),
            pl.BlockSpec((1, qc, dv), lambda b_, h, c: (b_, c, col0 + h)),
        ],
        out_specs=pl.BlockSpec((1, qc, dv), lambda b_, h, c: (b_, c, h)),
        scratch_shapes=[
            pltpu.VMEM((n_maps, SUBLANES, LANES), jnp.float32),
            pltpu.VMEM((nq, SUBLANES, n), jnp.float32),
            pltpu.VMEM((KV_UNROLL * KV_TILE, n), jnp.bfloat16),
            pltpu.VMEM((KV_UNROLL * KV_TILE, n), jnp.bfloat16),
            pltpu.VMEM((SUBLANES, n), jnp.float32),
            pltpu.VMEM((nq, SUBLANES, n), jnp.float32),
            pltpu.VMEM((nq, dv, n), jnp.float32),
        ],
        compiler_params=pltpu.CompilerParams(
            dimension_semantics=("arbitrary", "arbitrary", "arbitrary"),
            vmem_limit_bytes=VMEM_LIMIT),
        name=name,
    )(*extra_in, q, k, vt, sg)


def _attn_a(lam_vecs, gsub, qa, ka, vta, sg, lambda_init):
    const = lambda shape: pl.BlockSpec(shape, lambda b_, h, c: (0, 0))
    return _attn_call(functools.partial(_attn_a_kernel, lambda_init=lambda_init), "attn_a",
                      (lam_vecs, gsub), [const((4, A_QK_DIM)), const((1, A_V_DIM))],
                      qa, ka, vta, sg, Q_TILE_A, 2, 0)


def _attn_b(qb, kb, vtb, sg):
    return _attn_call(_attn_b_kernel, "attn_b", (), [], qb, kb, vtb, sg, Q_TILE_B, 1, A_HEADS)


def _out_proj_kernel(ya_ref, yb_ref, x_ref, mod_ref, wout_ref, gpost_ref, o_ref):
    out = (_bdot(ya_ref[0], wout_ref[0:A_WIDTH, :])
           + _bdot(yb_ref[0], wout_ref[A_WIDTH:A_WIDTH + B_WIDTH, :]))
    gate = mod_ref[0][:, 2 * D_MODEL:3 * D_MODEL]
    o_ref[0] = x_ref[0] + gate * _rms(out, gpost_ref[...])


def _out_proj(ya, yb, x, mod, wout, gpost):
    b, s, _ = x.shape
    tr = ROW_TILE
    return pl.pallas_call(
        _out_proj_kernel,
        out_shape=jax.ShapeDtypeStruct(x.shape, x.dtype),
        grid=(b, s // tr),
        in_specs=[
            pl.BlockSpec((1, tr, A_WIDTH), lambda i, j: (i, j, 0)),
            pl.BlockSpec((1, tr, B_WIDTH), lambda i, j: (i, j, 0)),
            pl.BlockSpec((1, tr, D_MODEL), lambda i, j: (i, j, 0)),
            pl.BlockSpec((1, 1, 3 * D_MODEL), lambda i, j: (i, 0, 0)),
            pl.BlockSpec((A_WIDTH + B_WIDTH, D_MODEL), lambda i, j: (0, 0)),
            pl.BlockSpec((1, D_MODEL), lambda i, j: (0, 0)),
        ],
        out_specs=pl.BlockSpec((1, tr, D_MODEL), lambda i, j: (i, j, 0)),
        compiler_params=pltpu.CompilerParams(
            dimension_semantics=("arbitrary", "arbitrary"), vmem_limit_bytes=VMEM_LIMIT),
        name="out_proj",
    )(ya, yb, x, mod, wout, gpost)


def _rope_tables(seq, dim):
    half = dim // 2
    inv_freq = jnp.float32(ROPE_THETA) ** (-jnp.arange(0, dim, 2, dtype=jnp.float32) / dim)
    ang = jnp.arange(seq, dtype=jnp.float32)[:, None] * inv_freq[None, :]
    cos, sin = jnp.cos(ang), jnp.sin(ang)
    pad = 64 - dim
    ones = jnp.ones((seq, pad), jnp.float32)
    zeros = jnp.zeros((seq, pad), jnp.float32)
    zh = jnp.zeros((seq, half), jnp.float32)
    c = jnp.concatenate([cos, cos, ones], axis=1)
    s_lo = jnp.concatenate([-sin, zh, zeros], axis=1)
    s_hi = jnp.concatenate([zh, sin, zeros], axis=1)
    return tuple(jnp.concatenate([t, t], axis=1) for t in (c, s_lo, s_hi))


def _layout_w_in(w_in):
    aq, ak, av, ag, cq, ckv, kr, bg = jnp.split(
        w_in, [512, 1024, 1536, 2048, 2432, 2688, 2752], axis=1)
    kr = jnp.pad(kr, ((0, 0), (0, LANES - B_ROPE_DIM)))
    return jnp.concatenate([aq, ak, av, ag, cq, ckv, bg, kr], axis=1).astype(jnp.bfloat16)


def _layout_w_uq(w_uq):
    w = w_uq.reshape(Q_LORA, B_HEADS, B_NOPE_DIM + B_ROPE_DIM)
    nope = w[:, :, :B_NOPE_DIM].reshape(Q_LORA, B_HEADS * B_NOPE_DIM)
    rope = jnp.pad(w[:, :, B_NOPE_DIM:], ((0, 0), (0, 0), (0, LANES - B_ROPE_DIM)))
    return jnp.concatenate([nope, rope.reshape(Q_LORA, B_HEADS * LANES)], axis=1).astype(jnp.bfloat16)


def _encoder_layer(x, mod, layer, tabs, g_pre, w_in, lam_vecs, g_subln, g_cq, w_uq, g_ckv, w_ukv,
                   w_out, g_post):
    lambda_init = 0.8 - 0.6 * math.exp(-0.3 * layer)
    qa, ka, vta, qb, kb, vtb, sg = _in_proj(x, mod, g_pre, w_in, g_cq, w_uq, g_ckv, w_ukv, tabs)
    ya = _attn_a(lam_vecs, g_subln, qa, ka, vta, sg, lambda_init)
    yb = _attn_b(qb, kb, vtb, sg)
    return _out_proj(ya, yb, x, mod, w_out, g_post)


def kernel(x_prompt, x_sample, c_prompt, c_sample, w_ada, b_ada, g_pre, w_in, lambda_q1, lambda_k1,
           lambda_q2, lambda_k2, g_subln, g_cq, w_uq, g_ckv, w_ukv, w_out, g_post):
    depth = w_ada.shape[0]
    nb_p, nb_s = c_prompt.shape[0], c_sample.shape[0]
    pad_rows = -(nb_p + nb_s) % 8
    c_all = jnp.concatenate(
        [c_prompt, c_sample, jnp.zeros((pad_rows, D_MODEL), jnp.float32)], axis=0)
    tabs_p = _rope_tables(x_prompt.shape[1], A_ROPE_DIM) + _rope_tables(x_prompt.shape[1], B_ROPE_DIM)
    tabs_s = _rope_tables(x_sample.shape[1], A_ROPE_DIM) + _rope_tables(x_sample.shape[1], B_ROPE_DIM)
    y_p, y_s = x_prompt, x_sample
    for layer in range(depth):
        mod = _adaln_mod(c_all, w_ada[layer], b_ada[layer])
        mod_p = mod[:nb_p].reshape(nb_p, 1, 3 * D_MODEL)
        mod_s = mod[nb_p:nb_p + nb_s].reshape(nb_s, 1, 3 * D_MODEL)
        lam_vecs = jnp.stack([lambda_q1[layer], lambda_k1[layer], lambda_q2[layer], lambda_k2[layer]])
        params = (g_pre[layer].reshape(1, -1), _layout_w_in(w_in[layer]), lam_vecs,
                  g_subln[layer].reshape(1, -1), g_cq[layer].reshape(1, -1),
                  _layout_w_uq(w_uq[layer]), g_ckv[layer].reshape(1, -1),
                  w_ukv[layer].astype(jnp.bfloat16), w_out[layer].astype(jnp.bfloat16),
                  g_post[layer].reshape(1, -1))
        y_p = _encoder_layer(y_p, mod_p, layer, tabs_p, *params)
        y_s = _encoder_layer(y_s, mod_s, layer, tabs_s, *params)
    return (y_p, y_s)
```

```python
import functools
import math

import numpy as np
import jax
import jax.numpy as jnp
from jax import lax
from jax.experimental import pallas as pl
from jax.experimental.pallas import tpu as pltpu

D_MODEL = 1024
A_HEADS = 4
A_QK_DIM = 64
A_V_DIM = 128
A_WIDTH = A_HEADS * A_V_DIM
A_ROPE_DIM = A_QK_DIM // 4
B_HEADS = 4
B_NOPE_DIM = 128
B_ROPE_DIM = 64
B_V_DIM = 128
B_WIDTH = B_HEADS * B_V_DIM
B_QK_PAD = 256
Q_LORA = 384
KV_LORA = 256
ROPE_THETA = 500000.0
NORM_EPS = 1e-6
LOG2E = 1.4426950408889634

LANES = 128
SUBLANES = 8
VMEM_LIMIT = 56 * 1024 * 1024

ROW_TILE = 512
KV_TILE = ROW_TILE
KV_UNROLL = 2
Q_TILE_A = 256
Q_TILE_B = 512
Q_CHUNK = 4096
PIPE_UNROLL = 16
FINISH_UNROLL = 2
NORM_CHUNK = 1024

L_MIN = 2.0 ** -80
L_MAX = 2.0 ** 100
BOUND_MARGIN = 1.0 + 2.0 ** -5

SEG_AQ = 0
SEG_AK = 512
SEG_AV = 1024
SEG_AG = 1536
SEG_CQ = 2048
SEG_CKV = 2432
SEG_BG = 2688
SEG_KR = 3200
IN_COLS_PAD = 3328


def _silu(x):
    return x * (1.0 / (1.0 + jnp.exp(-x)))


def _rms(x, g):
    return x * lax.rsqrt(jnp.mean(x * x, axis=-1, keepdims=True) + NORM_EPS) * g


def _bdot(a, b):
    return jnp.dot(a, b, preferred_element_type=jnp.float32)


def _mod_kernel(c_ref, w_ref, b_ref, o_ref):
    c = _silu(c_ref[...]).astype(jnp.bfloat16)
    o_ref[...] = _bdot(c, w_ref[...].astype(jnp.bfloat16)) + b_ref[...]


def _adaln_mod(c_all, w_ada, b_ada):
    rows = c_all.shape[0]
    n_blk = 1024
    return pl.pallas_call(
        _mod_kernel,
        out_shape=jax.ShapeDtypeStruct((rows, 3 * D_MODEL), jnp.float32),
        grid=(3 * D_MODEL // n_blk,),
        in_specs=[
            pl.BlockSpec((rows, D_MODEL), lambda j: (0, 0)),
            pl.BlockSpec((D_MODEL, n_blk), lambda j: (0, j)),
            pl.BlockSpec((1, n_blk), lambda j: (0, j)),
        ],
        out_specs=pl.BlockSpec((rows, n_blk), lambda j: (0, j)),
        compiler_params=pltpu.CompilerParams(
            dimension_semantics=("arbitrary",), vmem_limit_bytes=VMEM_LIMIT),
        name="adaln_mod",
    )(c_all, w_ada, b_ada.reshape(1, -1))


def _rope(x, c, s_lo, s_hi, half):
    return (x * c + pltpu.roll(x, LANES - half, 1) * s_lo
            + pltpu.roll(x, half, 1) * s_hi)


def _in_proj_kernel(x_ref, mod_ref, gpre_ref, win_ref, gcq_ref, wuq_ref, gckv_ref, wukv_ref,
                    ca_ref, sa_lo_ref, sa_hi_ref, cb_ref, sb_lo_ref, sb_hi_ref,
                    qa_ref, ka_ref, va_ref, qb_ref, kb_ref, vb_ref, sg_ref,
                    *, scale_a, scale_b):
    x = x_ref[0]
    mod = mod_ref[0]
    shift = mod[:, 0:D_MODEL]
    scale = mod[:, D_MODEL:2 * D_MODEL]
    h = (_rms(x, gpre_ref[...]) * (1.0 + scale) + shift).astype(jnp.bfloat16)

    def proj(lo, width):
        return _bdot(h, win_ref[:, lo:lo + width])

    ca, sa_lo, sa_hi = ca_ref[...], sa_lo_ref[...], sa_hi_ref[...]
    cb, sb_lo, sb_hi = cb_ref[...], sb_lo_ref[...], sb_hi_ref[...]
    half_a = A_ROPE_DIM // 2
    half_b = B_ROPE_DIM // 2

    aq = proj(SEG_AQ, 512)
    ak = proj(SEG_AK, 512)
    av = proj(SEG_AV, 512)
    for hd in range(A_HEADS):
        sl = slice(hd * LANES, (hd + 1) * LANES)
        qt = (_rope(aq[:, sl], ca, sa_lo, sa_hi, half_a) * scale_a).T.astype(jnp.bfloat16)
        for blk in range(ROW_TILE // Q_TILE_A):
            qa_ref[0, hd, blk] = qt[:, blk * Q_TILE_A:(blk + 1) * Q_TILE_A]
        ka_ref[0, hd] = _rope(ak[:, sl], ca, sa_lo, sa_hi, half_a).astype(jnp.bfloat16)
        va_ref[0, hd, 0] = av[:, sl].T.astype(jnp.bfloat16)

    sg_ref[0, :, 0:A_WIDTH] = _silu(proj(SEG_AG, A_WIDTH))
    sg_ref[0, :, A_WIDTH:A_WIDTH + B_WIDTH] = _silu(proj(SEG_BG, B_WIDTH))

    cq = _rms(proj(SEG_CQ, Q_LORA), gcq_ref[...]).astype(jnp.bfloat16)
    q = _bdot(cq, wuq_ref[...])
    ckv = _rms(proj(SEG_CKV, KV_LORA), gckv_ref[...]).astype(jnp.bfloat16)
    kv = _bdot(ckv, wukv_ref[...])
    kr = _rope(proj(SEG_KR, LANES), cb, sb_lo, sb_hi, half_b).astype(jnp.bfloat16)
    for hd in range(B_HEADS):
        qn = q[:, hd * LANES:(hd + 1) * LANES]
        qr = q[:, (B_HEADS + hd) * LANES:(B_HEADS + hd + 1) * LANES]
        qb_ref[0, hd, 0, 0:LANES, :] = (qn * scale_b).T.astype(jnp.bfloat16)
        qb_ref[0, hd, 0, LANES:2 * LANES, :] = (
            _rope(qr, cb, sb_lo, sb_hi, half_b) * scale_b).T.astype(jnp.bfloat16)
        kb_ref[0, hd, :, 0:LANES] = kv[:, 2 * hd * LANES:(2 * hd + 1) * LANES].astype(jnp.bfloat16)
        kb_ref[0, hd, :, LANES:2 * LANES] = kr
        vb_ref[0, hd, 0] = kv[:, (2 * hd + 1) * LANES:(2 * hd + 2) * LANES].T.astype(jnp.bfloat16)


def _in_proj(x, mod, gpre, win, gcq, wuq, gckv, wukv, tabs):
    b, s, _ = x.shape
    tr = ROW_TILE
    const = lambda shape: pl.BlockSpec(shape, lambda i, j: (0,) * len(shape))
    tab = pl.BlockSpec((tr, LANES), lambda i, j: (j, 0))
    head = lambda d: pl.BlockSpec((1, A_HEADS, tr, d), lambda i, j: (i, 0, j, 0))
    head_t = lambda d: pl.BlockSpec((1, A_HEADS, 1, d, tr), lambda i, j: (i, 0, j, 0, 0))
    qa_blocks = tr // Q_TILE_A
    assert tr == Q_TILE_B and tr == qa_blocks * Q_TILE_A
    bf = jnp.bfloat16
    kern = functools.partial(_in_proj_kernel,
                             scale_a=A_QK_DIM ** -0.5 * LOG2E,
                             scale_b=(B_NOPE_DIM + B_ROPE_DIM) ** -0.5 * LOG2E)
    return pl.pallas_call(
        kern,
        out_shape=(
            jax.ShapeDtypeStruct((b, A_HEADS, s // Q_TILE_A, LANES, Q_TILE_A), bf),
            jax.ShapeDtypeStruct((b, A_HEADS, s, LANES), bf),
            jax.ShapeDtypeStruct((b, A_HEADS, s // tr, A_V_DIM, tr), bf),
            jax.ShapeDtypeStruct((b, B_HEADS, s // tr, B_QK_PAD, tr), bf),
            jax.ShapeDtypeStruct((b, B_HEADS, s, B_QK_PAD), bf),
            jax.ShapeDtypeStruct((b, B_HEADS, s // tr, B_V_DIM, tr), bf),
            jax.ShapeDtypeStruct((b, s, A_WIDTH + B_WIDTH), jnp.float32),
        ),
        grid=(b, s // tr),
        in_specs=[
            pl.BlockSpec((1, tr, D_MODEL), lambda i, j: (i, j, 0)),
            pl.BlockSpec((1, 1, 3 * D_MODEL), lambda i, j: (i, 0, 0)),
            const((1, D_MODEL)),
            const((D_MODEL, IN_COLS_PAD)),
            const((1, Q_LORA)),
            const((Q_LORA, 2 * B_HEADS * LANES)),
            const((1, KV_LORA)),
            const((KV_LORA, B_HEADS * (B_NOPE_DIM + B_V_DIM))),
            tab, tab, tab, tab, tab, tab,
        ],
        out_specs=(pl.BlockSpec((1, A_HEADS, qa_blocks, LANES, Q_TILE_A),
                                lambda i, j: (i, 0, j, 0, 0)),
                   head(LANES), head_t(A_V_DIM), head_t(B_QK_PAD), head(B_QK_PAD),
                   head_t(B_V_DIM),
                   pl.BlockSpec((1, tr, A_WIDTH + B_WIDTH), lambda i, j: (i, j, 0))),
        compiler_params=pltpu.CompilerParams(
            dimension_semantics=("arbitrary", "arbitrary"), vmem_limit_bytes=VMEM_LIMIT),
        name="in_proj",
    )(x, mod, gpre, win, gcq, wuq, gckv, wukv, *tabs)


def _key_tile(k_ref, t):
    return k_ref[0, 0, pl.ds(pl.multiple_of(t * KV_TILE, KV_TILE), KV_TILE), :]


def _key_norm_max(k_ref, lane_spans):
    s = k_ref.shape[2]
    d = k_ref.shape[3]

    def body(c, carry):
        kk = k_ref[0, 0, pl.ds(pl.multiple_of(c * NORM_CHUNK, NORM_CHUNK), NORM_CHUNK), :]
        ksq = kk.astype(jnp.float32)
        ksq = ksq * ksq
        lane = lax.broadcasted_iota(jnp.int32, ksq.shape, 1)
        out = []
        for (lo, hi), cur in zip(lane_spans, carry):
            part = ksq if (lo, hi) == (0, d) else jnp.where((lane >= lo) & (lane < hi), ksq, 0.0)
            n2 = jnp.sum(part, axis=1, keepdims=True)
            out.append(jnp.maximum(cur, jnp.max(n2, axis=0, keepdims=True)))
        return tuple(out)

    init = tuple(jnp.zeros((1, 1), jnp.float32) for _ in lane_spans)
    res = lax.fori_loop(0, s // NORM_CHUNK, body, init)
    return [jnp.broadcast_to(r, (SUBLANES, LANES)) for r in res]


def _col_bound(qt, kmax_tiles, cols_per_map):
    qsq = qt.astype(jnp.float32)
    qsq = (qsq * qsq).astype(jnp.bfloat16)
    ones = jnp.ones((2 * SUBLANES, qt.shape[0]), jnp.bfloat16)
    q2 = _bdot(ones, qsq)[0:SUBLANES]
    kmax = jnp.concatenate(
        [t for t in kmax_tiles for _ in range(cols_per_map // LANES)], axis=1)
    return jnp.sqrt(q2 * kmax) * BOUND_MARGIN


def _running_max_pass(qt, k_ref, vt_ref, m_ref, l_ref, acc_ref):
    n = qt.shape[1]
    n_kv = k_ref.shape[2] // KV_TILE
    m_ref[...] = jnp.full(m_ref.shape, -jnp.inf, jnp.float32)
    l_ref[...] = jnp.zeros(l_ref.shape, jnp.float32)
    acc_ref[...] = jnp.zeros(acc_ref.shape, jnp.float32)

    def body(t, carry):
        s = _bdot(_key_tile(k_ref, t), qt).reshape(KV_TILE // SUBLANES, SUBLANES, n)
        m_old = m_ref[...]
        m_col = jnp.max(jnp.max(s, axis=0), axis=0, keepdims=True)
        m_new = jnp.maximum(m_old, m_col)
        alpha = jnp.exp2(m_old - m_new)
        p = jnp.exp2(s - m_new)
        l_ref[...] = alpha * l_ref[...] + jnp.sum(p, axis=0)
        pv = _bdot(vt_ref[0, 0, t], p.reshape(KV_TILE, n).astype(jnp.bfloat16))
        acc_ref[...] = alpha[0:1] * acc_ref[...] + pv
        m_ref[...] = m_new
        return carry

    lax.fori_loop(0, n_kv, body, 0)


def _attention_chunk(q_tile, finish, lane_spans, tq, k_ref, vt_ref,
                     kmax_ref, bound_ref, pa_ref, pb_ref, m_ref, l_ref, acc_ref):
    nq, _, n = bound_ref.shape
    n_grp = k_ref.shape[2] // (KV_TILE * KV_UNROLL)
    total = n_grp * nq
    log_nq = nq.bit_length() - 1
    assert nq == 1 << log_nq and total % 2 == 0 and total >= 2

    @pl.when(pl.program_id(2) == 0)
    def _():
        for idx, tile in enumerate(_key_norm_max(k_ref, lane_spans)):
            kmax_ref[idx] = tile

    kmax_tiles = [kmax_ref[idx] for idx in range(len(lane_spans))]
    for i in range(nq):
        bound_ref[i] = _col_bound(q_tile(i), kmax_tiles, tq)
    l_ref[...] = jnp.zeros(l_ref.shape, jnp.float32)
    acc_ref[...] = jnp.zeros(acc_ref.shape, jnp.float32)

    def numerators(step, p_ref):
        g, i = step >> log_nq, step & (nq - 1)
        qt = q_tile(i)
        bound = bound_ref[i]
        lsum = l_ref[i]
        rows = KV_UNROLL * KV_TILE
        keys = k_ref[0, 0, pl.ds(pl.multiple_of(g * rows, rows), rows), :]
        s = _bdot(keys, qt)
        p = jnp.exp2(s.reshape(rows // SUBLANES, SUBLANES, n) - bound)
        p_ref[...] = p.reshape(rows, n).astype(jnp.bfloat16)
        l_ref[i] = lsum + jnp.sum(p, axis=0)

    def weighted_values(step, p_ref):
        g, i = step >> log_nq, step & (nq - 1)
        vt = jnp.concatenate([vt_ref[0, 0, g * KV_UNROLL + u] for u in range(KV_UNROLL)], axis=1)
        acc_ref[i] += _bdot(vt, p_ref[...])

    p_refs = (pa_ref, pb_ref)

    def pipeline_step(step, parity):
        numerators(step + 1, p_refs[1 - parity])
        weighted_values(step, p_refs[parity])

    numerators(0, pa_ref)
    n_body = (total - 1) // PIPE_UNROLL

    def body(t, carry):
        for h in range(PIPE_UNROLL):
            pipeline_step(PIPE_UNROLL * t + h, h % 2)
        return carry

    lax.fori_loop(0, n_body, body, 0)
    for step in range(n_body * PIPE_UNROLL, total - 1):
        pipeline_step(step, step % 2)
    weighted_values(total - 1, p_refs[(total - 1) % 2])

    def recompute_if_unusable(i, carry):
        l_col = jnp.sum(l_ref[i], axis=0, keepdims=True)
        usable = (jnp.min(l_col) >= L_MIN) & (jnp.max(l_col) <= L_MAX)

        @pl.when(jnp.logical_not(usable))
        def _():
            _running_max_pass(q_tile(i), k_ref, vt_ref, m_ref, l_ref.at[i], acc_ref.at[i])

        return carry

    lax.fori_loop(0, nq, recompute_if_unusable, 0)

    def finish_tiles(t, carry):
        for h in range(FINISH_UNROLL):
            i = FINISH_UNROLL * t + h
            finish(i, acc_ref[i], jnp.sum(l_ref[i], axis=0, keepdims=True))
        return carry

    assert nq % FINISH_UNROLL == 0
    lax.fori_loop(0, nq // FINISH_UNROLL, finish_tiles, 0)


def _rows(i, tq):
    return pl.ds(pl.multiple_of(i * tq, tq), tq)


def _attn_a_kernel(lam_ref, gsub_ref, q_ref, k_ref, vt_ref, sg_ref, o_ref,
                   kmax_ref, bound_ref, pa_ref, pb_ref, m_ref, l_ref, acc_ref, *, lambda_init):
    tq = Q_TILE_A

    def q_tile(i):
        qt = q_ref[0, 0, i]
        row = lax.broadcasted_iota(jnp.int32, qt.shape, 0)
        zero = jnp.zeros_like(qt)
        return jnp.concatenate(
            [jnp.where(row < A_QK_DIM, qt, zero), jnp.where(row >= A_QK_DIM, qt, zero)], axis=1)

    lam_v = lam_ref[...]
    lam = (jnp.exp(jnp.sum(lam_v[0:1] * lam_v[1:2], axis=-1, keepdims=True))
           - jnp.exp(jnp.sum(lam_v[2:3] * lam_v[3:4], axis=-1, keepdims=True))
           + lambda_init)

    def finish(i, acc, l):
        ot = acc * (1.0 / l)
        o = (ot[:, 0:tq] - lam * ot[:, tq:2 * tq]).T
        o = _rms(o, gsub_ref[...]) * (1.0 - lambda_init)
        o_ref[0, _rows(i, tq), :] = (o * sg_ref[0, _rows(i, tq), :]).astype(o_ref.dtype)

    spans = ((0, A_QK_DIM), (A_QK_DIM, 2 * A_QK_DIM))
    _attention_chunk(q_tile, finish, spans, tq, k_ref, vt_ref,
                     kmax_ref, bound_ref, pa_ref, pb_ref, m_ref, l_ref, acc_ref)


def _attn_b_kernel(q_ref, k_ref, vt_ref, sg_ref, o_ref,
                   kmax_ref, bound_ref, pa_ref, pb_ref, m_ref, l_ref, acc_ref):
    tq = Q_TILE_B

    def q_tile(i):
        return q_ref[0, 0, i]

    def finish(i, acc, l):
        o = (acc * (1.0 / l)).T
        o_ref[0, _rows(i, tq), :] = (o * sg_ref[0, _rows(i, tq), :]).astype(o_ref.dtype)

    _attention_chunk(q_tile, finish, ((0, k_ref.shape[3]),), tq, k_ref, vt_ref,
                     kmax_ref, bound_ref, pa_ref, pb_ref, m_ref, l_ref, acc_ref)


def _attn_call(kern, name, extra_in, extra_specs, q, k, vt, sg, tq, n_maps, col0):
    b, nh, s, dq = k.shape
    dv = vt.shape[3]
    n = n_maps * tq
    qc = Q_CHUNK
    nq = qc // tq
    assert q.shape == (b, nh, s // tq, dq, tq)
    return pl.pallas_call(
        kern,
        out_shape=jax.ShapeDtypeStruct((b, s, nh * dv), jnp.bfloat16),
        grid=(b, nh, s // qc),
        in_specs=extra_specs + [
            pl.BlockSpec((1, 1, nq, dq, tq), lambda b_, h, c: (b_, h, c, 0, 0)),
            pl.BlockSpec((1, 1, s, dq), lambda b_, h, c: (b_, h, 0, 0)),
            pl.BlockSpec((1, 1, s // KV_TILE, dv, KV_TILE), lambda b_, h, c: (b_, h, 0, 0, 0)),
            pl.BlockSpec((1, qc, dv), lambda b_, h, c: (b_, c, col0 + h)),
        ],
        out_specs=pl.BlockSpec((1, qc, dv), lambda b_, h, c: (b_, c, h)),
        scratch_shapes=[
            pltpu.VMEM((n_maps, SUBLANES, LANES), jnp.float32),
            pltpu.VMEM((nq, SUBLANES, n), jnp.float32),
            pltpu.VMEM((KV_UNROLL * KV_TILE, n), jnp.bfloat16),
            pltpu.VMEM((KV_UNROLL * KV_TILE, n), jnp.bfloat16),
            pltpu.VMEM((SUBLANES, n), jnp.float32),
            pltpu.VMEM((nq, SUBLANES, n), jnp.float32),
            pltpu.VMEM((nq, dv, n), jnp.float32),
        ],
        compiler_params=pltpu.CompilerParams(
            dimension_semantics=("arbitrary", "arbitrary", "arbitrary"),
            vmem_limit_bytes=VMEM_LIMIT),
        name=name,
    )(*extra_in, q, k, vt, sg)


def _attn_a(lam_vecs, gsub, qa, ka, vta, sg, lambda_init):
    const = lambda shape: pl.BlockSpec(shape, lambda b_, h, c: (0, 0))
    return _attn_call(functools.partial(_attn_a_kernel, lambda_init=lambda_init), "attn_a",
                      (lam_vecs, gsub), [const((4, A_QK_DIM)), const((1, A_V_DIM))],
                      qa, ka, vta, sg, Q_TILE_A, 2, 0)


def _attn_b(qb, kb, vtb, sg):
    return _attn_call(_attn_b_kernel, "attn_b", (), [], qb, kb, vtb, sg, Q_TILE_B, 1, A_HEADS)


def _out_proj_kernel(ya_ref, yb_ref, x_ref, mod_ref, wout_ref, gpost_ref, o_ref):
    out = (_bdot(ya_ref[0], wout_ref[0:A_WIDTH, :])
           + _bdot(yb_ref[0], wout_ref[A_WIDTH:A_WIDTH + B_WIDTH, :]))
    gate = mod_ref[0][:, 2 * D_MODEL:3 * D_MODEL]
    o_ref[0] = x_ref[0] + gate * _rms(out, gpost_ref[...])


def _out_proj(ya, yb, x, mod, wout, gpost):
    b, s, _ = x.shape
    tr = ROW_TILE
    return pl.pallas_call(
        _out_proj_kernel,
        out_shape=jax.ShapeDtypeStruct(x.shape, x.dtype),
        grid=(b, s // tr),
        in_specs=[
            pl.BlockSpec((1, tr, A_WIDTH), lambda i, j: (i, j, 0)),
            pl.BlockSpec((1, tr, B_WIDTH), lambda i, j: (i, j, 0)),
            pl.BlockSpec((1, tr, D_MODEL), lambda i, j: (i, j, 0)),
            pl.BlockSpec((1, 1, 3 * D_MODEL), lambda i, j: (i, 0, 0)),
            pl.BlockSpec((A_WIDTH + B_WIDTH, D_MODEL), lambda i, j: (0, 0)),
            pl.BlockSpec((1, D_MODEL), lambda i, j: (0, 0)),
        ],
        out_specs=pl.BlockSpec((1, tr, D_MODEL), lambda i, j: (i, j, 0)),
        compiler_params=pltpu.CompilerParams(
            dimension_semantics=("arbitrary", "arbitrary"), vmem_limit_bytes=VMEM_LIMIT),
        name="out_proj",
    )(ya, yb, x, mod, wout, gpost)


def _rope_tables(seq, dim):
    half = dim // 2
    inv_freq = jnp.float32(ROPE_THETA) ** (-jnp.arange(0, dim, 2, dtype=jnp.float32) / dim)
    d = np.arange(LANES) % 64
    in_span, lo = d < dim, d < half
    freq = jnp.where(in_span, inv_freq[d % half], 0.0)
    ang = jnp.arange(seq, dtype=jnp.float32)[:, None] * freq[None, :]
    cos, sin = jnp.cos(ang), jnp.sin(ang)
    return cos, jnp.where(lo, -sin, 0.0), jnp.where(in_span & ~lo, sin, 0.0)


def _layout_w_in(w_in):
    aq, ak, av, ag, cq, ckv, kr, bg = jnp.split(
        w_in, [512, 1024, 1536, 2048, 2432, 2688, 2752], axis=1)
    kr = jnp.pad(kr, ((0, 0), (0, LANES - B_ROPE_DIM)))
    return jnp.concatenate([aq, ak, av, ag, cq, ckv, bg, kr], axis=1).astype(jnp.bfloat16)


def _layout_w_uq(w_uq):
    w = w_uq.reshape(Q_LORA, B_HEADS, B_NOPE_DIM + B_ROPE_DIM)
    nope = w[:, :, :B_NOPE_DIM].reshape(Q_LORA, B_HEADS * B_NOPE_DIM)
    rope = jnp.pad(w[:, :, B_NOPE_DIM:], ((0, 0), (0, 0), (0, LANES - B_ROPE_DIM)))
    return jnp.concatenate([nope, rope.reshape(Q_LORA, B_HEADS * LANES)], axis=1).astype(jnp.bfloat16)


def _encoder_layer(x, mod, layer, tabs, g_pre, w_in, lam_vecs, g_subln, g_cq, w_uq, g_ckv, w_ukv,
                   w_out, g_post):
    lambda_init = 0.8 - 0.6 * math.exp(-0.3 * layer)
    qa, ka, vta, qb, kb, vtb, sg = _in_proj(x, mod, g_pre, w_in, g_cq, w_uq, g_ckv, w_ukv, tabs)
    ya = _attn_a(lam_vecs, g_subln, qa, ka, vta, sg, lambda_init)
    yb = _attn_b(qb, kb, vtb, sg)
    return _out_proj(ya, yb, x, mod, w_out, g_post)


def kernel(x_prompt, x_sample, c_prompt, c_sample, w_ada, b_ada, g_pre, w_in, lambda_q1, lambda_k1,
           lambda_q2, lambda_k2, g_subln, g_cq, w_uq, g_ckv, w_ukv, w_out, g_post):
    depth = w_ada.shape[0]
    nb_p, nb_s = c_prompt.shape[0], c_sample.shape[0]
    pad_rows = -(nb_p + nb_s) % 8
    c_all = jnp.concatenate(
        [c_prompt, c_sample, jnp.zeros((pad_rows, D_MODEL), jnp.float32)], axis=0)
    max_seq = max(x_prompt.shape[1], x_sample.shape[1])
    tabs = _rope_tables(max_seq, A_ROPE_DIM) + _rope_tables(max_seq, B_ROPE_DIM)
    y_p, y_s = x_prompt, x_sample
    for layer in range(depth):
        mod = _adaln_mod(c_all, w_ada[layer], b_ada[layer])
        mod_p = mod[:nb_p].reshape(nb_p, 1, 3 * D_MODEL)
        mod_s = mod[nb_p:nb_p + nb_s].reshape(nb_s, 1, 3 * D_MODEL)
        lam_vecs = jnp.stack([lambda_q1[layer], lambda_k1[layer], lambda_q2[layer], lambda_k2[layer]])
        params = (g_pre[layer].reshape(1, -1), _layout_w_in(w_in[layer]), lam_vecs,
                  g_subln[layer].reshape(1, -1), g_cq[layer].reshape(1, -1),
                  _layout_w_uq(w_uq[layer]), g_ckv[layer].reshape(1, -1),
                  w_ukv[layer].astype(jnp.bfloat16), w_out[layer].astype(jnp.bfloat16),
                  g_post[layer].reshape(1, -1))
        y_p = _encoder_layer(y_p, mod_p, layer, tabs, *params)
        y_s = _encoder_layer(y_s, mod_s, layer, tabs, *params)
    return (y_p, y_s)
```

```python
import functools
import math

import numpy as np
import jax
import jax.numpy as jnp
from jax import lax
from jax.experimental import pallas as pl
from jax.experimental.pallas import tpu as pltpu

D_MODEL = 1024
A_HEADS = 4
A_QK_DIM = 64
A_V_DIM = 128
A_WIDTH = A_HEADS * A_V_DIM
A_ROPE_DIM = A_QK_DIM // 4
B_HEADS = 4
B_NOPE_DIM = 128
B_ROPE_DIM = 64
B_V_DIM = 128
B_WIDTH = B_HEADS * B_V_DIM
B_QK_PAD = 256
Q_LORA = 384
KV_LORA = 256
ROPE_THETA = 500000.0
NORM_EPS = 1e-6
LOG2E = 1.4426950408889634

LANES = 128
SUBLANES = 8
VMEM_LIMIT = 56 * 1024 * 1024

ROW_TILE = 512
KV_TILE = ROW_TILE
KV_UNROLL = 2
Q_TILE_A = 256
Q_TILE_B = 512
Q_CHUNK = 4096
PIPE_UNROLL = 16
FINISH_UNROLL = 4
NORM_CHUNK = 1024

L_MIN = 2.0 ** -80
L_MAX = 2.0 ** 100
BOUND_MARGIN = 1.0 + 2.0 ** -5

SEG_AQ = 0
SEG_AK = 512
SEG_AV = 1024
SEG_AG = 1536
SEG_CQ = 2048
SEG_CKV = 2432
SEG_BG = 2688
SEG_KR = 3200
IN_COLS_PAD = 3328


def _silu(x):
    return x * (1.0 / (1.0 + jnp.exp(-x)))


def _rms(x, g):
    return x * lax.rsqrt(jnp.mean(x * x, axis=-1, keepdims=True) + NORM_EPS) * g


def _bdot(a, b):
    return jnp.dot(a, b, preferred_element_type=jnp.float32)


def _mod_kernel(c_ref, w_ref, b_ref, o_ref):
    c = _silu(c_ref[...]).astype(jnp.bfloat16)
    o_ref[...] = _bdot(c, w_ref[...].astype(jnp.bfloat16)) + b_ref[...]


def _adaln_mod(c_all, w_ada, b_ada):
    rows = c_all.shape[0]
    n_blk = 1024
    return pl.pallas_call(
        _mod_kernel,
        out_shape=jax.ShapeDtypeStruct((rows, 3 * D_MODEL), jnp.float32),
        grid=(3 * D_MODEL // n_blk,),
        in_specs=[
            pl.BlockSpec((rows, D_MODEL), lambda j: (0, 0)),
            pl.BlockSpec((D_MODEL, n_blk), lambda j: (0, j)),
            pl.BlockSpec((1, n_blk), lambda j: (0, j)),
        ],
        out_specs=pl.BlockSpec((rows, n_blk), lambda j: (0, j)),
        compiler_params=pltpu.CompilerParams(
            dimension_semantics=("arbitrary",), vmem_limit_bytes=VMEM_LIMIT),
        name="adaln_mod",
    )(c_all, w_ada, b_ada.reshape(1, -1))


def _rope(x, c, s_lo, s_hi, half):
    return (x * c + pltpu.roll(x, LANES - half, 1) * s_lo
            + pltpu.roll(x, half, 1) * s_hi)


def _in_proj_kernel(x_ref, mod_ref, gpre_ref, win_ref, gcq_ref, wuq_ref, gckv_ref, wukv_ref,
                    ca_ref, sa_lo_ref, sa_hi_ref, cb_ref, sb_lo_ref, sb_hi_ref,
                    qa_ref, ka_ref, va_ref, qb_ref, kb_ref, vb_ref, sg_ref,
                    *, scale_a, scale_b):
    mod = mod_ref[0]
    shift = mod[:, 0:D_MODEL]
    scale = mod[:, D_MODEL:2 * D_MODEL]
    half_a = A_ROPE_DIM // 2
    half_b = B_ROPE_DIM // 2
    bf = jnp.bfloat16

    h = (_rms(x_ref[0], gpre_ref[...]) * (1.0 + scale) + shift).astype(bf)

    def proj(lo, width):
        return _bdot(h, win_ref[:, lo:lo + width])

    ca, sa_lo, sa_hi = ca_ref[...], sa_lo_ref[...], sa_hi_ref[...]
    cb, sb_lo, sb_hi = cb_ref[...], sb_lo_ref[...], sb_hi_ref[...]

    cq = _rms(proj(SEG_CQ, Q_LORA), gcq_ref[...]).astype(bf)
    ckv = _rms(proj(SEG_CKV, KV_LORA), gckv_ref[...]).astype(bf)
    kr = _rope(proj(SEG_KR, LANES), cb, sb_lo, sb_hi, half_b).astype(bf)

    aq = proj(SEG_AQ, 512)
    ak = proj(SEG_AK, 512)
    av = proj(SEG_AV, 512)
    for hd in range(A_HEADS):
        sl = slice(hd * LANES, (hd + 1) * LANES)
        qt = (_rope(aq[:, sl], ca, sa_lo, sa_hi, half_a) * scale_a).T.astype(bf)
        for blk in range(ROW_TILE // Q_TILE_A):
            qa_ref[0, hd, blk] = qt[:, blk * Q_TILE_A:(blk + 1) * Q_TILE_A]
        ka_ref[0, hd] = _rope(ak[:, sl], ca, sa_lo, sa_hi, half_a).astype(bf)
        va_ref[0, hd, 0] = av[:, sl].T.astype(bf)

    q = _bdot(cq, wuq_ref[...])
    kv = _bdot(ckv, wukv_ref[...])
    for hd in range(B_HEADS):
        qn = q[:, hd * LANES:(hd + 1) * LANES]
        qr = q[:, (B_HEADS + hd) * LANES:(B_HEADS + hd + 1) * LANES]
        qb_ref[0, hd, 0, 0:LANES, :] = (qn * scale_b).T.astype(bf)
        qb_ref[0, hd, 0, LANES:2 * LANES, :] = (
            _rope(qr, cb, sb_lo, sb_hi, half_b) * scale_b).T.astype(bf)
        kb_ref[0, hd, :, 0:LANES] = kv[:, 2 * hd * LANES:(2 * hd + 1) * LANES].astype(bf)
        kb_ref[0, hd, :, LANES:2 * LANES] = kr
        vb_ref[0, hd, 0] = kv[:, (2 * hd + 1) * LANES:(2 * hd + 2) * LANES].T.astype(bf)

    sg_ref[0, :, 0:A_WIDTH] = _silu(proj(SEG_AG, A_WIDTH))
    sg_ref[0, :, A_WIDTH:A_WIDTH + B_WIDTH] = _silu(proj(SEG_BG, B_WIDTH))


def _in_proj(x, mod, gpre, win, gcq, wuq, gckv, wukv, tabs):
    b, s, _ = x.shape
    tr = ROW_TILE
    const = lambda shape: pl.BlockSpec(shape, lambda i, j: (0,) * len(shape))
    tab = pl.BlockSpec((tr, LANES), lambda i, j: (j, 0))
    head = lambda d: pl.BlockSpec((1, A_HEADS, tr, d), lambda i, j: (i, 0, j, 0))
    head_t = lambda d: pl.BlockSpec((1, A_HEADS, 1, d, tr), lambda i, j: (i, 0, j, 0, 0))
    qa_blocks = tr // Q_TILE_A
    assert tr == Q_TILE_B and tr == qa_blocks * Q_TILE_A
    bf = jnp.bfloat16
    kern = functools.partial(_in_proj_kernel,
                             scale_a=A_QK_DIM ** -0.5 * LOG2E,
                             scale_b=(B_NOPE_DIM + B_ROPE_DIM) ** -0.5 * LOG2E)
    return pl.pallas_call(
        kern,
        out_shape=(
            jax.ShapeDtypeStruct((b, A_HEADS, s // Q_TILE_A, LANES, Q_TILE_A), bf),
            jax.ShapeDtypeStruct((b, A_HEADS, s, LANES), bf),
            jax.ShapeDtypeStruct((b, A_HEADS, s // tr, A_V_DIM, tr), bf),
            jax.ShapeDtypeStruct((b, B_HEADS, s // tr, B_QK_PAD, tr), bf),
            jax.ShapeDtypeStruct((b, B_HEADS, s, B_QK_PAD), bf),
            jax.ShapeDtypeStruct((b, B_HEADS, s // tr, B_V_DIM, tr), bf),
            jax.ShapeDtypeStruct((b, s, A_WIDTH + B_WIDTH), jnp.float32),
        ),
        grid=(b, s // tr),
        in_specs=[
            pl.BlockSpec((1, tr, D_MODEL), lambda i, j: (i, j, 0)),
            pl.BlockSpec((1, 1, 3 * D_MODEL), lambda i, j: (i, 0, 0)),
            const((1, D_MODEL)),
            const((D_MODEL, IN_COLS_PAD)),
            const((1, Q_LORA)),
            const((Q_LORA, 2 * B_HEADS * LANES)),
            const((1, KV_LORA)),
            const((KV_LORA, B_HEADS * (B_NOPE_DIM + B_V_DIM))),
            tab, tab, tab, tab, tab, tab,
        ],
        out_specs=(pl.BlockSpec((1, A_HEADS, qa_blocks, LANES, Q_TILE_A),
                                lambda i, j: (i, 0, j, 0, 0)),
                   head(LANES), head_t(A_V_DIM), head_t(B_QK_PAD), head(B_QK_PAD),
                   head_t(B_V_DIM),
                   pl.BlockSpec((1, tr, A_WIDTH + B_WIDTH), lambda i, j: (i, j, 0))),
        compiler_params=pltpu.CompilerParams(
            dimension_semantics=("arbitrary", "arbitrary"), vmem_limit_bytes=VMEM_LIMIT),
        name="in_proj",
    )(x, mod, gpre, win, gcq, wuq, gckv, wukv, *tabs)


def _key_tile(k_ref, t):
    return k_ref[0, 0, pl.ds(pl.multiple_of(t * KV_TILE, KV_TILE), KV_TILE), :]


def _key_norm_max(k_ref, lane_spans):
    s = k_ref.shape[2]
    d = k_ref.shape[3]

    def body(c, carry):
        kk = k_ref[0, 0, pl.ds(pl.multiple_of(c * NORM_CHUNK, NORM_CHUNK), NORM_CHUNK), :]
        ksq = kk.astype(jnp.float32)
        ksq = ksq * ksq
        lane = lax.broadcasted_iota(jnp.int32, ksq.shape, 1)
        out = []
        for (lo, hi), cur in zip(lane_spans, carry):
            part = ksq if (lo, hi) == (0, d) else jnp.where((lane >= lo) & (lane < hi), ksq, 0.0)
            n2 = jnp.sum(part, axis=1, keepdims=True)
            out.append(jnp.maximum(cur, jnp.max(n2, axis=0, keepdims=True)))
        return tuple(out)

    init = tuple(jnp.zeros((1, 1), jnp.float32) for _ in lane_spans)
    res = lax.fori_loop(0, s // NORM_CHUNK, body, init)
    return [jnp.broadcast_to(r, (SUBLANES, LANES)) for r in res]


def _col_bound(qt, kmax_tiles, cols_per_map):
    qsq = qt.astype(jnp.float32)
    qsq = (qsq * qsq).astype(jnp.bfloat16)
    ones = jnp.ones((2 * SUBLANES, qt.shape[0]), jnp.bfloat16)
    q2 = _bdot(ones, qsq)[0:SUBLANES]
    kmax = jnp.concatenate(
        [t for t in kmax_tiles for _ in range(cols_per_map // LANES)], axis=1)
    return jnp.sqrt(q2 * kmax) * BOUND_MARGIN


def _running_max_pass(qt, k_ref, vt_ref, m_ref, l_ref, acc_ref):
    n = qt.shape[1]
    n_kv = k_ref.shape[2] // KV_TILE
    m_ref[...] = jnp.full(m_ref.shape, -jnp.inf, jnp.float32)
    l_ref[...] = jnp.zeros(l_ref.shape, jnp.float32)
    acc_ref[...] = jnp.zeros(acc_ref.shape, jnp.float32)

    def body(t, carry):
        s = _bdot(_key_tile(k_ref, t), qt).reshape(KV_TILE // SUBLANES, SUBLANES, n)
        m_old = m_ref[...]
        m_col = jnp.max(jnp.max(s, axis=0), axis=0, keepdims=True)
        m_new = jnp.maximum(m_old, m_col)
        alpha = jnp.exp2(m_old - m_new)
        p = jnp.exp2(s - m_new)
        l_ref[...] = alpha * l_ref[...] + jnp.sum(p, axis=0)
        pv = _bdot(vt_ref[0, 0, t], p.reshape(KV_TILE, n).astype(jnp.bfloat16))
        acc_ref[...] = alpha[0:1] * acc_ref[...] + pv
        m_ref[...] = m_new
        return carry

    lax.fori_loop(0, n_kv, body, 0)


def _attention_chunk(q_tile, finish, lane_spans, tq, k_ref, vt_ref,
                     kmax_ref, bound_ref, pa_ref, pb_ref, m_ref, l_ref, acc_ref):
    nq, _, n = bound_ref.shape
    n_grp = k_ref.shape[2] // (KV_TILE * KV_UNROLL)
    total = n_grp * nq
    log_nq = nq.bit_length() - 1
    assert nq == 1 << log_nq and total % 2 == 0 and total >= 2

    @pl.when(pl.program_id(2) == 0)
    def _():
        for idx, tile in enumerate(_key_norm_max(k_ref, lane_spans)):
            kmax_ref[idx] = tile

    kmax_tiles = [kmax_ref[idx] for idx in range(len(lane_spans))]
    for i in range(nq):
        bound_ref[i] = _col_bound(q_tile(i), kmax_tiles, tq)
    l_ref[...] = jnp.zeros(l_ref.shape, jnp.float32)
    acc_ref[...] = jnp.zeros(acc_ref.shape, jnp.float32)

    def numerators(step, p_ref):
        g, i = step >> log_nq, step & (nq - 1)
        qt = q_tile(i)
        bound = bound_ref[i]
        lsum = l_ref[i]
        rows = KV_UNROLL * KV_TILE
        keys = k_ref[0, 0, pl.ds(pl.multiple_of(g * rows, rows), rows), :]
        s = _bdot(keys, qt)
        p = jnp.exp2(s.reshape(rows // SUBLANES, SUBLANES, n) - bound)
        p_ref[...] = p.reshape(rows, n).astype(jnp.bfloat16)
        l_ref[i] = lsum + jnp.sum(p, axis=0)

    def weighted_values(step, p_ref):
        g, i = step >> log_nq, step & (nq - 1)
        vt = jnp.concatenate([vt_ref[0, 0, g * KV_UNROLL + u] for u in range(KV_UNROLL)], axis=1)
        acc_ref[i] += _bdot(vt, p_ref[...])

    p_refs = (pa_ref, pb_ref)

    def pipeline_step(step, parity):
        numerators(step + 1, p_refs[1 - parity])
        weighted_values(step, p_refs[parity])

    numerators(0, pa_ref)
    n_body = (total - 1) // PIPE_UNROLL

    def body(t, carry):
        for h in range(PIPE_UNROLL):
            pipeline_step(PIPE_UNROLL * t + h, h % 2)
        return carry

    lax.fori_loop(0, n_body, body, 0)
    for step in range(n_body * PIPE_UNROLL, total - 1):
        pipeline_step(step, step % 2)
    weighted_values(total - 1, p_refs[(total - 1) % 2])

    def usable(l_part):
        l_col = jnp.sum(l_part, axis=-2)
        return (jnp.min(l_col) >= L_MIN) & (jnp.max(l_col) <= L_MAX)

    def recompute_if_unusable(i, carry):
        @pl.when(jnp.logical_not(usable(l_ref[i])))
        def _():
            _running_max_pass(q_tile(i), k_ref, vt_ref, m_ref, l_ref.at[i], acc_ref.at[i])

        return carry

    @pl.when(jnp.logical_not(usable(l_ref[...])))
    def _():
        lax.fori_loop(0, nq, recompute_if_unusable, 0)

    def finish_tiles(t, carry):
        for h in range(FINISH_UNROLL):
            i = FINISH_UNROLL * t + h
            finish(i, acc_ref[i], jnp.sum(l_ref[i], axis=0, keepdims=True))
        return carry

    assert nq % FINISH_UNROLL == 0
    lax.fori_loop(0, nq // FINISH_UNROLL, finish_tiles, 0)


def _rows(i, tq):
    return pl.ds(pl.multiple_of(i * tq, tq), tq)


def _attn_a_kernel(lam_ref, gsub_ref, q_ref, k_ref, vt_ref, sg_ref, o_ref,
                   kmax_ref, bound_ref, pa_ref, pb_ref, m_ref, l_ref, acc_ref, *, lambda_init):
    tq = Q_TILE_A

    def q_tile(i):
        qt = q_ref[0, 0, i]
        row = lax.broadcasted_iota(jnp.int32, qt.shape, 0)
        zero = jnp.zeros_like(qt)
        return jnp.concatenate(
            [jnp.where(row < A_QK_DIM, qt, zero), jnp.where(row >= A_QK_DIM, qt, zero)], axis=1)

    lam_v = lam_ref[...]
    lam = (jnp.exp(jnp.sum(lam_v[0:1] * lam_v[1:2], axis=-1, keepdims=True))
           - jnp.exp(jnp.sum(lam_v[2:3] * lam_v[3:4], axis=-1, keepdims=True))
           + lambda_init)

    def finish(i, acc, l):
        ot = acc * (1.0 / l)
        o = (ot[:, 0:tq] - lam * ot[:, tq:2 * tq]).T
        o = _rms(o, gsub_ref[...]) * (1.0 - lambda_init)
        o_ref[0, _rows(i, tq), :] = (o * sg_ref[0, _rows(i, tq), :]).astype(o_ref.dtype)

    spans = ((0, A_QK_DIM), (A_QK_DIM, 2 * A_QK_DIM))
    _attention_chunk(q_tile, finish, spans, tq, k_ref, vt_ref,
                     kmax_ref, bound_ref, pa_ref, pb_ref, m_ref, l_ref, acc_ref)


def _attn_b_kernel(q_ref, k_ref, vt_ref, sg_ref, o_ref,
                   kmax_ref, bound_ref, pa_ref, pb_ref, m_ref, l_ref, acc_ref):
    tq = Q_TILE_B

    def q_tile(i):
        return q_ref[0, 0, i]

    def finish(i, acc, l):
        o = (acc * (1.0 / l)).T
        o_ref[0, _rows(i, tq), :] = (o * sg_ref[0, _rows(i, tq), :]).astype(o_ref.dtype)

    _attention_chunk(q_tile, finish, ((0, k_ref.shape[3]),), tq, k_ref, vt_ref,
                     kmax_ref, bound_ref, pa_ref, pb_ref, m_ref, l_ref, acc_ref)


def _attn_call(kern, name, extra_in, extra_specs, q, k, vt, sg, tq, n_maps, col0):
    b, nh, s, dq = k.shape
    dv = vt.shape[3]
    n = n_maps * tq
    qc = Q_CHUNK
    nq = qc // tq
    assert q.shape == (b, nh, s // tq, dq, tq)
    return pl.pallas_call(
        kern,
        out_shape=jax.ShapeDtypeStruct((b, s, nh * dv), jnp.bfloat16),
        grid=(b, nh, s // qc),
        in_specs=extra_specs + [
            pl.BlockSpec((1, 1, nq, dq, tq), lambda b_, h, c: (b_, h, c, 0, 0)),
            pl.BlockSpec((1, 1, s, dq), lambda b_, h, c: (b_, h, 0, 0)),
            pl.BlockSpec((1, 1, s // KV_TILE, dv, KV_TILE), lambda b_, h, c: (b_, h, 0, 0, 0)),
            pl.BlockSpec((1, qc, dv), lambda b_, h, c: (b_, c, col0 + h)),
        ],
        out_specs=pl.BlockSpec((1, qc, dv), lambda b_, h, c: (b_, c, h)),
        scratch_shapes=[
            pltpu.VMEM((n_maps, SUBLANES, LANES), jnp.float32),
            pltpu.VMEM((nq, SUBLANES, n), jnp.float32),
            pltpu.VMEM((KV_UNROLL * KV_TILE, n), jnp.bfloat16),
            pltpu.VMEM((KV_UNROLL * KV_TILE, n), jnp.bfloat16),
            pltpu.VMEM((SUBLANES, n), jnp.float32),
            pltpu.VMEM((nq, SUBLANES, n), jnp.float32),
            pltpu.VMEM((nq, dv, n), jnp.float32),
        ],
        compiler_params=pltpu.CompilerParams(
            dimension_semantics=("arbitrary", "arbitrary", "arbitrary"),
            vmem_limit_bytes=VMEM_LIMIT),
        name=name,
    )(*extra_in, q, k, vt, sg)


def _attn_a(lam_vecs, gsub, qa, ka, vta, sg, lambda_init):
    const = lambda shape: pl.BlockSpec(shape, lambda b_, h, c: (0, 0))
    return _attn_call(functools.partial(_attn_a_kernel, lambda_init=lambda_init), "attn_a",
                      (lam_vecs, gsub), [const((4, A_QK_DIM)), const((1, A_V_DIM))],
                      qa, ka, vta, sg, Q_TILE_A, 2, 0)


def _attn_b(qb, kb, vtb, sg):
    return _attn_call(_attn_b_kernel, "attn_b", (), [], qb, kb, vtb, sg, Q_TILE_B, 1, A_HEADS)


def _out_proj_kernel(ya_ref, yb_ref, x_ref, mod_ref, wout_ref, gpost_ref, o_ref):
    out = (_bdot(ya_ref[0], wout_ref[0:A_WIDTH, :])
           + _bdot(yb_ref[0], wout_ref[A_WIDTH:A_WIDTH + B_WIDTH, :]))
    gate = mod_ref[0][:, 2 * D_MODEL:3 * D_MODEL]
    o_ref[0] = x_ref[0] + gate * _rms(out, gpost_ref[...])


def _out_proj(ya, yb, x, mod, wout, gpost):
    b, s, _ = x.shape
    tr = ROW_TILE
    return pl.pallas_call(
        _out_proj_kernel,
        out_shape=jax.ShapeDtypeStruct(x.shape, x.dtype),
        grid=(b, s // tr),
        in_specs=[
            pl.BlockSpec((1, tr, A_WIDTH), lambda i, j: (i, j, 0)),
            pl.BlockSpec((1, tr, B_WIDTH), lambda i, j: (i, j, 0)),
            pl.BlockSpec((1, tr, D_MODEL), lambda i, j: (i, j, 0)),
            pl.BlockSpec((1, 1, 3 * D_MODEL), lambda i, j: (i, 0, 0)),
            pl.BlockSpec((A_WIDTH + B_WIDTH, D_MODEL), lambda i, j: (0, 0)),
            pl.BlockSpec((1, D_MODEL), lambda i, j: (0, 0)),
        ],
        out_specs=pl.BlockSpec((1, tr, D_MODEL), lambda i, j: (i, j, 0)),
        compiler_params=pltpu.CompilerParams(
            dimension_semantics=("arbitrary", "arbitrary"), vmem_limit_bytes=VMEM_LIMIT),
        name="out_proj",
    )(ya, yb, x, mod, wout, gpost)


def _rope_tables(seq, dim):
    half = dim // 2
    inv_freq = jnp.float32(ROPE_THETA) ** (-jnp.arange(0, dim, 2, dtype=jnp.float32) / dim)
    d = np.arange(LANES) % 64
    in_span, lo = d < dim, d < half
    freq = jnp.where(in_span, inv_freq[d % half], 0.0)
    ang = jnp.arange(seq, dtype=jnp.float32)[:, None] * freq[None, :]
    cos, sin = jnp.cos(ang), jnp.sin(ang)
    return cos, jnp.where(lo, -sin, 0.0), jnp.where(in_span & ~lo, sin, 0.0)


def _layout_w_in(w_in):
    aq, ak, av, ag, cq, ckv, kr, bg = jnp.split(
        w_in, [512, 1024, 1536, 2048, 2432, 2688, 2752], axis=1)
    kr = jnp.pad(kr, ((0, 0), (0, LANES - B_ROPE_DIM)))
    return jnp.concatenate([aq, ak, av, ag, cq, ckv, bg, kr], axis=1).astype(jnp.bfloat16)


def _layout_w_uq(w_uq):
    w = w_uq.reshape(Q_LORA, B_HEADS, B_NOPE_DIM + B_ROPE_DIM)
    nope = w[:, :, :B_NOPE_DIM].reshape(Q_LORA, B_HEADS * B_NOPE_DIM)
    rope = jnp.pad(w[:, :, B_NOPE_DIM:], ((0, 0), (0, 0), (0, LANES - B_ROPE_DIM)))
    return jnp.concatenate([nope, rope.reshape(Q_LORA, B_HEADS * LANES)], axis=1).astype(jnp.bfloat16)


def _encoder_layer(x, mod, layer, tabs, g_pre, w_in, lam_vecs, g_subln, g_cq, w_uq, g_ckv, w_ukv,
                   w_out, g_post):
    lambda_init = 0.8 - 0.6 * math.exp(-0.3 * layer)
    qa, ka, vta, qb, kb, vtb, sg = _in_proj(x, mod, g_pre, w_in, g_cq, w_uq, g_ckv, w_ukv, tabs)
    ya = _attn_a(lam_vecs, g_subln, qa, ka, vta, sg, lambda_init)
    yb = _attn_b(qb, kb, vtb, sg)
    return _out_proj(ya, yb, x, mod, w_out, g_post)


def kernel(x_prompt, x_sample, c_prompt, c_sample, w_ada, b_ada, g_pre, w_in, lambda_q1, lambda_k1,
           lambda_q2, lambda_k2, g_subln, g_cq, w_uq, g_ckv, w_ukv, w_out, g_post):
    depth = w_ada.shape[0]
    nb_p, nb_s = c_prompt.shape[0], c_sample.shape[0]
    pad_rows = -(nb_p + nb_s) % 8
    c_all = jnp.concatenate(
        [c_prompt, c_sample, jnp.zeros((pad_rows, D_MODEL), jnp.float32)], axis=0)
    max_seq = max(x_prompt.shape[1], x_sample.shape[1])
    tabs = _rope_tables(max_seq, A_ROPE_DIM) + _rope_tables(max_seq, B_ROPE_DIM)
    y_p, y_s = x_prompt, x_sample
    for layer in range(depth):
        mod = _adaln_mod(c_all, w_ada[layer], b_ada[layer])
        mod_p = mod[:nb_p].reshape(nb_p, 1, 3 * D_MODEL)
        mod_s = mod[nb_p:nb_p + nb_s].reshape(nb_s, 1, 3 * D_MODEL)
        lam_vecs = jnp.stack([lambda_q1[layer], lambda_k1[layer], lambda_q2[layer], lambda_k2[layer]])
        params = (g_pre[layer].reshape(1, -1), _layout_w_in(w_in[layer]), lam_vecs,
                  g_subln[layer].reshape(1, -1), g_cq[layer].reshape(1, -1),
                  _layout_w_uq(w_uq[layer]), g_ckv[layer].reshape(1, -1),
                  w_ukv[layer].astype(jnp.bfloat16), w_out[layer].astype(jnp.bfloat16),
                  g_post[layer].reshape(1, -1))
        y_p = _encoder_layer(y_p, mod_p, layer, tabs, *params)
        y_s = _encoder_layer(y_s, mod_s, layer, tabs, *params)
    return (y_p, y_s)
```

```python
import functools
import math

import numpy as np
import jax
import jax.numpy as jnp
from jax import lax
from jax.experimental import pallas as pl
from jax.experimental.pallas import tpu as pltpu

D_MODEL = 1024
A_HEADS = 4
A_QK_DIM = 64
A_V_DIM = 128
A_WIDTH = A_HEADS * A_V_DIM
A_ROPE_DIM = A_QK_DIM // 4
B_HEADS = 4
B_NOPE_DIM = 128
B_ROPE_DIM = 64
B_V_DIM = 128
B_WIDTH = B_HEADS * B_V_DIM
B_QK_PAD = 256
Q_LORA = 384
KV_LORA = 256
ROPE_THETA = 500000.0
NORM_EPS = 1e-6
LOG2E = 1.4426950408889634

LANES = 128
SUBLANES = 8
VMEM_LIMIT = 56 * 1024 * 1024

ROW_TILE = 512
OUT_ROW_TILE = 1024
KV_TILE = ROW_TILE
KV_UNROLL = 2
Q_TILE_A = 256
Q_TILE_B = 512
Q_CHUNKS = (8192, 4096, 2048)
ATTN_VMEM_BUDGET = 48 * 1024 * 1024
PIPE_UNROLL = 16
FINISH_UNROLL = 4
NORM_CHUNK = 1024

L_MIN = 2.0 ** -80
L_MAX = 2.0 ** 100
BOUND_MARGIN = 1.0 + 2.0 ** -5

SEG_AQ = 0
SEG_AK = 512
SEG_AV = 1024
SEG_AG = 1536
SEG_CQ = 2048
SEG_CKV = 2432
SEG_BG = 2688
SEG_KR = 3200
IN_COLS_PAD = 3328


def _silu(x):
    return x * (1.0 / (1.0 + jnp.exp(-x)))


def _rms(x, g):
    return x * lax.rsqrt(jnp.mean(x * x, axis=-1, keepdims=True) + NORM_EPS) * g


def _bdot(a, b):
    return jnp.dot(a, b, preferred_element_type=jnp.float32)


def _mod_kernel(c_ref, w_ref, b_ref, o_ref):
    c = _silu(c_ref[...]).astype(jnp.bfloat16)
    o_ref[...] = _bdot(c, w_ref[...].astype(jnp.bfloat16)) + b_ref[...]


def _adaln_mod(c_all, w_ada, b_ada):
    rows = c_all.shape[0]
    n_blk = 1024
    return pl.pallas_call(
        _mod_kernel,
        out_shape=jax.ShapeDtypeStruct((rows, 3 * D_MODEL), jnp.float32),
        grid=(3 * D_MODEL // n_blk,),
        in_specs=[
            pl.BlockSpec((rows, D_MODEL), lambda j: (0, 0)),
            pl.BlockSpec((D_MODEL, n_blk), lambda j: (0, j)),
            pl.BlockSpec((1, n_blk), lambda j: (0, j)),
        ],
        out_specs=pl.BlockSpec((rows, n_blk), lambda j: (0, j)),
        compiler_params=pltpu.CompilerParams(
            dimension_semantics=("arbitrary",), vmem_limit_bytes=VMEM_LIMIT),
        name="adaln_mod",
    )(c_all, w_ada, b_ada.reshape(1, -1))


def _rope(x, c, s_lo, s_hi, half):
    return (x * c + pltpu.roll(x, LANES - half, 1) * s_lo
            + pltpu.roll(x, half, 1) * s_hi)


def _in_proj_kernel(x_ref, mod_ref, gpre_ref, win_ref, gcq_ref, wuq_ref, gckv_ref, wukv_ref,
                    ca_ref, sa_lo_ref, sa_hi_ref, cb_ref, sb_lo_ref, sb_hi_ref,
                    qa_ref, ka_ref, va_ref, qb_ref, kb_ref, vb_ref, sg_ref,
                    *, scale_a, scale_b):
    mod = mod_ref[0]
    shift = mod[:, 0:D_MODEL]
    scale = mod[:, D_MODEL:2 * D_MODEL]
    half_a = A_ROPE_DIM // 2
    half_b = B_ROPE_DIM // 2
    bf = jnp.bfloat16

    h = (_rms(x_ref[0], gpre_ref[...]) * (1.0 + scale) + shift).astype(bf)

    def proj(lo, width):
        return _bdot(h, win_ref[:, lo:lo + width])

    ca, sa_lo, sa_hi = ca_ref[...], sa_lo_ref[...], sa_hi_ref[...]
    cb, sb_lo, sb_hi = cb_ref[...], sb_lo_ref[...], sb_hi_ref[...]

    cq = _rms(proj(SEG_CQ, Q_LORA), gcq_ref[...]).astype(bf)
    ckv = _rms(proj(SEG_CKV, KV_LORA), gckv_ref[...]).astype(bf)
    kr = _rope(proj(SEG_KR, LANES), cb, sb_lo, sb_hi, half_b).astype(bf)

    aq = proj(SEG_AQ, 512)
    ak = proj(SEG_AK, 512)
    av = proj(SEG_AV, 512)
    for hd in range(A_HEADS):
        sl = slice(hd * LANES, (hd + 1) * LANES)
        qt = (_rope(aq[:, sl], ca, sa_lo, sa_hi, half_a) * scale_a).T.astype(bf)
        for blk in range(ROW_TILE // Q_TILE_A):
            qa_ref[0, hd, blk] = qt[:, blk * Q_TILE_A:(blk + 1) * Q_TILE_A]
        ka_ref[0, hd] = _rope(ak[:, sl], ca, sa_lo, sa_hi, half_a).astype(bf)
        va_ref[0, hd, 0] = av[:, sl].T.astype(bf)

    q = _bdot(cq, wuq_ref[...])
    kv = _bdot(ckv, wukv_ref[...])
    for hd in range(B_HEADS):
        qn = q[:, hd * LANES:(hd + 1) * LANES]
        qr = q[:, (B_HEADS + hd) * LANES:(B_HEADS + hd + 1) * LANES]
        qb_ref[0, hd, 0, 0:LANES, :] = (qn * scale_b).T.astype(bf)
        qb_ref[0, hd, 0, LANES:2 * LANES, :] = (
            _rope(qr, cb, sb_lo, sb_hi, half_b) * scale_b).T.astype(bf)
        kb_ref[0, hd, :, 0:LANES] = kv[:, 2 * hd * LANES:(2 * hd + 1) * LANES].astype(bf)
        kb_ref[0, hd, :, LANES:2 * LANES] = kr
        vb_ref[0, hd, 0] = kv[:, (2 * hd + 1) * LANES:(2 * hd + 2) * LANES].T.astype(bf)

    sg_ref[0, :, 0:A_WIDTH] = _silu(proj(SEG_AG, A_WIDTH))
    sg_ref[0, :, A_WIDTH:A_WIDTH + B_WIDTH] = _silu(proj(SEG_BG, B_WIDTH))


def _in_proj(x, mod, gpre, win, gcq, wuq, gckv, wukv, tabs):
    b, s, _ = x.shape
    tr = ROW_TILE
    const = lambda shape: pl.BlockSpec(shape, lambda i, j: (0,) * len(shape))
    tab = pl.BlockSpec((tr, LANES), lambda i, j: (j, 0))
    head = lambda d: pl.BlockSpec((1, A_HEADS, tr, d), lambda i, j: (i, 0, j, 0))
    head_t = lambda d: pl.BlockSpec((1, A_HEADS, 1, d, tr), lambda i, j: (i, 0, j, 0, 0))
    qa_blocks = tr // Q_TILE_A
    assert tr == Q_TILE_B and tr == qa_blocks * Q_TILE_A
    bf = jnp.bfloat16
    kern = functools.partial(_in_proj_kernel,
                             scale_a=A_QK_DIM ** -0.5 * LOG2E,
                             scale_b=(B_NOPE_DIM + B_ROPE_DIM) ** -0.5 * LOG2E)
    return pl.pallas_call(
        kern,
        out_shape=(
            jax.ShapeDtypeStruct((b, A_HEADS, s // Q_TILE_A, LANES, Q_TILE_A), bf),
            jax.ShapeDtypeStruct((b, A_HEADS, s, LANES), bf),
            jax.ShapeDtypeStruct((b, A_HEADS, s // tr, A_V_DIM, tr), bf),
            jax.ShapeDtypeStruct((b, B_HEADS, s // tr, B_QK_PAD, tr), bf),
            jax.ShapeDtypeStruct((b, B_HEADS, s, B_QK_PAD), bf),
            jax.ShapeDtypeStruct((b, B_HEADS, s // tr, B_V_DIM, tr), bf),
            jax.ShapeDtypeStruct((b, s, A_WIDTH + B_WIDTH), jnp.float32),
        ),
        grid=(b, s // tr),
        in_specs=[
            pl.BlockSpec((1, tr, D_MODEL), lambda i, j: (i, j, 0)),
            pl.BlockSpec((1, 1, 3 * D_MODEL), lambda i, j: (i, 0, 0)),
            const((1, D_MODEL)),
            const((D_MODEL, IN_COLS_PAD)),
            const((1, Q_LORA)),
            const((Q_LORA, 2 * B_HEADS * LANES)),
            const((1, KV_LORA)),
            const((KV_LORA, B_HEADS * (B_NOPE_DIM + B_V_DIM))),
            tab, tab, tab, tab, tab, tab,
        ],
        out_specs=(pl.BlockSpec((1, A_HEADS, qa_blocks, LANES, Q_TILE_A),
                                lambda i, j: (i, 0, j, 0, 0)),
                   head(LANES), head_t(A_V_DIM), head_t(B_QK_PAD), head(B_QK_PAD),
                   head_t(B_V_DIM),
                   pl.BlockSpec((1, tr, A_WIDTH + B_WIDTH), lambda i, j: (i, j, 0))),
        compiler_params=pltpu.CompilerParams(
            dimension_semantics=("arbitrary", "arbitrary"), vmem_limit_bytes=VMEM_LIMIT),
        name="in_proj",
    )(x, mod, gpre, win, gcq, wuq, gckv, wukv, *tabs)


def _key_tile(k_ref, t):
    return k_ref[0, 0, pl.ds(pl.multiple_of(t * KV_TILE, KV_TILE), KV_TILE), :]


def _key_norm_max(k_ref, lane_spans):
    s = k_ref.shape[2]
    d = k_ref.shape[3]

    def body(c, carry):
        kk = k_ref[0, 0, pl.ds(pl.multiple_of(c * NORM_CHUNK, NORM_CHUNK), NORM_CHUNK), :]
        ksq = kk.astype(jnp.float32)
        ksq = ksq * ksq
        lane = lax.broadcasted_iota(jnp.int32, ksq.shape, 1)
        out = []
        for (lo, hi), cur in zip(lane_spans, carry):
            part = ksq if (lo, hi) == (0, d) else jnp.where((lane >= lo) & (lane < hi), ksq, 0.0)
            n2 = jnp.sum(part, axis=1, keepdims=True)
            out.append(jnp.maximum(cur, jnp.max(n2, axis=0, keepdims=True)))
        return tuple(out)

    init = tuple(jnp.zeros((1, 1), jnp.float32) for _ in lane_spans)
    res = lax.fori_loop(0, s // NORM_CHUNK, body, init)
    return [jnp.broadcast_to(r, (SUBLANES, LANES)) for r in res]


def _col_bound(qt, kmax_tiles, cols_per_map):
    qsq = qt.astype(jnp.float32)
    qsq = (qsq * qsq).astype(jnp.bfloat16)
    ones = jnp.ones((2 * SUBLANES, qt.shape[0]), jnp.bfloat16)
    q2 = _bdot(ones, qsq)[0:SUBLANES]
    kmax = jnp.concatenate(
        [t for t in kmax_tiles for _ in range(cols_per_map // LANES)], axis=1)
    return jnp.sqrt(q2 * kmax) * BOUND_MARGIN


def _running_max_pass(qt, k_ref, vt_ref, m_ref, l_ref, acc_ref):
    n = qt.shape[1]
    n_kv = k_ref.shape[2] // KV_TILE
    m_ref[...] = jnp.full(m_ref.shape, -jnp.inf, jnp.float32)
    l_ref[...] = jnp.zeros(l_ref.shape, jnp.float32)
    acc_ref[...] = jnp.zeros(acc_ref.shape, jnp.float32)

    def body(t, carry):
        s = _bdot(_key_tile(k_ref, t), qt).reshape(KV_TILE // SUBLANES, SUBLANES, n)
        m_old = m_ref[...]
        m_col = jnp.max(jnp.max(s, axis=0), axis=0, keepdims=True)
        m_new = jnp.maximum(m_old, m_col)
        alpha = jnp.exp2(m_old - m_new)
        p = jnp.exp2(s - m_new)
        l_ref[...] = alpha * l_ref[...] + jnp.sum(p, axis=0)
        pv = _bdot(vt_ref[0, 0, t], p.reshape(KV_TILE, n).astype(jnp.bfloat16))
        acc_ref[...] = alpha[0:1] * acc_ref[...] + pv
        m_ref[...] = m_new
        return carry

    lax.fori_loop(0, n_kv, body, 0)


def _attention_chunk(q_tile, finish, lane_spans, tq, k_ref, vt_ref,
                     kmax_ref, bound_ref, pa_ref, pb_ref, m_ref, l_ref, acc_ref):
    nq, _, n = bound_ref.shape
    n_grp = k_ref.shape[2] // (KV_TILE * KV_UNROLL)
    total = n_grp * nq
    log_nq = nq.bit_length() - 1
    assert nq == 1 << log_nq and total % 2 == 0 and total >= 2

    @pl.when(pl.program_id(2) == 0)
    def _():
        for idx, tile in enumerate(_key_norm_max(k_ref, lane_spans)):
            kmax_ref[idx] = tile

    kmax_tiles = [kmax_ref[idx] for idx in range(len(lane_spans))]
    for i in range(nq):
        bound_ref[i] = _col_bound(q_tile(i), kmax_tiles, tq)
    l_ref[...] = jnp.zeros(l_ref.shape, jnp.float32)
    acc_ref[...] = jnp.zeros(acc_ref.shape, jnp.float32)

    def numerators(step, p_ref):
        g, i = step >> log_nq, step & (nq - 1)
        qt = q_tile(i)
        bound = bound_ref[i]
        lsum = l_ref[i]
        rows = KV_UNROLL * KV_TILE
        keys = k_ref[0, 0, pl.ds(pl.multiple_of(g * rows, rows), rows), :]
        s = _bdot(keys, qt)
        p = jnp.exp2(s.reshape(rows // SUBLANES, SUBLANES, n) - bound)
        p_ref[...] = p.reshape(rows, n).astype(jnp.bfloat16)
        l_ref[i] = lsum + jnp.sum(p, axis=0)

    def weighted_values(step, p_ref):
        g, i = step >> log_nq, step & (nq - 1)
        vt = jnp.concatenate([vt_ref[0, 0, g * KV_UNROLL + u] for u in range(KV_UNROLL)], axis=1)
        acc_ref[i] += _bdot(vt, p_ref[...])

    p_refs = (pa_ref, pb_ref)

    def pipeline_step(step, parity):
        numerators(step + 1, p_refs[1 - parity])
        weighted_values(step, p_refs[parity])

    numerators(0, pa_ref)
    n_body = (total - 1) // PIPE_UNROLL

    def body(t, carry):
        for h in range(PIPE_UNROLL):
            pipeline_step(PIPE_UNROLL * t + h, h % 2)
        return carry

    lax.fori_loop(0, n_body, body, 0)
    for step in range(n_body * PIPE_UNROLL, total - 1):
        pipeline_step(step, step % 2)
    weighted_values(total - 1, p_refs[(total - 1) % 2])

    def usable(l_part):
        l_col = jnp.sum(l_part, axis=-2)
        return (jnp.min(l_col) >= L_MIN) & (jnp.max(l_col) <= L_MAX)

    def recompute_if_unusable(i, carry):
        @pl.when(jnp.logical_not(usable(l_ref[i])))
        def _():
            _running_max_pass(q_tile(i), k_ref, vt_ref, m_ref, l_ref.at[i], acc_ref.at[i])

        return carry

    @pl.when(jnp.logical_not(usable(l_ref[...])))
    def _():
        lax.fori_loop(0, nq, recompute_if_unusable, 0)

    def finish_tiles(t, carry):
        for h in range(FINISH_UNROLL):
            i = FINISH_UNROLL * t + h
            finish(i, acc_ref[i], jnp.sum(l_ref[i], axis=0, keepdims=True))
        return carry

    assert nq % FINISH_UNROLL == 0
    lax.fori_loop(0, nq // FINISH_UNROLL, finish_tiles, 0)


def _rows(i, tq):
    return pl.ds(pl.multiple_of(i * tq, tq), tq)


def _attn_a_kernel(lam_ref, gsub_ref, q_ref, k_ref, vt_ref, sg_ref, o_ref,
                   kmax_ref, bound_ref, pa_ref, pb_ref, m_ref, l_ref, acc_ref, *, lambda_init):
    tq = Q_TILE_A

    def q_tile(i):
        qt = q_ref[0, 0, i]
        row = lax.broadcasted_iota(jnp.int32, qt.shape, 0)
        zero = jnp.zeros_like(qt)
        return jnp.concatenate(
            [jnp.where(row < A_QK_DIM, qt, zero), jnp.where(row >= A_QK_DIM, qt, zero)], axis=1)

    lam_v = lam_ref[...]
    lam = (jnp.exp(jnp.sum(lam_v[0:1] * lam_v[1:2], axis=-1, keepdims=True))
           - jnp.exp(jnp.sum(lam_v[2:3] * lam_v[3:4], axis=-1, keepdims=True))
           + lambda_init)

    def finish(i, acc, l):
        ot = acc * (1.0 / l)
        o = (ot[:, 0:tq] - lam * ot[:, tq:2 * tq]).T
        o = _rms(o, gsub_ref[...]) * (1.0 - lambda_init)
        o_ref[0, _rows(i, tq), :] = (o * sg_ref[0, _rows(i, tq), :]).astype(o_ref.dtype)

    spans = ((0, A_QK_DIM), (A_QK_DIM, 2 * A_QK_DIM))
    _attention_chunk(q_tile, finish, spans, tq, k_ref, vt_ref,
                     kmax_ref, bound_ref, pa_ref, pb_ref, m_ref, l_ref, acc_ref)


def _attn_b_kernel(q_ref, k_ref, vt_ref, sg_ref, o_ref,
                   kmax_ref, bound_ref, pa_ref, pb_ref, m_ref, l_ref, acc_ref):
    tq = Q_TILE_B

    def q_tile(i):
        return q_ref[0, 0, i]

    def finish(i, acc, l):
        o = (acc * (1.0 / l)).T
        o_ref[0, _rows(i, tq), :] = (o * sg_ref[0, _rows(i, tq), :]).astype(o_ref.dtype)

    _attention_chunk(q_tile, finish, ((0, k_ref.shape[3]),), tq, k_ref, vt_ref,
                     kmax_ref, bound_ref, pa_ref, pb_ref, m_ref, l_ref, acc_ref)


def _query_chunk(s, dq, dv, tq, n):
    group = KV_UNROLL * KV_TILE
    for qc in Q_CHUNKS:
        if s % qc:
            continue
        windows = 2 * (s * dq * 2 + s * dv * 2 + qc * dq * 2 + qc * dv * 4 + qc * dv * 2)
        scratch = (qc // tq) * (dv + 2 * SUBLANES) * n * 4 + 2 * group * n * 2
        step_temps = 2 * group * n * 4
        if windows + scratch + step_temps <= ATTN_VMEM_BUDGET:
            return qc
    raise ValueError(f"no query chunk fits VMEM for sequence length {s}")


def _attn_call(kern, name, extra_in, extra_specs, q, k, vt, sg, tq, n_maps, col0):
    b, nh, s, dq = k.shape
    dv = vt.shape[3]
    n = n_maps * tq
    qc = _query_chunk(s, dq, dv, tq, n)
    nq = qc // tq
    assert q.shape == (b, nh, s // tq, dq, tq)
    return pl.pallas_call(
        kern,
        out_shape=jax.ShapeDtypeStruct((b, s, nh * dv), jnp.bfloat16),
        grid=(b, nh, s // qc),
        in_specs=extra_specs + [
            pl.BlockSpec((1, 1, nq, dq, tq), lambda b_, h, c: (b_, h, c, 0, 0)),
            pl.BlockSpec((1, 1, s, dq), lambda b_, h, c: (b_, h, 0, 0)),
            pl.BlockSpec((1, 1, s // KV_TILE, dv, KV_TILE), lambda b_, h, c: (b_, h, 0, 0, 0)),
            pl.BlockSpec((1, qc, dv), lambda b_, h, c: (b_, c, col0 + h)),
        ],
        out_specs=pl.BlockSpec((1, qc, dv), lambda b_, h, c: (b_, c, h)),
        scratch_shapes=[
            pltpu.VMEM((n_maps, SUBLANES, LANES), jnp.float32),
            pltpu.VMEM((nq, SUBLANES, n), jnp.float32),
            pltpu.VMEM((KV_UNROLL * KV_TILE, n), jnp.bfloat16),
            pltpu.VMEM((KV_UNROLL * KV_TILE, n), jnp.bfloat16),
            pltpu.VMEM((SUBLANES, n), jnp.float32),
            pltpu.VMEM((nq, SUBLANES, n), jnp.float32),
            pltpu.VMEM((nq, dv, n), jnp.float32),
        ],
        compiler_params=pltpu.CompilerParams(
            dimension_semantics=("arbitrary", "arbitrary", "arbitrary"),
            vmem_limit_bytes=VMEM_LIMIT),
        name=name,
    )(*extra_in, q, k, vt, sg)


def _attn_a(lam_vecs, gsub, qa, ka, vta, sg, lambda_init):
    const = lambda shape: pl.BlockSpec(shape, lambda b_, h, c: (0, 0))
    return _attn_call(functools.partial(_attn_a_kernel, lambda_init=lambda_init), "attn_a",
                      (lam_vecs, gsub), [const((4, A_QK_DIM)), const((1, A_V_DIM))],
                      qa, ka, vta, sg, Q_TILE_A, 2, 0)


def _attn_b(qb, kb, vtb, sg):
    return _attn_call(_attn_b_kernel, "attn_b", (), [], qb, kb, vtb, sg, Q_TILE_B, 1, A_HEADS)


def _out_proj_kernel(ya_ref, yb_ref, x_ref, mod_ref, wout_ref, gpost_ref, o_ref):
    out = (_bdot(ya_ref[0], wout_ref[0:A_WIDTH, :])
           + _bdot(yb_ref[0], wout_ref[A_WIDTH:A_WIDTH + B_WIDTH, :]))
    gate = mod_ref[0][:, 2 * D_MODEL:3 * D_MODEL]
    o_ref[0] = x_ref[0] + gate * _rms(out, gpost_ref[...])


def _out_proj(ya, yb, x, mod, wout, gpost):
    b, s, _ = x.shape
    tr = OUT_ROW_TILE
    return pl.pallas_call(
        _out_proj_kernel,
        out_shape=jax.ShapeDtypeStruct(x.shape, x.dtype),
        grid=(b, s // tr),
        in_specs=[
            pl.BlockSpec((1, tr, A_WIDTH), lambda i, j: (i, j, 0)),
            pl.BlockSpec((1, tr, B_WIDTH), lambda i, j: (i, j, 0)),
            pl.BlockSpec((1, tr, D_MODEL), lambda i, j: (i, j, 0)),
            pl.BlockSpec((1, 1, 3 * D_MODEL), lambda i, j: (i, 0, 0)),
            pl.BlockSpec((A_WIDTH + B_WIDTH, D_MODEL), lambda i, j: (0, 0)),
            pl.BlockSpec((1, D_MODEL), lambda i, j: (0, 0)),
        ],
        out_specs=pl.BlockSpec((1, tr, D_MODEL), lambda i, j: (i, j, 0)),
        compiler_params=pltpu.CompilerParams(
            dimension_semantics=("arbitrary", "arbitrary"), vmem_limit_bytes=VMEM_LIMIT),
        name="out_proj",
    )(ya, yb, x, mod, wout, gpost)


def _rope_tables(seq, dim):
    half = dim // 2
    inv_freq = jnp.float32(ROPE_THETA) ** (-jnp.arange(0, dim, 2, dtype=jnp.float32) / dim)
    d = np.arange(LANES) % 64
    in_span, lo = d < dim, d < half
    freq = jnp.where(in_span, inv_freq[d % half], 0.0)
    ang = jnp.arange(seq, dtype=jnp.float32)[:, None] * freq[None, :]
    cos, sin = jnp.cos(ang), jnp.sin(ang)
    return cos, jnp.where(lo, -sin, 0.0), jnp.where(in_span & ~lo, sin, 0.0)


def _layout_w_in(w_in):
    aq, ak, av, ag, cq, ckv, kr, bg = jnp.split(
        w_in, [512, 1024, 1536, 2048, 2432, 2688, 2752], axis=1)
    kr = jnp.pad(kr, ((0, 0), (0, LANES - B_ROPE_DIM)))
    return jnp.concatenate([aq, ak, av, ag, cq, ckv, bg, kr], axis=1).astype(jnp.bfloat16)


def _layout_w_uq(w_uq):
    w = w_uq.reshape(Q_LORA, B_HEADS, B_NOPE_DIM + B_ROPE_DIM)
    nope = w[:, :, :B_NOPE_DIM].reshape(Q_LORA, B_HEADS * B_NOPE_DIM)
    rope = jnp.pad(w[:, :, B_NOPE_DIM:], ((0, 0), (0, 0), (0, LANES - B_ROPE_DIM)))
    return jnp.concatenate([nope, rope.reshape(Q_LORA, B_HEADS * LANES)], axis=1).astype(jnp.bfloat16)


def _encoder_layer(x, mod, layer, tabs, g_pre, w_in, lam_vecs, g_subln, g_cq, w_uq, g_ckv, w_ukv,
                   w_out, g_post):
    lambda_init = 0.8 - 0.6 * math.exp(-0.3 * layer)
    qa, ka, vta, qb, kb, vtb, sg = _in_proj(x, mod, g_pre, w_in, g_cq, w_uq, g_ckv, w_ukv, tabs)
    ya = _attn_a(lam_vecs, g_subln, qa, ka, vta, sg, lambda_init)
    yb = _attn_b(qb, kb, vtb, sg)
    return _out_proj(ya, yb, x, mod, w_out, g_post)


def kernel(x_prompt, x_sample, c_prompt, c_sample, w_ada, b_ada, g_pre, w_in, lambda_q1, lambda_k1,
           lambda_q2, lambda_k2, g_subln, g_cq, w_uq, g_ckv, w_ukv, w_out, g_post):
    depth = w_ada.shape[0]
    nb_p, nb_s = c_prompt.shape[0], c_sample.shape[0]
    pad_rows = -(nb_p + nb_s) % 8
    c_all = jnp.concatenate(
        [c_prompt, c_sample, jnp.zeros((pad_rows, D_MODEL), jnp.float32)], axis=0)
    max_seq = max(x_prompt.shape[1], x_sample.shape[1])
    tabs = _rope_tables(max_seq, A_ROPE_DIM) + _rope_tables(max_seq, B_ROPE_DIM)
    y_p, y_s = x_prompt, x_sample
    for layer in range(depth):
        mod = _adaln_mod(c_all, w_ada[layer], b_ada[layer])
        mod_p = mod[:nb_p].reshape(nb_p, 1, 3 * D_MODEL)
        mod_s = mod[nb_p:nb_p + nb_s].reshape(nb_s, 1, 3 * D_MODEL)
        lam_vecs = jnp.stack([lambda_q1[layer], lambda_k1[layer], lambda_q2[layer], lambda_k2[layer]])
        params = (g_pre[layer].reshape(1, -1), _layout_w_in(w_in[layer]), lam_vecs,
                  g_subln[layer].reshape(1, -1), g_cq[layer].reshape(1, -1),
                  _layout_w_uq(w_uq[layer]), g_ckv[layer].reshape(1, -1),
                  w_ukv[layer].astype(jnp.bfloat16), w_out[layer].astype(jnp.bfloat16),
                  g_post[layer].reshape(1, -1))
        y_p = _encoder_layer(y_p, mod_p, layer, tabs, *params)
        y_s = _encoder_layer(y_s, mod_s, layer, tabs, *params)
    return (y_p, y_s)
```

```python
import functools
import math

import numpy as np
import jax
import jax.numpy as jnp
from jax import lax
from jax.experimental import pallas as pl
from jax.experimental.pallas import tpu as pltpu

D_MODEL = 1024
A_HEADS = 4
A_QK_DIM = 64
A_V_DIM = 128
A_WIDTH = A_HEADS * A_V_DIM
A_ROPE_DIM = A_QK_DIM // 4
B_HEADS = 4
B_NOPE_DIM = 128
B_ROPE_DIM = 64
B_V_DIM = 128
B_WIDTH = B_HEADS * B_V_DIM
B_QK_PAD = 256
A_QK_PAD = 256
Q_LORA = 384
KV_LORA = 256
ROPE_THETA = 500000.0
NORM_EPS = 1e-6
LOG2E = 1.4426950408889634

LANES = 128
SUBLANES = 8
VMEM_LIMIT = 56 * 1024 * 1024

ROW_TILE = 512
OUT_ROW_TILE = 1024
KV_TILE = ROW_TILE
KV_UNROLL = 2
Q_TILE_A = 256
Q_TILE_B = 512
Q_CHUNKS = (8192, 4096, 2048)
ATTN_VMEM_BUDGET = 48 * 1024 * 1024
PIPE_UNROLL = 16
FINISH_UNROLL = 4
NORM_CHUNK = 2048

L_MIN = 2.0 ** -80
L_MAX = 2.0 ** 100
BOUND_MARGIN = 1.0 + 2.0 ** -5

SEG_AQ = 0
SEG_AK = 512
SEG_AV = 1024
SEG_AG = 1536
SEG_CQ = 2048
SEG_CKV = 2432
SEG_BG = 2688
SEG_KR = 3200
IN_COLS_PAD = 3328


def _silu(x):
    return x * (1.0 / (1.0 + jnp.exp(-x)))


def _rms(x, g):
    return x * lax.rsqrt(jnp.mean(x * x, axis=-1, keepdims=True) + NORM_EPS) * g


def _bdot(a, b):
    return jnp.dot(a, b, preferred_element_type=jnp.float32)


def _mod_kernel(c_ref, w_ref, b_ref, o_ref):
    c = _silu(c_ref[...]).astype(jnp.bfloat16)
    o_ref[...] = _bdot(c, w_ref[...].astype(jnp.bfloat16)) + b_ref[...]


def _adaln_mod(c_all, w_ada, b_ada):
    rows = c_all.shape[0]
    n_blk = 1024
    return pl.pallas_call(
        _mod_kernel,
        out_shape=jax.ShapeDtypeStruct((rows, 3 * D_MODEL), jnp.float32),
        grid=(3 * D_MODEL // n_blk,),
        in_specs=[
            pl.BlockSpec((rows, D_MODEL), lambda j: (0, 0)),
            pl.BlockSpec((D_MODEL, n_blk), lambda j: (0, j)),
            pl.BlockSpec((1, n_blk), lambda j: (0, j)),
        ],
        out_specs=pl.BlockSpec((rows, n_blk), lambda j: (0, j)),
        compiler_params=pltpu.CompilerParams(
            dimension_semantics=("arbitrary",), vmem_limit_bytes=VMEM_LIMIT),
        name="adaln_mod",
    )(c_all, w_ada, b_ada.reshape(1, -1))


def _rope(x, c, s_lo, s_hi, half):
    return (x * c + pltpu.roll(x, LANES - half, 1) * s_lo
            + pltpu.roll(x, half, 1) * s_hi)


def _in_proj_kernel(x_ref, mod_ref, gpre_ref, win_ref, gcq_ref, wuq_ref, gckv_ref, wukv_ref,
                    ca_ref, sa_lo_ref, sa_hi_ref, cb_ref, sb_lo_ref, sb_hi_ref,
                    qa_ref, ka_ref, va_ref, qb_ref, kb_ref, vb_ref, sg_ref,
                    *, scale_a, scale_b):
    mod = mod_ref[0]
    shift = mod[:, 0:D_MODEL]
    scale = mod[:, D_MODEL:2 * D_MODEL]
    half_a = A_ROPE_DIM // 2
    half_b = B_ROPE_DIM // 2
    bf = jnp.bfloat16

    h = (_rms(x_ref[0], gpre_ref[...]) * (1.0 + scale) + shift).astype(bf)

    def proj(lo, width):
        return _bdot(h, win_ref[:, lo:lo + width])

    ca, sa_lo, sa_hi = ca_ref[...], sa_lo_ref[...], sa_hi_ref[...]
    cb, sb_lo, sb_hi = cb_ref[...], sb_lo_ref[...], sb_hi_ref[...]

    cq = _rms(proj(SEG_CQ, Q_LORA), gcq_ref[...]).astype(bf)
    ckv = _rms(proj(SEG_CKV, KV_LORA), gckv_ref[...]).astype(bf)
    kr = _rope(proj(SEG_KR, LANES), cb, sb_lo, sb_hi, half_b).astype(bf)

    aq = proj(SEG_AQ, 512)
    ak = proj(SEG_AK, 512)
    av = proj(SEG_AV, 512)
    for hd in range(A_HEADS):
        sl = slice(hd * LANES, (hd + 1) * LANES)
        qt = (_rope(aq[:, sl], ca, sa_lo, sa_hi, half_a) * scale_a).T.astype(bf)
        for blk in range(ROW_TILE // Q_TILE_A):
            qa_ref[0, hd, blk, 0:LANES, :] = qt[:, blk * Q_TILE_A:(blk + 1) * Q_TILE_A]
            qa_ref[0, hd, blk, LANES:A_QK_PAD, :] = jnp.zeros((A_QK_PAD - LANES, Q_TILE_A), bf)
        ka_ref[0, hd, :, 0:LANES] = _rope(ak[:, sl], ca, sa_lo, sa_hi, half_a).astype(bf)
        ka_ref[0, hd, :, LANES:A_QK_PAD] = jnp.zeros((ROW_TILE, A_QK_PAD - LANES), bf)
        va_ref[0, hd, 0] = av[:, sl].T.astype(bf)

    q = _bdot(cq, wuq_ref[...])
    kv = _bdot(ckv, wukv_ref[...])
    for hd in range(B_HEADS):
        qn = q[:, hd * LANES:(hd + 1) * LANES]
        qr = q[:, (B_HEADS + hd) * LANES:(B_HEADS + hd + 1) * LANES]
        qb_ref[0, hd, 0, 0:LANES, :] = (qn * scale_b).T.astype(bf)
        qb_ref[0, hd, 0, LANES:2 * LANES, :] = (
            _rope(qr, cb, sb_lo, sb_hi, half_b) * scale_b).T.astype(bf)
        kb_ref[0, hd, :, 0:LANES] = kv[:, 2 * hd * LANES:(2 * hd + 1) * LANES].astype(bf)
        kb_ref[0, hd, :, LANES:2 * LANES] = kr
        vb_ref[0, hd, 0] = kv[:, (2 * hd + 1) * LANES:(2 * hd + 2) * LANES].T.astype(bf)

    sg_ref[0, :, 0:A_WIDTH] = _silu(proj(SEG_AG, A_WIDTH))
    sg_ref[0, :, A_WIDTH:A_WIDTH + B_WIDTH] = _silu(proj(SEG_BG, B_WIDTH))


def _in_proj(x, mod, gpre, win, gcq, wuq, gckv, wukv, tabs):
    b, s, _ = x.shape
    tr = ROW_TILE
    const = lambda shape: pl.BlockSpec(shape, lambda i, j: (0,) * len(shape))
    tab = pl.BlockSpec((tr, LANES), lambda i, j: (j, 0))
    head = lambda d: pl.BlockSpec((1, A_HEADS, tr, d), lambda i, j: (i, 0, j, 0))
    head_t = lambda d: pl.BlockSpec((1, A_HEADS, 1, d, tr), lambda i, j: (i, 0, j, 0, 0))
    qa_blocks = tr // Q_TILE_A
    assert tr == Q_TILE_B and tr == qa_blocks * Q_TILE_A
    bf = jnp.bfloat16
    kern = functools.partial(_in_proj_kernel,
                             scale_a=A_QK_DIM ** -0.5 * LOG2E,
                             scale_b=(B_NOPE_DIM + B_ROPE_DIM) ** -0.5 * LOG2E)
    return pl.pallas_call(
        kern,
        out_shape=(
            jax.ShapeDtypeStruct((b, A_HEADS, s // Q_TILE_A, A_QK_PAD, Q_TILE_A), bf),
            jax.ShapeDtypeStruct((b, A_HEADS, s, A_QK_PAD), bf),
            jax.ShapeDtypeStruct((b, A_HEADS, s // tr, A_V_DIM, tr), bf),
            jax.ShapeDtypeStruct((b, B_HEADS, s // tr, B_QK_PAD, tr), bf),
            jax.ShapeDtypeStruct((b, B_HEADS, s, B_QK_PAD), bf),
            jax.ShapeDtypeStruct((b, B_HEADS, s // tr, B_V_DIM, tr), bf),
            jax.ShapeDtypeStruct((b, s, A_WIDTH + B_WIDTH), jnp.float32),
        ),
        grid=(b, s // tr),
        in_specs=[
            pl.BlockSpec((1, tr, D_MODEL), lambda i, j: (i, j, 0)),
            pl.BlockSpec((1, 1, 3 * D_MODEL), lambda i, j: (i, 0, 0)),
            const((1, D_MODEL)),
            const((D_MODEL, IN_COLS_PAD)),
            const((1, Q_LORA)),
            const((Q_LORA, 2 * B_HEADS * LANES)),
            const((1, KV_LORA)),
            const((KV_LORA, B_HEADS * (B_NOPE_DIM + B_V_DIM))),
            tab, tab, tab, tab, tab, tab,
        ],
        out_specs=(pl.BlockSpec((1, A_HEADS, qa_blocks, A_QK_PAD, Q_TILE_A),
                                lambda i, j: (i, 0, j, 0, 0)),
                   head(A_QK_PAD), head_t(A_V_DIM), head_t(B_QK_PAD), head(B_QK_PAD),
                   head_t(B_V_DIM),
                   pl.BlockSpec((1, tr, A_WIDTH + B_WIDTH), lambda i, j: (i, j, 0))),
        compiler_params=pltpu.CompilerParams(
            dimension_semantics=("arbitrary", "arbitrary"), vmem_limit_bytes=VMEM_LIMIT),
        name="in_proj",
    )(x, mod, gpre, win, gcq, wuq, gckv, wukv, *tabs)


def _key_tile(k_ref, t):
    return k_ref[0, 0, pl.ds(pl.multiple_of(t * KV_TILE, KV_TILE), KV_TILE), :]


def _key_norm_max(k_ref, lane_spans):
    s = k_ref.shape[2]
    d = k_ref.shape[3]
    width = LANES // len(lane_spans)
    row = lax.broadcasted_iota(jnp.int32, (d, LANES), 0)
    col = lax.broadcasted_iota(jnp.int32, (d, LANES), 1)
    sel = jnp.zeros((d, LANES), jnp.float32)
    for j, (lo, hi) in enumerate(lane_spans):
        routed = (row >= lo) & (row < hi) & (col >= j * width) & (col < (j + 1) * width)
        sel = jnp.where(routed, 1.0, sel)
    sel = sel.astype(jnp.bfloat16)

    def body(c, cur):
        kk = k_ref[0, 0, pl.ds(pl.multiple_of(c * NORM_CHUNK, NORM_CHUNK), NORM_CHUNK), :]
        ksq = kk.astype(jnp.float32)
        n2 = _bdot((ksq * ksq).astype(jnp.bfloat16), sel)
        return jnp.maximum(
            cur, jnp.max(n2.reshape(NORM_CHUNK // SUBLANES, SUBLANES, LANES), axis=0))

    res = lax.fori_loop(0, s // NORM_CHUNK, body, jnp.zeros((SUBLANES, LANES), jnp.float32))
    lane = lax.broadcasted_iota(jnp.int32, res.shape, 1)
    out = []
    for j in range(len(lane_spans)):
        block = jnp.where((lane >= j * width) & (lane < (j + 1) * width), res, 0.0)
        out.append(jnp.broadcast_to(jnp.max(block, axis=(0, 1), keepdims=True), res.shape))
    return out


def _col_bound(qt, kmax_tiles, cols_per_map):
    qsq = qt.astype(jnp.float32)
    qsq = (qsq * qsq).astype(jnp.bfloat16)
    ones = jnp.ones((2 * SUBLANES, qt.shape[0]), jnp.bfloat16)
    q2 = _bdot(ones, qsq)[0:SUBLANES]
    kmax = jnp.concatenate(
        [t for t in kmax_tiles for _ in range(cols_per_map // LANES)], axis=1)
    return jnp.sqrt(q2 * kmax) * BOUND_MARGIN


def _running_max_pass(qt, k_ref, vt_ref, m_ref, l_ref, acc_ref):
    n = qt.shape[1]
    n_kv = k_ref.shape[2] // KV_TILE
    m_ref[...] = jnp.full(m_ref.shape, -jnp.inf, jnp.float32)
    l_ref[...] = jnp.zeros(l_ref.shape, jnp.float32)
    acc_ref[...] = jnp.zeros(acc_ref.shape, jnp.float32)

    def body(t, carry):
        s = _bdot(_key_tile(k_ref, t), qt).reshape(KV_TILE // SUBLANES, SUBLANES, n)
        m_old = m_ref[...]
        m_col = jnp.max(jnp.max(s, axis=0), axis=0, keepdims=True)
        m_new = jnp.maximum(m_old, m_col)
        alpha = jnp.exp2(m_old - m_new)
        p = jnp.exp2(s - m_new)
        l_ref[...] = alpha * l_ref[...] + jnp.sum(p, axis=0)
        pv = _bdot(vt_ref[0, 0, t], p.reshape(KV_TILE, n).astype(jnp.bfloat16))
        acc_ref[...] = alpha[0:1] * acc_ref[...] + pv
        m_ref[...] = m_new
        return carry

    lax.fori_loop(0, n_kv, body, 0)


def _attention_chunk(q_tile, finish, lane_spans, tq, k_ref, vt_ref,
                     kmax_ref, bound_ref, pa_ref, pb_ref, m_ref, l_ref, acc_ref):
    nq, _, n = bound_ref.shape
    n_grp = k_ref.shape[2] // (KV_TILE * KV_UNROLL)
    total = n_grp * nq
    log_nq = nq.bit_length() - 1
    assert nq == 1 << log_nq and total % 2 == 0 and total >= 2

    @pl.when(pl.program_id(2) == 0)
    def _():
        for idx, tile in enumerate(_key_norm_max(k_ref, lane_spans)):
            kmax_ref[idx] = tile

    kmax_tiles = [kmax_ref[idx] for idx in range(len(lane_spans))]
    for i in range(nq):
        bound_ref[i] = _col_bound(q_tile(i), kmax_tiles, tq)
    l_ref[...] = jnp.zeros(l_ref.shape, jnp.float32)
    acc_ref[...] = jnp.zeros(acc_ref.shape, jnp.float32)

    def numerators(step, p_ref):
        g, i = step >> log_nq, step & (nq - 1)
        qt = q_tile(i)
        bound = bound_ref[i]
        lsum = l_ref[i]
        rows = KV_UNROLL * KV_TILE
        keys = k_ref[0, 0, pl.ds(pl.multiple_of(g * rows, rows), rows), :]
        s = _bdot(keys, qt)
        p = jnp.exp2(s.reshape(rows // SUBLANES, SUBLANES, n) - bound)
        p_ref[...] = p.reshape(rows, n).astype(jnp.bfloat16)
        l_ref[i] = lsum + jnp.sum(p, axis=0)

    def weighted_values(step, p_ref):
        g, i = step >> log_nq, step & (nq - 1)
        vt = jnp.concatenate([vt_ref[0, 0, g * KV_UNROLL + u] for u in range(KV_UNROLL)], axis=1)
        acc_ref[i] += _bdot(vt, p_ref[...])

    p_refs = (pa_ref, pb_ref)

    def pipeline_step(step, parity):
        numerators(step + 1, p_refs[1 - parity])
        weighted_values(step, p_refs[parity])

    numerators(0, pa_ref)
    n_body = (total - 1) // PIPE_UNROLL

    def body(t, carry):
        for h in range(PIPE_UNROLL):
            pipeline_step(PIPE_UNROLL * t + h, h % 2)
        return carry

    lax.fori_loop(0, n_body, body, 0)
    for step in range(n_body * PIPE_UNROLL, total - 1):
        pipeline_step(step, step % 2)
    weighted_values(total - 1, p_refs[(total - 1) % 2])

    def usable(l_part):
        l_col = jnp.sum(l_part, axis=-2)
        return (jnp.min(l_col) >= L_MIN) & (jnp.max(l_col) <= L_MAX)

    def recompute_if_unusable(i, carry):
        @pl.when(jnp.logical_not(usable(l_ref[i])))
        def _():
            _running_max_pass(q_tile(i), k_ref, vt_ref, m_ref, l_ref.at[i], acc_ref.at[i])

        return carry

    @pl.when(jnp.logical_not(usable(l_ref[...])))
    def _():
        lax.fori_loop(0, nq, recompute_if_unusable, 0)

    def finish_tiles(t, carry):
        for h in range(FINISH_UNROLL):
            i = FINISH_UNROLL * t + h
            finish(i, acc_ref[i], jnp.sum(l_ref[i], axis=0, keepdims=True))
        return carry

    assert nq % FINISH_UNROLL == 0
    lax.fori_loop(0, nq // FINISH_UNROLL, finish_tiles, 0)


def _rows(i, tq):
    return pl.ds(pl.multiple_of(i * tq, tq), tq)


def _attn_a_kernel(lam_ref, gsub_ref, q_ref, k_ref, vt_ref, sg_ref, o_ref,
                   kmax_ref, bound_ref, pa_ref, pb_ref, m_ref, l_ref, acc_ref, *, lambda_init):
    tq = Q_TILE_A

    def q_tile(i):
        qt = q_ref[0, 0, i]
        row = lax.broadcasted_iota(jnp.int32, qt.shape, 0)
        zero = jnp.zeros_like(qt)
        return jnp.concatenate(
            [jnp.where(row < A_QK_DIM, qt, zero), jnp.where(row >= A_QK_DIM, qt, zero)], axis=1)

    lam_v = lam_ref[...]
    lam = (jnp.exp(jnp.sum(lam_v[0:1] * lam_v[1:2], axis=-1, keepdims=True))
           - jnp.exp(jnp.sum(lam_v[2:3] * lam_v[3:4], axis=-1, keepdims=True))
           + lambda_init)

    def finish(i, acc, l):
        ot = acc * (1.0 / l)
        o = (ot[:, 0:tq] - lam * ot[:, tq:2 * tq]).T
        o = _rms(o, gsub_ref[...]) * (1.0 - lambda_init)
        o_ref[0, _rows(i, tq), :] = (o * sg_ref[0, _rows(i, tq), :]).astype(o_ref.dtype)

    spans = ((0, A_QK_DIM), (A_QK_DIM, 2 * A_QK_DIM))
    _attention_chunk(q_tile, finish, spans, tq, k_ref, vt_ref,
                     kmax_ref, bound_ref, pa_ref, pb_ref, m_ref, l_ref, acc_ref)


def _attn_b_kernel(q_ref, k_ref, vt_ref, sg_ref, o_ref,
                   kmax_ref, bound_ref, pa_ref, pb_ref, m_ref, l_ref, acc_ref):
    tq = Q_TILE_B

    def q_tile(i):
        return q_ref[0, 0, i]

    def finish(i, acc, l):
        o = (acc * (1.0 / l)).T
        o_ref[0, _rows(i, tq), :] = (o * sg_ref[0, _rows(i, tq), :]).astype(o_ref.dtype)

    _attention_chunk(q_tile, finish, ((0, k_ref.shape[3]),), tq, k_ref, vt_ref,
                     kmax_ref, bound_ref, pa_ref, pb_ref, m_ref, l_ref, acc_ref)


def _query_chunk(s, dq, dv, tq, n):
    group = KV_UNROLL * KV_TILE
    for qc in Q_CHUNKS:
        if s % qc:
            continue
        windows = 2 * (s * dq * 2 + s * dv * 2 + qc * dq * 2 + qc * dv * 4 + qc * dv * 2)
        scratch = (qc // tq) * (dv + 2 * SUBLANES) * n * 4 + 2 * group * n * 2
        step_temps = 2 * group * n * 4
        if windows + scratch + step_temps <= ATTN_VMEM_BUDGET:
            return qc
    raise ValueError(f"no query chunk fits VMEM for sequence length {s}")


def _attn_call(kern, name, extra_in, extra_specs, q, k, vt, sg, tq, n_maps, col0):
    b, nh, s, dq = k.shape
    dv = vt.shape[3]
    n = n_maps * tq
    qc = _query_chunk(s, dq, dv, tq, n)
    nq = qc // tq
    assert q.shape == (b, nh, s // tq, dq, tq)
    return pl.pallas_call(
        kern,
        out_shape=jax.ShapeDtypeStruct((b, s, nh * dv), jnp.bfloat16),
        grid=(b, nh, s // qc),
        in_specs=extra_specs + [
            pl.BlockSpec((1, 1, nq, dq, tq), lambda b_, h, c: (b_, h, c, 0, 0)),
            pl.BlockSpec((1, 1, s, dq), lambda b_, h, c: (b_, h, 0, 0)),
            pl.BlockSpec((1, 1, s // KV_TILE, dv, KV_TILE), lambda b_, h, c: (b_, h, 0, 0, 0)),
            pl.BlockSpec((1, qc, dv), lambda b_, h, c: (b_, c, col0 + h)),
        ],
        out_specs=pl.BlockSpec((1, qc, dv), lambda b_, h, c: (b_, c, h)),
        scratch_shapes=[
            pltpu.VMEM((n_maps, SUBLANES, LANES), jnp.float32),
            pltpu.VMEM((nq, SUBLANES, n), jnp.float32),
            pltpu.VMEM((KV_UNROLL * KV_TILE, n), jnp.bfloat16),
            pltpu.VMEM((KV_UNROLL * KV_TILE, n), jnp.bfloat16),
            pltpu.VMEM((SUBLANES, n), jnp.float32),
            pltpu.VMEM((nq, SUBLANES, n), jnp.float32),
            pltpu.VMEM((nq, dv, n), jnp.float32),
        ],
        compiler_params=pltpu.CompilerParams(
            dimension_semantics=("arbitrary", "arbitrary", "arbitrary"),
            vmem_limit_bytes=VMEM_LIMIT),
        name=name,
    )(*extra_in, q, k, vt, sg)


def _attn_a(lam_vecs, gsub, qa, ka, vta, sg, lambda_init):
    const = lambda shape: pl.BlockSpec(shape, lambda b_, h, c: (0, 0))
    return _attn_call(functools.partial(_attn_a_kernel, lambda_init=lambda_init), "attn_a",
                      (lam_vecs, gsub), [const((4, A_QK_DIM)), const((1, A_V_DIM))],
                      qa, ka, vta, sg, Q_TILE_A, 2, 0)


def _attn_b(qb, kb, vtb, sg):
    return _attn_call(_attn_b_kernel, "attn_b", (), [], qb, kb, vtb, sg, Q_TILE_B, 1, A_HEADS)


def _out_proj_kernel(ya_ref, yb_ref, x_ref, mod_ref, wout_ref, gpost_ref, o_ref):
    out = (_bdot(ya_ref[0], wout_ref[0:A_WIDTH, :])
           + _bdot(yb_ref[0], wout_ref[A_WIDTH:A_WIDTH + B_WIDTH, :]))
    gate = mod_ref[0][:, 2 * D_MODEL:3 * D_MODEL]
    o_ref[0] = x_ref[0] + gate * _rms(out, gpost_ref[...])


def _out_proj(ya, yb, x, mod, wout, gpost):
    b, s, _ = x.shape
    tr = OUT_ROW_TILE
    return pl.pallas_call(
        _out_proj_kernel,
        out_shape=jax.ShapeDtypeStruct(x.shape, x.dtype),
        grid=(b, s // tr),
        in_specs=[
            pl.BlockSpec((1, tr, A_WIDTH), lambda i, j: (i, j, 0)),
            pl.BlockSpec((1, tr, B_WIDTH), lambda i, j: (i, j, 0)),
            pl.BlockSpec((1, tr, D_MODEL), lambda i, j: (i, j, 0)),
            pl.BlockSpec((1, 1, 3 * D_MODEL), lambda i, j: (i, 0, 0)),
            pl.BlockSpec((A_WIDTH + B_WIDTH, D_MODEL), lambda i, j: (0, 0)),
            pl.BlockSpec((1, D_MODEL), lambda i, j: (0, 0)),
        ],
        out_specs=pl.BlockSpec((1, tr, D_MODEL), lambda i, j: (i, j, 0)),
        compiler_params=pltpu.CompilerParams(
            dimension_semantics=("arbitrary", "arbitrary"), vmem_limit_bytes=VMEM_LIMIT),
        name="out_proj",
    )(ya, yb, x, mod, wout, gpost)


def _rope_tables(seq, dim):
    half = dim // 2
    inv_freq = jnp.float32(ROPE_THETA) ** (-jnp.arange(0, dim, 2, dtype=jnp.float32) / dim)
    d = np.arange(LANES) % 64
    in_span, lo = d < dim, d < half
    freq = jnp.where(in_span, inv_freq[d % half], 0.0)
    ang = jnp.arange(seq, dtype=jnp.float32)[:, None] * freq[None, :]
    cos, sin = jnp.cos(ang), jnp.sin(ang)
    return cos, jnp.where(lo, -sin, 0.0), jnp.where(in_span & ~lo, sin, 0.0)


def _layout_w_in(w_in):
    aq, ak, av, ag, cq, ckv, kr, bg = jnp.split(
        w_in, [512, 1024, 1536, 2048, 2432, 2688, 2752], axis=1)
    kr = jnp.pad(kr, ((0, 0), (0, LANES - B_ROPE_DIM)))
    return jnp.concatenate([aq, ak, av, ag, cq, ckv, bg, kr], axis=1).astype(jnp.bfloat16)


def _layout_w_uq(w_uq):
    w = w_uq.reshape(Q_LORA, B_HEADS, B_NOPE_DIM + B_ROPE_DIM)
    nope = w[:, :, :B_NOPE_DIM].reshape(Q_LORA, B_HEADS * B_NOPE_DIM)
    rope = jnp.pad(w[:, :, B_NOPE_DIM:], ((0, 0), (0, 0), (0, LANES - B_ROPE_DIM)))
    return jnp.concatenate([nope, rope.reshape(Q_LORA, B_HEADS * LANES)], axis=1).astype(jnp.bfloat16)


def _encoder_layer(x, mod, layer, tabs, g_pre, w_in, lam_vecs, g_subln, g_cq, w_uq, g_ckv, w_ukv,
                   w_out, g_post):
    lambda_init = 0.8 - 0.6 * math.exp(-0.3 * layer)
    qa, ka, vta, qb, kb, vtb, sg = _in_proj(x, mod, g_pre, w_in, g_cq, w_uq, g_ckv, w_ukv, tabs)
    ya = _attn_a(lam_vecs, g_subln, qa, ka, vta, sg, lambda_init)
    yb = _attn_b(qb, kb, vtb, sg)
    return _out_proj(ya, yb, x, mod, w_out, g_post)


def kernel(x_prompt, x_sample, c_prompt, c_sample, w_ada, b_ada, g_pre, w_in, lambda_q1, lambda_k1,
           lambda_q2, lambda_k2, g_subln, g_cq, w_uq, g_ckv, w_ukv, w_out, g_post):
    depth = w_ada.shape[0]
    nb_p, nb_s = c_prompt.shape[0], c_sample.shape[0]
    pad_rows = -(nb_p + nb_s) % 8
    c_all = jnp.concatenate(
        [c_prompt, c_sample, jnp.zeros((pad_rows, D_MODEL), jnp.float32)], axis=0)
    max_seq = max(x_prompt.shape[1], x_sample.shape[1])
    tabs = _rope_tables(max_seq, A_ROPE_DIM) + _rope_tables(max_seq, B_ROPE_DIM)
    y_p, y_s = x_prompt, x_sample
    for layer in range(depth):
        mod = _adaln_mod(c_all, w_ada[layer], b_ada[layer])
        mod_p = mod[:nb_p].reshape(nb_p, 1, 3 * D_MODEL)
        mod_s = mod[nb_p:nb_p + nb_s].reshape(nb_s, 1, 3 * D_MODEL)
        lam_vecs = jnp.stack([lambda_q1[layer], lambda_k1[layer], lambda_q2[layer], lambda_k2[layer]])
        params = (g_pre[layer].reshape(1, -1), _layout_w_in(w_in[layer]), lam_vecs,
                  g_subln[layer].reshape(1, -1), g_cq[layer].reshape(1, -1),
                  _layout_w_uq(w_uq[layer]), g_ckv[layer].reshape(1, -1),
                  w_ukv[layer].astype(jnp.bfloat16), w_out[layer].astype(jnp.bfloat16),
                  g_post[layer].reshape(1, -1))
        y_p = _encoder_layer(y_p, mod_p, layer, tabs, *params)
        y_s = _encoder_layer(y_s, mod_s, layer, tabs, *params)
    return (y_p, y_s)
```

```python
import functools
import math

import numpy as np
import jax
import jax.numpy as jnp
from jax import lax
from jax.experimental import pallas as pl
from jax.experimental.pallas import tpu as pltpu

D_MODEL = 1024
A_HEADS = 4
A_QK_DIM = 64
A_V_DIM = 128
A_WIDTH = A_HEADS * A_V_DIM
A_ROPE_DIM = A_QK_DIM // 4
B_HEADS = 4
B_NOPE_DIM = 128
B_ROPE_DIM = 64
B_V_DIM = 128
B_WIDTH = B_HEADS * B_V_DIM
B_QK_PAD = 256
Q_LORA = 384
KV_LORA = 256
ROPE_THETA = 500000.0
NORM_EPS = 1e-6
LOG2E = 1.4426950408889634

LANES = 128
SUBLANES = 8
VMEM_LIMIT = 56 * 1024 * 1024

ROW_TILE = 512
OUT_ROW_TILE = 1024
KV_TILE = ROW_TILE
KV_UNROLL = 2
Q_TILE_A = 256
Q_TILE_B = 512
Q_CHUNKS = (8192, 4096, 2048)
ATTN_VMEM_BUDGET = 48 * 1024 * 1024
PIPE_UNROLL = 32
FINISH_UNROLL = 4
NORM_CHUNK = 1024

L_MIN = 2.0 ** -80
L_MAX = 2.0 ** 100
BOUND_MARGIN = 1.0 + 2.0 ** -5

SEG_AQ = 0
SEG_AK = 512
SEG_AV = 1024
SEG_AG = 1536
SEG_CQ = 2048
SEG_CKV = 2432
SEG_BG = 2688
SEG_KR = 3200
IN_COLS_PAD = 3328


def _silu(x):
    return x * (1.0 / (1.0 + jnp.exp(-x)))


def _rms(x, g):
    return x * lax.rsqrt(jnp.mean(x * x, axis=-1, keepdims=True) + NORM_EPS) * g


def _bdot(a, b):
    return jnp.dot(a, b, preferred_element_type=jnp.float32)


def _mod_kernel(c_ref, w_ref, b_ref, o_ref):
    c = _silu(c_ref[...]).astype(jnp.bfloat16)
    o_ref[...] = _bdot(c, w_ref[...].astype(jnp.bfloat16)) + b_ref[...]


def _adaln_mod(c_all, w_ada, b_ada):
    rows = c_all.shape[0]
    n_blk = 1024
    return pl.pallas_call(
        _mod_kernel,
        out_shape=jax.ShapeDtypeStruct((rows, 3 * D_MODEL), jnp.float32),
        grid=(3 * D_MODEL // n_blk,),
        in_specs=[
            pl.BlockSpec((rows, D_MODEL), lambda j: (0, 0)),
            pl.BlockSpec((D_MODEL, n_blk), lambda j: (0, j)),
            pl.BlockSpec((1, n_blk), lambda j: (0, j)),
        ],
        out_specs=pl.BlockSpec((rows, n_blk), lambda j: (0, j)),
        compiler_params=pltpu.CompilerParams(
            dimension_semantics=("arbitrary",), vmem_limit_bytes=VMEM_LIMIT),
        name="adaln_mod",
    )(c_all, w_ada, b_ada.reshape(1, -1))


def _rope(x, c, s_lo, s_hi, half):
    return (x * c + pltpu.roll(x, LANES - half, 1) * s_lo
            + pltpu.roll(x, half, 1) * s_hi)


def _in_proj_kernel(x_ref, mod_ref, gpre_ref, win_ref, gcq_ref, wuq_ref, gckv_ref, wukv_ref,
                    ca_ref, sa_lo_ref, sa_hi_ref, cb_ref, sb_lo_ref, sb_hi_ref,
                    qa_ref, ka_ref, va_ref, qb_ref, kb_ref, vb_ref, sg_ref,
                    *, scale_a, scale_b):
    mod = mod_ref[0]
    shift = mod[:, 0:D_MODEL]
    scale = mod[:, D_MODEL:2 * D_MODEL]
    half_a = A_ROPE_DIM // 2
    half_b = B_ROPE_DIM // 2
    bf = jnp.bfloat16

    h = (_rms(x_ref[0], gpre_ref[...]) * (1.0 + scale) + shift).astype(bf)

    def proj(lo, width):
        return _bdot(h, win_ref[:, lo:lo + width])

    ca, sa_lo, sa_hi = ca_ref[...], sa_lo_ref[...], sa_hi_ref[...]
    cb, sb_lo, sb_hi = cb_ref[...], sb_lo_ref[...], sb_hi_ref[...]

    cq = _rms(proj(SEG_CQ, Q_LORA), gcq_ref[...]).astype(bf)
    ckv = _rms(proj(SEG_CKV, KV_LORA), gckv_ref[...]).astype(bf)
    kr = _rope(proj(SEG_KR, LANES), cb, sb_lo, sb_hi, half_b).astype(bf)

    aq = proj(SEG_AQ, 512)
    ak = proj(SEG_AK, 512)
    av = proj(SEG_AV, 512)
    for hd in range(A_HEADS):
        sl = slice(hd * LANES, (hd + 1) * LANES)
        qt = (_rope(aq[:, sl], ca, sa_lo, sa_hi, half_a) * scale_a).T.astype(bf)
        for blk in range(ROW_TILE // Q_TILE_A):
            qa_ref[0, hd, blk] = qt[:, blk * Q_TILE_A:(blk + 1) * Q_TILE_A]
        ka_ref[0, hd] = _rope(ak[:, sl], ca, sa_lo, sa_hi, half_a).astype(bf)
        va_ref[0, hd, 0] = av[:, sl].T.astype(bf)

    q = _bdot(cq, wuq_ref[...])
    kv = _bdot(ckv, wukv_ref[...])
    for hd in range(B_HEADS):
        qn = q[:, hd * LANES:(hd + 1) * LANES]
        qr = q[:, (B_HEADS + hd) * LANES:(B_HEADS + hd + 1) * LANES]
        qb_ref[0, hd, 0, 0:LANES, :] = (qn * scale_b).T.astype(bf)
        qb_ref[0, hd, 0, LANES:2 * LANES, :] = (
            _rope(qr, cb, sb_lo, sb_hi, half_b) * scale_b).T.astype(bf)
        kb_ref[0, hd, :, 0:LANES] = kv[:, 2 * hd * LANES:(2 * hd + 1) * LANES].astype(bf)
        kb_ref[0, hd, :, LANES:2 * LANES] = kr
        vb_ref[0, hd, 0] = kv[:, (2 * hd + 1) * LANES:(2 * hd + 2) * LANES].T.astype(bf)

    sg_ref[0, :, 0:A_WIDTH] = _silu(proj(SEG_AG, A_WIDTH))
    sg_ref[0, :, A_WIDTH:A_WIDTH + B_WIDTH] = _silu(proj(SEG_BG, B_WIDTH))


def _in_proj(x, mod, gpre, win, gcq, wuq, gckv, wukv, tabs):
    b, s, _ = x.shape
    tr = ROW_TILE
    const = lambda shape: pl.BlockSpec(shape, lambda i, j: (0,) * len(shape))
    tab = pl.BlockSpec((tr, LANES), lambda i, j: (j, 0))
    head = lambda d: pl.BlockSpec((1, A_HEADS, tr, d), lambda i, j: (i, 0, j, 0))
    head_t = lambda d: pl.BlockSpec((1, A_HEADS, 1, d, tr), lambda i, j: (i, 0, j, 0, 0))
    qa_blocks = tr // Q_TILE_A
    assert tr == Q_TILE_B and tr == qa_blocks * Q_TILE_A
    bf = jnp.bfloat16
    kern = functools.partial(_in_proj_kernel,
                             scale_a=A_QK_DIM ** -0.5 * LOG2E,
                             scale_b=(B_NOPE_DIM + B_ROPE_DIM) ** -0.5 * LOG2E)
    return pl.pallas_call(
        kern,
        out_shape=(
            jax.ShapeDtypeStruct((b, A_HEADS, s // Q_TILE_A, LANES, Q_TILE_A), bf),
            jax.ShapeDtypeStruct((b, A_HEADS, s, LANES), bf),
            jax.ShapeDtypeStruct((b, A_HEADS, s // tr, A_V_DIM, tr), bf),
            jax.ShapeDtypeStruct((b, B_HEADS, s // tr, B_QK_PAD, tr), bf),
            jax.ShapeDtypeStruct((b, B_HEADS, s, B_QK_PAD), bf),
            jax.ShapeDtypeStruct((b, B_HEADS, s // tr, B_V_DIM, tr), bf),
            jax.ShapeDtypeStruct((b, s, A_WIDTH + B_WIDTH), jnp.float32),
        ),
        grid=(b, s // tr),
        in_specs=[
            pl.BlockSpec((1, tr, D_MODEL), lambda i, j: (i, j, 0)),
            pl.BlockSpec((1, 1, 3 * D_MODEL), lambda i, j: (i, 0, 0)),
            const((1, D_MODEL)),
            const((D_MODEL, IN_COLS_PAD)),
            const((1, Q_LORA)),
            const((Q_LORA, 2 * B_HEADS * LANES)),
            const((1, KV_LORA)),
            const((KV_LORA, B_HEADS * (B_NOPE_DIM + B_V_DIM))),
            tab, tab, tab, tab, tab, tab,
        ],
        out_specs=(pl.BlockSpec((1, A_HEADS, qa_blocks, LANES, Q_TILE_A),
                                lambda i, j: (i, 0, j, 0, 0)),
                   head(LANES), head_t(A_V_DIM), head_t(B_QK_PAD), head(B_QK_PAD),
                   head_t(B_V_DIM),
                   pl.BlockSpec((1, tr, A_WIDTH + B_WIDTH), lambda i, j: (i, j, 0))),
        compiler_params=pltpu.CompilerParams(
            dimension_semantics=("arbitrary", "arbitrary"), vmem_limit_bytes=VMEM_LIMIT),
        name="in_proj",
    )(x, mod, gpre, win, gcq, wuq, gckv, wukv, *tabs)


def _key_tile(k_ref, t):
    return k_ref[0, 0, pl.ds(pl.multiple_of(t * KV_TILE, KV_TILE), KV_TILE), :]


def _key_norm_max(k_ref, lane_spans):
    s = k_ref.shape[2]
    d = k_ref.shape[3]

    def body(c, carry):
        kk = k_ref[0, 0, pl.ds(pl.multiple_of(c * NORM_CHUNK, NORM_CHUNK), NORM_CHUNK), :]
        ksq = kk.astype(jnp.float32)
        ksq = ksq * ksq
        lane = lax.broadcasted_iota(jnp.int32, ksq.shape, 1)
        out = []
        for (lo, hi), cur in zip(lane_spans, carry):
            part = ksq if (lo, hi) == (0, d) else jnp.where((lane >= lo) & (lane < hi), ksq, 0.0)
            n2 = jnp.sum(part, axis=1, keepdims=True)
            out.append(jnp.maximum(cur, jnp.max(n2, axis=0, keepdims=True)))
        return tuple(out)

    init = tuple(jnp.zeros((1, 1), jnp.float32) for _ in lane_spans)
    res = lax.fori_loop(0, s // NORM_CHUNK, body, init)
    return [jnp.broadcast_to(r, (SUBLANES, LANES)) for r in res]


def _col_bound(qt, kmax_tiles, cols_per_map):
    qsq = qt.astype(jnp.float32)
    qsq = (qsq * qsq).astype(jnp.bfloat16)
    ones = jnp.ones((2 * SUBLANES, qt.shape[0]), jnp.bfloat16)
    q2 = _bdot(ones, qsq)[0:SUBLANES]
    kmax = jnp.concatenate(
        [t for t in kmax_tiles for _ in range(cols_per_map // LANES)], axis=1)
    return jnp.sqrt(q2 * kmax) * BOUND_MARGIN


def _running_max_pass(qt, k_ref, vt_ref, m_ref, l_ref, acc_ref):
    n = qt.shape[1]
    n_kv = k_ref.shape[2] // KV_TILE
    m_ref[...] = jnp.full(m_ref.shape, -jnp.inf, jnp.float32)
    l_ref[...] = jnp.zeros(l_ref.shape, jnp.float32)
    acc_ref[...] = jnp.zeros(acc_ref.shape, jnp.float32)

    def body(t, carry):
        s = _bdot(_key_tile(k_ref, t), qt).reshape(KV_TILE // SUBLANES, SUBLANES, n)
        m_old = m_ref[...]
        m_col = jnp.max(jnp.max(s, axis=0), axis=0, keepdims=True)
        m_new = jnp.maximum(m_old, m_col)
        alpha = jnp.exp2(m_old - m_new)
        p = jnp.exp2(s - m_new)
        l_ref[...] = alpha * l_ref[...] + jnp.sum(p, axis=0)
        pv = _bdot(vt_ref[0, 0, t], p.reshape(KV_TILE, n).astype(jnp.bfloat16))
        acc_ref[...] = alpha[0:1] * acc_ref[...] + pv
        m_ref[...] = m_new
        return carry

    lax.fori_loop(0, n_kv, body, 0)


def _attention_chunk(q_tile, finish, lane_spans, tq, k_ref, vt_ref,
                     kmax_ref, bound_ref, pa_ref, pb_ref, m_ref, l_ref, acc_ref):
    nq, _, n = bound_ref.shape
    n_grp = k_ref.shape[2] // (KV_TILE * KV_UNROLL)
    total = n_grp * nq
    log_nq = nq.bit_length() - 1
    assert nq == 1 << log_nq and total % 2 == 0 and total >= 2

    @pl.when(pl.program_id(2) == 0)
    def _():
        for idx, tile in enumerate(_key_norm_max(k_ref, lane_spans)):
            kmax_ref[idx] = tile

    kmax_tiles = [kmax_ref[idx] for idx in range(len(lane_spans))]
    for i in range(nq):
        bound_ref[i] = _col_bound(q_tile(i), kmax_tiles, tq)
    l_ref[...] = jnp.zeros(l_ref.shape, jnp.float32)
    acc_ref[...] = jnp.zeros(acc_ref.shape, jnp.float32)

    def numerators(step, p_ref):
        g, i = step >> log_nq, step & (nq - 1)
        qt = q_tile(i)
        bound = bound_ref[i]
        lsum = l_ref[i]
        rows = KV_UNROLL * KV_TILE
        keys = k_ref[0, 0, pl.ds(pl.multiple_of(g * rows, rows), rows), :]
        s = _bdot(keys, qt)
        p = jnp.exp2(s.reshape(rows // SUBLANES, SUBLANES, n) - bound)
        p_ref[...] = p.reshape(rows, n).astype(jnp.bfloat16)
        l_ref[i] = lsum + jnp.sum(p, axis=0)

    def weighted_values(step, p_ref):
        g, i = step >> log_nq, step & (nq - 1)
        vt = jnp.concatenate([vt_ref[0, 0, g * KV_UNROLL + u] for u in range(KV_UNROLL)], axis=1)
        acc_ref[i] += _bdot(vt, p_ref[...])

    p_refs = (pa_ref, pb_ref)

    def pipeline_step(step, parity):
        numerators(step + 1, p_refs[1 - parity])
        weighted_values(step, p_refs[parity])

    numerators(0, pa_ref)
    n_body = (total - 1) // PIPE_UNROLL

    def body(t, carry):
        for h in range(PIPE_UNROLL):
            pipeline_step(PIPE_UNROLL * t + h, h % 2)
        return carry

    lax.fori_loop(0, n_body, body, 0)
    for step in range(n_body * PIPE_UNROLL, total - 1):
        pipeline_step(step, step % 2)
    weighted_values(total - 1, p_refs[(total - 1) % 2])

    def usable(l_part):
        l_col = jnp.sum(l_part, axis=-2)
        return (jnp.min(l_col) >= L_MIN) & (jnp.max(l_col) <= L_MAX)

    def recompute_if_unusable(i, carry):
        @pl.when(jnp.logical_not(usable(l_ref[i])))
        def _():
            _running_max_pass(q_tile(i), k_ref, vt_ref, m_ref, l_ref.at[i], acc_ref.at[i])

        return carry

    @pl.when(jnp.logical_not(usable(l_ref[...])))
    def _():
        lax.fori_loop(0, nq, recompute_if_unusable, 0)

    def finish_tiles(t, carry):
        for h in range(FINISH_UNROLL):
            i = FINISH_UNROLL * t + h
            finish(i, acc_ref[i], jnp.sum(l_ref[i], axis=0, keepdims=True))
        return carry

    assert nq % FINISH_UNROLL == 0
    lax.fori_loop(0, nq // FINISH_UNROLL, finish_tiles, 0)


def _rows(i, tq):
    return pl.ds(pl.multiple_of(i * tq, tq), tq)


def _attn_a_kernel(lam_ref, gsub_ref, q_ref, k_ref, vt_ref, sg_ref, o_ref,
                   kmax_ref, bound_ref, pa_ref, pb_ref, m_ref, l_ref, acc_ref, *, lambda_init):
    tq = Q_TILE_A

    def q_tile(i):
        qt = q_ref[0, 0, i]
        row = lax.broadcasted_iota(jnp.int32, qt.shape, 0)
        zero = jnp.zeros_like(qt)
        return jnp.concatenate(
            [jnp.where(row < A_QK_DIM, qt, zero), jnp.where(row >= A_QK_DIM, qt, zero)], axis=1)

    lam_v = lam_ref[...]
    lam = (jnp.exp(jnp.sum(lam_v[0:1] * lam_v[1:2], axis=-1, keepdims=True))
           - jnp.exp(jnp.sum(lam_v[2:3] * lam_v[3:4], axis=-1, keepdims=True))
           + lambda_init)

    def finish(i, acc, l):
        ot = acc * (1.0 / l)
        o = (ot[:, 0:tq] - lam * ot[:, tq:2 * tq]).T
        o = _rms(o, gsub_ref[...]) * (1.0 - lambda_init)
        o_ref[0, _rows(i, tq), :] = (o * sg_ref[0, _rows(i, tq), :]).astype(o_ref.dtype)

    spans = ((0, A_QK_DIM), (A_QK_DIM, 2 * A_QK_DIM))
    _attention_chunk(q_tile, finish, spans, tq, k_ref, vt_ref,
                     kmax_ref, bound_ref, pa_ref, pb_ref, m_ref, l_ref, acc_ref)


def _attn_b_kernel(q_ref, k_ref, vt_ref, sg_ref, o_ref,
                   kmax_ref, bound_ref, pa_ref, pb_ref, m_ref, l_ref, acc_ref):
    tq = Q_TILE_B

    def q_tile(i):
        return q_ref[0, 0, i]

    def finish(i, acc, l):
        o = (acc * (1.0 / l)).T
        o_ref[0, _rows(i, tq), :] = (o * sg_ref[0, _rows(i, tq), :]).astype(o_ref.dtype)

    _attention_chunk(q_tile, finish, ((0, k_ref.shape[3]),), tq, k_ref, vt_ref,
                     kmax_ref, bound_ref, pa_ref, pb_ref, m_ref, l_ref, acc_ref)


def _query_chunk(s, dq, dv, tq, n):
    group = KV_UNROLL * KV_TILE
    for qc in Q_CHUNKS:
        if s % qc:
            continue
        windows = 2 * (s * dq * 2 + s * dv * 2 + qc * dq * 2 + qc * dv * 4 + qc * dv * 2)
        scratch = (qc // tq) * (dv + 2 * SUBLANES) * n * 4 + 2 * group * n * 2
        step_temps = 2 * group * n * 4
        if windows + scratch + step_temps <= ATTN_VMEM_BUDGET:
            return qc
    raise ValueError(f"no query chunk fits VMEM for sequence length {s}")


def _attn_call(kern, name, extra_in, extra_specs, q, k, vt, sg, tq, n_maps, col0):
    b, nh, s, dq = k.shape
    dv = vt.shape[3]
    n = n_maps * tq
    qc = _query_chunk(s, dq, dv, tq, n)
    nq = qc // tq
    assert q.shape == (b, nh, s // tq, dq, tq)
    return pl.pallas_call(
        kern,
        out_shape=jax.ShapeDtypeStruct((b, s, nh * dv), jnp.bfloat16),
        grid=(b, nh, s // qc),
        in_specs=extra_specs + [
            pl.BlockSpec((1, 1, nq, dq, tq), lambda b_, h, c: (b_, h, c, 0, 0)),
            pl.BlockSpec((1, 1, s, dq), lambda b_, h, c: (b_, h, 0, 0)),
            pl.BlockSpec((1, 1, s // KV_TILE, dv, KV_TILE), lambda b_, h, c: (b_, h, 0, 0, 0)),
            pl.BlockSpec((1, qc, dv), lambda b_, h, c: (b_, c, col0 + h)),
        ],
        out_specs=pl.BlockSpec((1, qc, dv), lambda b_, h, c: (b_, c, h)),
        scratch_shapes=[
            pltpu.VMEM((n_maps, SUBLANES, LANES), jnp.float32),
            pltpu.VMEM((nq, SUBLANES, n), jnp.float32),
            pltpu.VMEM((KV_UNROLL * KV_TILE, n), jnp.bfloat16),
            pltpu.VMEM((KV_UNROLL * KV_TILE, n), jnp.bfloat16),
            pltpu.VMEM((SUBLANES, n), jnp.float32),
            pltpu.VMEM((nq, SUBLANES, n), jnp.float32),
            pltpu.VMEM((nq, dv, n), jnp.float32),
        ],
        compiler_params=pltpu.CompilerParams(
            dimension_semantics=("arbitrary", "arbitrary", "arbitrary"),
            vmem_limit_bytes=VMEM_LIMIT),
        name=name,
    )(*extra_in, q, k, vt, sg)


def _attn_a(lam_vecs, gsub, qa, ka, vta, sg, lambda_init):
    const = lambda shape: pl.BlockSpec(shape, lambda b_, h, c: (0, 0))
    return _attn_call(functools.partial(_attn_a_kernel, lambda_init=lambda_init), "attn_a",
                      (lam_vecs, gsub), [const((4, A_QK_DIM)), const((1, A_V_DIM))],
                      qa, ka, vta, sg, Q_TILE_A, 2, 0)


def _attn_b(qb, kb, vtb, sg):
    return _attn_call(_attn_b_kernel, "attn_b", (), [], qb, kb, vtb, sg, Q_TILE_B, 1, A_HEADS)


def _out_proj_kernel(ya_ref, yb_ref, x_ref, mod_ref, wout_ref, gpost_ref, o_ref):
    out = (_bdot(ya_ref[0], wout_ref[0:A_WIDTH, :])
           + _bdot(yb_ref[0], wout_ref[A_WIDTH:A_WIDTH + B_WIDTH, :]))
    gate = mod_ref[0][:, 2 * D_MODEL:3 * D_MODEL]
    o_ref[0] = x_ref[0] + gate * _rms(out, gpost_ref[...])


def _out_proj(ya, yb, x, mod, wout, gpost):
    b, s, _ = x.shape
    tr = OUT_ROW_TILE
    return pl.pallas_call(
        _out_proj_kernel,
        out_shape=jax.ShapeDtypeStruct(x.shape, x.dtype),
        grid=(b, s // tr),
        in_specs=[
            pl.BlockSpec((1, tr, A_WIDTH), lambda i, j: (i, j, 0)),
            pl.BlockSpec((1, tr, B_WIDTH), lambda i, j: (i, j, 0)),
            pl.BlockSpec((1, tr, D_MODEL), lambda i, j: (i, j, 0)),
            pl.BlockSpec((1, 1, 3 * D_MODEL), lambda i, j: (i, 0, 0)),
            pl.BlockSpec((A_WIDTH + B_WIDTH, D_MODEL), lambda i, j: (0, 0)),
            pl.BlockSpec((1, D_MODEL), lambda i, j: (0, 0)),
        ],
        out_specs=pl.BlockSpec((1, tr, D_MODEL), lambda i, j: (i, j, 0)),
        compiler_params=pltpu.CompilerParams(
            dimension_semantics=("arbitrary", "arbitrary"), vmem_limit_bytes=VMEM_LIMIT),
        name="out_proj",
    )(ya, yb, x, mod, wout, gpost)


def _rope_tables(seq, dim):
    half = dim // 2
    inv_freq = jnp.float32(ROPE_THETA) ** (-jnp.arange(0, dim, 2, dtype=jnp.float32) / dim)
    d = np.arange(LANES) % 64
    in_span, lo = d < dim, d < half
    freq = jnp.where(in_span, inv_freq[d % half], 0.0)
    ang = jnp.arange(seq, dtype=jnp.float32)[:, None] * freq[None, :]
    cos, sin = jnp.cos(ang), jnp.sin(ang)
    return cos, jnp.where(lo, -sin, 0.0), jnp.where(in_span & ~lo, sin, 0.0)


def _layout_w_in(w_in):
    aq, ak, av, ag, cq, ckv, kr, bg = jnp.split(
        w_in, [512, 1024, 1536, 2048, 2432, 2688, 2752], axis=1)
    kr = jnp.pad(kr, ((0, 0), (0, LANES - B_ROPE_DIM)))
    return jnp.concatenate([aq, ak, av, ag, cq, ckv, bg, kr], axis=1).astype(jnp.bfloat16)


def _layout_w_uq(w_uq):
    w = w_uq.reshape(Q_LORA, B_HEADS, B_NOPE_DIM + B_ROPE_DIM)
    nope = w[:, :, :B_NOPE_DIM].reshape(Q_LORA, B_HEADS * B_NOPE_DIM)
    rope = jnp.pad(w[:, :, B_NOPE_DIM:], ((0, 0), (0, 0), (0, LANES - B_ROPE_DIM)))
    return jnp.concatenate([nope, rope.reshape(Q_LORA, B_HEADS * LANES)], axis=1).astype(jnp.bfloat16)


def _encoder_layer(x, mod, layer, tabs, g_pre, w_in, lam_vecs, g_subln, g_cq, w_uq, g_ckv, w_ukv,
                   w_out, g_post):
    lambda_init = 0.8 - 0.6 * math.exp(-0.3 * layer)
    qa, ka, vta, qb, kb, vtb, sg = _in_proj(x, mod, g_pre, w_in, g_cq, w_uq, g_ckv, w_ukv, tabs)
    ya = _attn_a(lam_vecs, g_subln, qa, ka, vta, sg, lambda_init)
    yb = _attn_b(qb, kb, vtb, sg)
    return _out_proj(ya, yb, x, mod, w_out, g_post)


def kernel(x_prompt, x_sample, c_prompt, c_sample, w_ada, b_ada, g_pre, w_in, lambda_q1, lambda_k1,
           lambda_q2, lambda_k2, g_subln, g_cq, w_uq, g_ckv, w_ukv, w_out, g_post):
    depth = w_ada.shape[0]
    nb_p, nb_s = c_prompt.shape[0], c_sample.shape[0]
    pad_rows = -(nb_p + nb_s) % 8
    c_all = jnp.concatenate(
        [c_prompt, c_sample, jnp.zeros((pad_rows, D_MODEL), jnp.float32)], axis=0)
    max_seq = max(x_prompt.shape[1], x_sample.shape[1])
    tabs = _rope_tables(max_seq, A_ROPE_DIM) + _rope_tables(max_seq, B_ROPE_DIM)
    y_p, y_s = x_prompt, x_sample
    for layer in range(depth):
        mod = _adaln_mod(c_all, w_ada[layer], b_ada[layer])
        mod_p = mod[:nb_p].reshape(nb_p, 1, 3 * D_MODEL)
        mod_s = mod[nb_p:nb_p + nb_s].reshape(nb_s, 1, 3 * D_MODEL)
        lam_vecs = jnp.stack([lambda_q1[layer], lambda_k1[layer], lambda_q2[layer], lambda_k2[layer]])
        params = (g_pre[layer].reshape(1, -1), _layout_w_in(w_in[layer]), lam_vecs,
                  g_subln[layer].reshape(1, -1), g_cq[layer].reshape(1, -1),
                  _layout_w_uq(w_uq[layer]), g_ckv[layer].reshape(1, -1),
                  w_ukv[layer].astype(jnp.bfloat16), w_out[layer].astype(jnp.bfloat16),
                  g_post[layer].reshape(1, -1))
        y_p = _encoder_layer(y_p, mod_p, layer, tabs, *params)
        y_s = _encoder_layer(y_s, mod_s, layer, tabs, *params)
    return (y_p, y_s)
```

```python
import functools
import math

import numpy as np
import jax
import jax.numpy as jnp
from jax import lax
from jax.experimental import pallas as pl
from jax.experimental.pallas import tpu as pltpu

D_MODEL = 1024
A_HEADS = 4
A_QK_DIM = 64
A_V_DIM = 128
A_WIDTH = A_HEADS * A_V_DIM
A_ROPE_DIM = A_QK_DIM // 4
B_HEADS = 4
B_NOPE_DIM = 128
B_ROPE_DIM = 64
B_V_DIM = 128
B_WIDTH = B_HEADS * B_V_DIM
B_QK_PAD = 256
Q_LORA = 384
KV_LORA = 256
ROPE_THETA = 500000.0
NORM_EPS = 1e-6
LOG2E = 1.4426950408889634

LANES = 128
SUBLANES = 8
VMEM_LIMIT = 56 * 1024 * 1024

ROW_TILE = 512
OUT_ROW_TILE = 1024
KV_TILE = ROW_TILE
KV_UNROLL = 2
Q_TILE_A = 256
Q_TILE_B = 512
Q_CHUNKS = (8192, 4096, 2048)
ATTN_VMEM_BUDGET = 48 * 1024 * 1024
PIPE_UNROLL = 32
FINISH_UNROLL = 4
NORM_CHUNK = 1024

L_MIN = 2.0 ** -80
L_MAX = 2.0 ** 100
BOUND_MARGIN = 1.0 + 2.0 ** -5

SEG_AQ = 0
SEG_AK = 512
SEG_AV = 1024
SEG_AG = 1536
SEG_CQ = 2048
SEG_KR = 2432
SEG_CKV = 2560
SEG_BG = 2816
IN_COLS_PAD = 3328


def _silu(x):
    return x * (1.0 / (1.0 + jnp.exp(-x)))


def _rms(x, g):
    return x * lax.rsqrt(jnp.mean(x * x, axis=-1, keepdims=True) + NORM_EPS) * g


def _bdot(a, b):
    return jnp.dot(a, b, preferred_element_type=jnp.float32)


def _mod_kernel(c_ref, w_ref, b_ref, o_ref):
    c = _silu(c_ref[...]).astype(jnp.bfloat16)
    o_ref[...] = _bdot(c, w_ref[...].astype(jnp.bfloat16)) + b_ref[...]


def _adaln_mod(c_all, w_ada, b_ada):
    rows = c_all.shape[0]
    n_blk = 1024
    return pl.pallas_call(
        _mod_kernel,
        out_shape=jax.ShapeDtypeStruct((rows, 3 * D_MODEL), jnp.float32),
        grid=(3 * D_MODEL // n_blk,),
        in_specs=[
            pl.BlockSpec((rows, D_MODEL), lambda j: (0, 0)),
            pl.BlockSpec((D_MODEL, n_blk), lambda j: (0, j)),
            pl.BlockSpec((1, n_blk), lambda j: (0, j)),
        ],
        out_specs=pl.BlockSpec((rows, n_blk), lambda j: (0, j)),
        compiler_params=pltpu.CompilerParams(
            dimension_semantics=("arbitrary",), vmem_limit_bytes=VMEM_LIMIT),
        name="adaln_mod",
    )(c_all, w_ada, b_ada.reshape(1, -1))


def _rope(x, c, s_lo, s_hi, half):
    return (x * c + pltpu.roll(x, LANES - half, 1) * s_lo
            + pltpu.roll(x, half, 1) * s_hi)


def _in_proj_kernel(x_ref, mod_ref, gpre_ref, win_ref, gcq_ref, wuq_ref, gckv_ref, wukv_ref,
                    ca_ref, sa_lo_ref, sa_hi_ref, cb_ref, sb_lo_ref, sb_hi_ref,
                    qa_ref, ka_ref, va_ref, qb_ref, kb_ref, vb_ref, sg_ref,
                    *, scale_a, scale_b):
    mod = mod_ref[0]
    shift = mod[:, 0:D_MODEL]
    scale = mod[:, D_MODEL:2 * D_MODEL]
    half_a = A_ROPE_DIM // 2
    half_b = B_ROPE_DIM // 2
    bf = jnp.bfloat16

    h = (_rms(x_ref[0], gpre_ref[...]) * (1.0 + scale) + shift).astype(bf)

    def proj(lo, width):
        return _bdot(h, win_ref[:, lo:lo + width])

    ca, sa_lo, sa_hi = ca_ref[...], sa_lo_ref[...], sa_hi_ref[...]
    cb, sb_lo, sb_hi = cb_ref[...], sb_lo_ref[...], sb_hi_ref[...]

    cq_kr = proj(SEG_CQ, SEG_CKV - SEG_CQ)
    cq = _rms(cq_kr[:, 0:Q_LORA], gcq_ref[...]).astype(bf)
    kr = _rope(cq_kr[:, SEG_KR - SEG_CQ:], cb, sb_lo, sb_hi, half_b).astype(bf)
    ckv = _rms(proj(SEG_CKV, KV_LORA), gckv_ref[...]).astype(bf)

    aq = proj(SEG_AQ, 512)
    ak = proj(SEG_AK, 512)
    av = proj(SEG_AV, 512)
    for hd in range(A_HEADS):
        sl = slice(hd * LANES, (hd + 1) * LANES)
        qt = (_rope(aq[:, sl], ca, sa_lo, sa_hi, half_a) * scale_a).T.astype(bf)
        for blk in range(ROW_TILE // Q_TILE_A):
            qa_ref[0, hd, blk] = qt[:, blk * Q_TILE_A:(blk + 1) * Q_TILE_A]
        ka_ref[0, hd] = _rope(ak[:, sl], ca, sa_lo, sa_hi, half_a).astype(bf)
        va_ref[0, hd, 0] = av[:, sl].T.astype(bf)

    q = _bdot(cq, wuq_ref[...])
    kv = _bdot(ckv, wukv_ref[...])
    for hd in range(B_HEADS):
        qn = q[:, hd * LANES:(hd + 1) * LANES]
        qr = q[:, (B_HEADS + hd) * LANES:(B_HEADS + hd + 1) * LANES]
        qb_ref[0, hd, 0, 0:LANES, :] = (qn * scale_b).T.astype(bf)
        qb_ref[0, hd, 0, LANES:2 * LANES, :] = (
            _rope(qr, cb, sb_lo, sb_hi, half_b) * scale_b).T.astype(bf)
        kb_ref[0, hd, :, 0:LANES] = kv[:, 2 * hd * LANES:(2 * hd + 1) * LANES].astype(bf)
        kb_ref[0, hd, :, LANES:2 * LANES] = kr
        vb_ref[0, hd, 0] = kv[:, (2 * hd + 1) * LANES:(2 * hd + 2) * LANES].T.astype(bf)

    sg_ref[0, :, 0:A_WIDTH] = _silu(proj(SEG_AG, A_WIDTH))
    sg_ref[0, :, A_WIDTH:A_WIDTH + B_WIDTH] = _silu(proj(SEG_BG, B_WIDTH))


def _in_proj(x, mod, gpre, win, gcq, wuq, gckv, wukv, tabs):
    b, s, _ = x.shape
    tr = ROW_TILE
    const = lambda shape: pl.BlockSpec(shape, lambda i, j: (0,) * len(shape))
    tab = pl.BlockSpec((tr, LANES), lambda i, j: (j, 0))
    head = lambda d: pl.BlockSpec((1, A_HEADS, tr, d), lambda i, j: (i, 0, j, 0))
    head_t = lambda d: pl.BlockSpec((1, A_HEADS, 1, d, tr), lambda i, j: (i, 0, j, 0, 0))
    qa_blocks = tr // Q_TILE_A
    assert tr == Q_TILE_B and tr == qa_blocks * Q_TILE_A
    bf = jnp.bfloat16
    kern = functools.partial(_in_proj_kernel,
                             scale_a=A_QK_DIM ** -0.5 * LOG2E,
                             scale_b=(B_NOPE_DIM + B_ROPE_DIM) ** -0.5 * LOG2E)
    return pl.pallas_call(
        kern,
        out_shape=(
            jax.ShapeDtypeStruct((b, A_HEADS, s // Q_TILE_A, LANES, Q_TILE_A), bf),
            jax.ShapeDtypeStruct((b, A_HEADS, s, LANES), bf),
            jax.ShapeDtypeStruct((b, A_HEADS, s // tr, A_V_DIM, tr), bf),
            jax.ShapeDtypeStruct((b, B_HEADS, s // tr, B_QK_PAD, tr), bf),
            jax.ShapeDtypeStruct((b, B_HEADS, s, B_QK_PAD), bf),
            jax.ShapeDtypeStruct((b, B_HEADS, s // tr, B_V_DIM, tr), bf),
            jax.ShapeDtypeStruct((b, s, A_WIDTH + B_WIDTH), jnp.float32),
        ),
        grid=(b, s // tr),
        in_specs=[
            pl.BlockSpec((1, tr, D_MODEL), lambda i, j: (i, j, 0)),
            pl.BlockSpec((1, 1, 3 * D_MODEL), lambda i, j: (i, 0, 0)),
            const((1, D_MODEL)),
            const((D_MODEL, IN_COLS_PAD)),
            const((1, Q_LORA)),
            const((Q_LORA, 2 * B_HEADS * LANES)),
            const((1, KV_LORA)),
            const((KV_LORA, B_HEADS * (B_NOPE_DIM + B_V_DIM))),
            tab, tab, tab, tab, tab, tab,
        ],
        out_specs=(pl.BlockSpec((1, A_HEADS, qa_blocks, LANES, Q_TILE_A),
                                lambda i, j: (i, 0, j, 0, 0)),
                   head(LANES), head_t(A_V_DIM), head_t(B_QK_PAD), head(B_QK_PAD),
                   head_t(B_V_DIM),
                   pl.BlockSpec((1, tr, A_WIDTH + B_WIDTH), lambda i, j: (i, j, 0))),
        compiler_params=pltpu.CompilerParams(
            dimension_semantics=("arbitrary", "arbitrary"), vmem_limit_bytes=VMEM_LIMIT),
        name="in_proj",
    )(x, mod, gpre, win, gcq, wuq, gckv, wukv, *tabs)


def _key_tile(k_ref, t):
    return k_ref[0, 0, pl.ds(pl.multiple_of(t * KV_TILE, KV_TILE), KV_TILE), :]


def _key_norm_max(k_ref, lane_spans):
    s = k_ref.shape[2]
    d = k_ref.shape[3]

    def body(c, carry):
        kk = k_ref[0, 0, pl.ds(pl.multiple_of(c * NORM_CHUNK, NORM_CHUNK), NORM_CHUNK), :]
        ksq = kk.astype(jnp.float32)
        ksq = ksq * ksq
        lane = lax.broadcasted_iota(jnp.int32, ksq.shape, 1)
        out = []
        for (lo, hi), cur in zip(lane_spans, carry):
            part = ksq if (lo, hi) == (0, d) else jnp.where((lane >= lo) & (lane < hi), ksq, 0.0)
            n2 = jnp.sum(part, axis=1, keepdims=True)
            out.append(jnp.maximum(cur, jnp.max(n2, axis=0, keepdims=True)))
        return tuple(out)

    init = tuple(jnp.zeros((1, 1), jnp.float32) for _ in lane_spans)
    res = lax.fori_loop(0, s // NORM_CHUNK, body, init)
    return [jnp.broadcast_to(r, (SUBLANES, LANES)) for r in res]


def _col_bound(qt, kmax_tiles, cols_per_map):
    qsq = qt.astype(jnp.float32)
    qsq = (qsq * qsq).astype(jnp.bfloat16)
    ones = jnp.ones((2 * SUBLANES, qt.shape[0]), jnp.bfloat16)
    q2 = _bdot(ones, qsq)[0:SUBLANES]
    kmax = jnp.concatenate(
        [t for t in kmax_tiles for _ in range(cols_per_map // LANES)], axis=1)
    return jnp.sqrt(q2 * kmax) * BOUND_MARGIN


def _running_max_pass(qt, k_ref, vt_ref, m_ref, l_ref, acc_ref):
    n = qt.shape[1]
    n_kv = k_ref.shape[2] // KV_TILE
    m_ref[...] = jnp.full(m_ref.shape, -jnp.inf, jnp.float32)
    l_ref[...] = jnp.zeros(l_ref.shape, jnp.float32)
    acc_ref[...] = jnp.zeros(acc_ref.shape, jnp.float32)

    def body(t, carry):
        s = _bdot(_key_tile(k_ref, t), qt).reshape(KV_TILE // SUBLANES, SUBLANES, n)
        m_old = m_ref[...]
        m_col = jnp.max(jnp.max(s, axis=0), axis=0, keepdims=True)
        m_new = jnp.maximum(m_old, m_col)
        alpha = jnp.exp2(m_old - m_new)
        p = jnp.exp2(s - m_new)
        l_ref[...] = alpha * l_ref[...] + jnp.sum(p, axis=0)
        pv = _bdot(vt_ref[0, 0, t], p.reshape(KV_TILE, n).astype(jnp.bfloat16))
        acc_ref[...] = alpha[0:1] * acc_ref[...] + pv
        m_ref[...] = m_new
        return carry

    lax.fori_loop(0, n_kv, body, 0)


def _attention_chunk(q_tile, finish, lane_spans, tq, k_ref, vt_ref,
                     kmax_ref, bound_ref, pa_ref, pb_ref, m_ref, l_ref, acc_ref):
    nq, _, n = bound_ref.shape
    n_grp = k_ref.shape[2] // (KV_TILE * KV_UNROLL)
    total = n_grp * nq
    log_nq = nq.bit_length() - 1
    assert nq == 1 << log_nq and total % 2 == 0 and total >= 2

    @pl.when(pl.program_id(2) == 0)
    def _():
        for idx, tile in enumerate(_key_norm_max(k_ref, lane_spans)):
            kmax_ref[idx] = tile

    kmax_tiles = [kmax_ref[idx] for idx in range(len(lane_spans))]
    for i in range(nq):
        bound_ref[i] = _col_bound(q_tile(i), kmax_tiles, tq)
    l_ref[...] = jnp.zeros(l_ref.shape, jnp.float32)
    acc_ref[...] = jnp.zeros(acc_ref.shape, jnp.float32)

    def numerators(step, p_ref):
        g, i = step >> log_nq, step & (nq - 1)
        qt = q_tile(i)
        bound = bound_ref[i]
        lsum = l_ref[i]
        rows = KV_UNROLL * KV_TILE
        keys = k_ref[0, 0, pl.ds(pl.multiple_of(g * rows, rows), rows), :]
        s = _bdot(keys, qt)
        p = jnp.exp2(s.reshape(rows // SUBLANES, SUBLANES, n) - bound)
        p_ref[...] = p.reshape(rows, n).astype(jnp.bfloat16)
        l_ref[i] = lsum + jnp.sum(p, axis=0)

    def weighted_values(step, p_ref):
        g, i = step >> log_nq, step & (nq - 1)
        vt = jnp.concatenate([vt_ref[0, 0, g * KV_UNROLL + u] for u in range(KV_UNROLL)], axis=1)
        acc_ref[i] += _bdot(vt, p_ref[...])

    p_refs = (pa_ref, pb_ref)

    def pipeline_step(step, parity):
        numerators(step + 1, p_refs[1 - parity])
        weighted_values(step, p_refs[parity])

    numerators(0, pa_ref)
    n_body = (total - 1) // PIPE_UNROLL

    def body(t, carry):
        for h in range(PIPE_UNROLL):
            pipeline_step(PIPE_UNROLL * t + h, h % 2)
        return carry

    lax.fori_loop(0, n_body, body, 0)
    for step in range(n_body * PIPE_UNROLL, total - 1):
        pipeline_step(step, step % 2)
    weighted_values(total - 1, p_refs[(total - 1) % 2])

    def usable(l_part):
        l_col = jnp.sum(l_part, axis=-2)
        return (jnp.min(l_col) >= L_MIN) & (jnp.max(l_col) <= L_MAX)

    def recompute_if_unusable(i, carry):
        @pl.when(jnp.logical_not(usable(l_ref[i])))
        def _():
            _running_max_pass(q_tile(i), k_ref, vt_ref, m_ref, l_ref.at[i], acc_ref.at[i])

        return carry

    @pl.when(jnp.logical_not(usable(l_ref[...])))
    def _():
        lax.fori_loop(0, nq, recompute_if_unusable, 0)

    def finish_tiles(t, carry):
        for h in range(FINISH_UNROLL):
            i = FINISH_UNROLL * t + h
            finish(i, acc_ref[i], jnp.sum(l_ref[i], axis=0, keepdims=True))
        return carry

    assert nq % FINISH_UNROLL == 0
    lax.fori_loop(0, nq // FINISH_UNROLL, finish_tiles, 0)


def _rows(i, tq):
    return pl.ds(pl.multiple_of(i * tq, tq), tq)


def _attn_a_kernel(lam_ref, gsub_ref, q_ref, k_ref, vt_ref, sg_ref, o_ref,
                   kmax_ref, bound_ref, pa_ref, pb_ref, m_ref, l_ref, acc_ref, *, lambda_init):
    tq = Q_TILE_A

    def q_tile(i):
        qt = q_ref[0, 0, i]
        row = lax.broadcasted_iota(jnp.int32, qt.shape, 0)
        zero = jnp.zeros_like(qt)
        return jnp.concatenate(
            [jnp.where(row < A_QK_DIM, qt, zero), jnp.where(row >= A_QK_DIM, qt, zero)], axis=1)

    lam_v = lam_ref[...]
    lam = (jnp.exp(jnp.sum(lam_v[0:1] * lam_v[1:2], axis=-1, keepdims=True))
           - jnp.exp(jnp.sum(lam_v[2:3] * lam_v[3:4], axis=-1, keepdims=True))
           + lambda_init)

    def finish(i, acc, l):
        ot = acc * (1.0 / l)
        o = (ot[:, 0:tq] - lam * ot[:, tq:2 * tq]).T
        o = _rms(o, gsub_ref[...]) * (1.0 - lambda_init)
        o_ref[0, _rows(i, tq), :] = (o * sg_ref[0, _rows(i, tq), :]).astype(o_ref.dtype)

    spans = ((0, A_QK_DIM), (A_QK_DIM, 2 * A_QK_DIM))
    _attention_chunk(q_tile, finish, spans, tq, k_ref, vt_ref,
                     kmax_ref, bound_ref, pa_ref, pb_ref, m_ref, l_ref, acc_ref)


def _attn_b_kernel(q_ref, k_ref, vt_ref, sg_ref, o_ref,
                   kmax_ref, bound_ref, pa_ref, pb_ref, m_ref, l_ref, acc_ref):
    tq = Q_TILE_B

    def q_tile(i):
        return q_ref[0, 0, i]

    def finish(i, acc, l):
        o = (acc * (1.0 / l)).T
        o_ref[0, _rows(i, tq), :] = (o * sg_ref[0, _rows(i, tq), :]).astype(o_ref.dtype)

    _attention_chunk(q_tile, finish, ((0, k_ref.shape[3]),), tq, k_ref, vt_ref,
                     kmax_ref, bound_ref, pa_ref, pb_ref, m_ref, l_ref, acc_ref)


def _query_chunk(s, dq, dv, tq, n):
    group = KV_UNROLL * KV_TILE
    for qc in Q_CHUNKS:
        if s % qc:
            continue
        windows = 2 * (s * dq * 2 + s * dv * 2 + qc * dq * 2 + qc * dv * 4 + qc * dv * 2)
        scratch = (qc // tq) * (dv + 2 * SUBLANES) * n * 4 + 2 * group * n * 2
        step_temps = 2 * group * n * 4
        if windows + scratch + step_temps <= ATTN_VMEM_BUDGET:
            return qc
    raise ValueError(f"no query chunk fits VMEM for sequence length {s}")


def _attn_call(kern, name, extra_in, extra_specs, q, k, vt, sg, tq, n_maps, col0):
    b, nh, s, dq = k.shape
    dv = vt.shape[3]
    n = n_maps * tq
    qc = _query_chunk(s, dq, dv, tq, n)
    nq = qc // tq
    assert q.shape == (b, nh, s // tq, dq, tq)
    return pl.pallas_call(
        kern,
        out_shape=jax.ShapeDtypeStruct((b, s, nh * dv), jnp.bfloat16),
        grid=(b, nh, s // qc),
        in_specs=extra_specs + [
            pl.BlockSpec((1, 1, nq, dq, tq), lambda b_, h, c: (b_, h, c, 0, 0)),
            pl.BlockSpec((1, 1, s, dq), lambda b_, h, c: (b_, h, 0, 0)),
            pl.BlockSpec((1, 1, s // KV_TILE, dv, KV_TILE), lambda b_, h, c: (b_, h, 0, 0, 0)),
            pl.BlockSpec((1, qc, dv), lambda b_, h, c: (b_, c, col0 + h)),
        ],
        out_specs=pl.BlockSpec((1, qc, dv), lambda b_, h, c: (b_, c, h)),
        scratch_shapes=[
            pltpu.VMEM((n_maps, SUBLANES, LANES), jnp.float32),
            pltpu.VMEM((nq, SUBLANES, n), jnp.float32),
            pltpu.VMEM((KV_UNROLL * KV_TILE, n), jnp.bfloat16),
            pltpu.VMEM((KV_UNROLL * KV_TILE, n), jnp.bfloat16),
            pltpu.VMEM((SUBLANES, n), jnp.float32),
            pltpu.VMEM((nq, SUBLANES, n), jnp.float32),
            pltpu.VMEM((nq, dv, n), jnp.float32),
        ],
        compiler_params=pltpu.CompilerParams(
            dimension_semantics=("arbitrary", "arbitrary", "arbitrary"),
            vmem_limit_bytes=VMEM_LIMIT),
        name=name,
    )(*extra_in, q, k, vt, sg)


def _attn_a(lam_vecs, gsub, qa, ka, vta, sg, lambda_init):
    const = lambda shape: pl.BlockSpec(shape, lambda b_, h, c: (0, 0))
    return _attn_call(functools.partial(_attn_a_kernel, lambda_init=lambda_init), "attn_a",
                      (lam_vecs, gsub), [const((4, A_QK_DIM)), const((1, A_V_DIM))],
                      qa, ka, vta, sg, Q_TILE_A, 2, 0)


def _attn_b(qb, kb, vtb, sg):
    return _attn_call(_attn_b_kernel, "attn_b", (), [], qb, kb, vtb, sg, Q_TILE_B, 1, A_HEADS)


def _out_proj_kernel(ya_ref, yb_ref, x_ref, mod_ref, wout_ref, gpost_ref, o_ref):
    out = _bdot(jnp.concatenate([ya_ref[0], yb_ref[0]], axis=1), wout_ref[...])
    gate = mod_ref[0][:, 2 * D_MODEL:3 * D_MODEL]
    o_ref[0] = x_ref[0] + gate * _rms(out, gpost_ref[...])


def _out_proj(ya, yb, x, mod, wout, gpost):
    b, s, _ = x.shape
    tr = OUT_ROW_TILE
    return pl.pallas_call(
        _out_proj_kernel,
        out_shape=jax.ShapeDtypeStruct(x.shape, x.dtype),
        grid=(b, s // tr),
        in_specs=[
            pl.BlockSpec((1, tr, A_WIDTH), lambda i, j: (i, j, 0)),
            pl.BlockSpec((1, tr, B_WIDTH), lambda i, j: (i, j, 0)),
            pl.BlockSpec((1, tr, D_MODEL), lambda i, j: (i, j, 0)),
            pl.BlockSpec((1, 1, 3 * D_MODEL), lambda i, j: (i, 0, 0)),
            pl.BlockSpec((A_WIDTH + B_WIDTH, D_MODEL), lambda i, j: (0, 0)),
            pl.BlockSpec((1, D_MODEL), lambda i, j: (0, 0)),
        ],
        out_specs=pl.BlockSpec((1, tr, D_MODEL), lambda i, j: (i, j, 0)),
        compiler_params=pltpu.CompilerParams(
            dimension_semantics=("arbitrary", "arbitrary"), vmem_limit_bytes=VMEM_LIMIT),
        name="out_proj",
    )(ya, yb, x, mod, wout, gpost)


def _rope_tables(seq, dim):
    half = dim // 2
    inv_freq = jnp.float32(ROPE_THETA) ** (-jnp.arange(0, dim, 2, dtype=jnp.float32) / dim)
    d = np.arange(LANES) % 64
    in_span, lo = d < dim, d < half
    freq = jnp.where(in_span, inv_freq[d % half], 0.0)
    ang = jnp.arange(seq, dtype=jnp.float32)[:, None] * freq[None, :]
    cos, sin = jnp.cos(ang), jnp.sin(ang)
    return cos, jnp.where(lo, -sin, 0.0), jnp.where(in_span & ~lo, sin, 0.0)


def _layout_w_in(w_in):
    aq, ak, av, ag, cq, ckv, kr, bg = jnp.split(
        w_in, [512, 1024, 1536, 2048, 2432, 2688, 2752], axis=1)
    kr = jnp.pad(kr, ((0, 0), (0, LANES - B_ROPE_DIM)))
    return jnp.concatenate([aq, ak, av, ag, cq, kr, ckv, bg], axis=1).astype(jnp.bfloat16)


def _layout_w_uq(w_uq):
    w = w_uq.reshape(Q_LORA, B_HEADS, B_NOPE_DIM + B_ROPE_DIM)
    nope = w[:, :, :B_NOPE_DIM].reshape(Q_LORA, B_HEADS * B_NOPE_DIM)
    rope = jnp.pad(w[:, :, B_NOPE_DIM:], ((0, 0), (0, 0), (0, LANES - B_ROPE_DIM)))
    return jnp.concatenate([nope, rope.reshape(Q_LORA, B_HEADS * LANES)], axis=1).astype(jnp.bfloat16)


def _encoder_layer(x, mod, layer, tabs, g_pre, w_in, lam_vecs, g_subln, g_cq, w_uq, g_ckv, w_ukv,
                   w_out, g_post):
    lambda_init = 0.8 - 0.6 * math.exp(-0.3 * layer)
    qa, ka, vta, qb, kb, vtb, sg = _in_proj(x, mod, g_pre, w_in, g_cq, w_uq, g_ckv, w_ukv, tabs)
    ya = _attn_a(lam_vecs, g_subln, qa, ka, vta, sg, lambda_init)
    yb = _attn_b(qb, kb, vtb, sg)
    return _out_proj(ya, yb, x, mod, w_out, g_post)


def kernel(x_prompt, x_sample, c_prompt, c_sample, w_ada, b_ada, g_pre, w_in, lambda_q1, lambda_k1,
           lambda_q2, lambda_k2, g_subln, g_cq, w_uq, g_ckv, w_ukv, w_out, g_post):
    depth = w_ada.shape[0]
    nb_p, nb_s = c_prompt.shape[0], c_sample.shape[0]
    pad_rows = -(nb_p + nb_s) % 8
    c_all = jnp.concatenate(
        [c_prompt, c_sample, jnp.zeros((pad_rows, D_MODEL), jnp.float32)], axis=0)
    max_seq = max(x_prompt.shape[1], x_sample.shape[1])
    tabs = _rope_tables(max_seq, A_ROPE_DIM) + _rope_tables(max_seq, B_ROPE_DIM)
    y_p, y_s = x_prompt, x_sample
    for layer in range(depth):
        mod = _adaln_mod(c_all, w_ada[layer], b_ada[layer])
        mod_p = mod[:nb_p].reshape(nb_p, 1, 3 * D_MODEL)
        mod_s = mod[nb_p:nb_p + nb_s].reshape(nb_s, 1, 3 * D_MODEL)
        lam_vecs = jnp.stack([lambda_q1[layer], lambda_k1[layer], lambda_q2[layer], lambda_k2[layer]])
        params = (g_pre[layer].reshape(1, -1), _layout_w_in(w_in[layer]), lam_vecs,
                  g_subln[layer].reshape(1, -1), g_cq[layer].reshape(1, -1),
                  _layout_w_uq(w_uq[layer]), g_ckv[layer].reshape(1, -1),
                  w_ukv[layer].astype(jnp.bfloat16), w_out[layer].astype(jnp.bfloat16),
                  g_post[layer].reshape(1, -1))
        y_p = _encoder_layer(y_p, mod_p, layer, tabs, *params)
        y_s = _encoder_layer(y_s, mod_s, layer, tabs, *params)
    return (y_p, y_s)
```

```python
import functools
import math

import numpy as np
import jax
import jax.numpy as jnp
from jax import lax
from jax.experimental import pallas as pl
from jax.experimental.pallas import tpu as pltpu

D_MODEL = 1024
A_HEADS = 4
A_QK_DIM = 64
A_V_DIM = 128
A_WIDTH = A_HEADS * A_V_DIM
A_ROPE_DIM = A_QK_DIM // 4
B_HEADS = 4
B_NOPE_DIM = 128
B_ROPE_DIM = 64
B_V_DIM = 128
B_WIDTH = B_HEADS * B_V_DIM
B_QK_PAD = 256
Q_LORA = 384
KV_LORA = 256
ROPE_THETA = 500000.0
NORM_EPS = 1e-6
LOG2E = 1.4426950408889634

LANES = 128
SUBLANES = 8
VMEM_LIMIT = 56 * 1024 * 1024

ROW_TILE = 512
OUT_ROW_TILE = 1024
KV_TILE = ROW_TILE
KV_UNROLL = 2
Q_TILE_A = 256
Q_TILE_B = 512
Q_CHUNKS = (8192, 4096, 2048)
ATTN_VMEM_BUDGET = 48 * 1024 * 1024
PIPE_UNROLL = 32
FINISH_UNROLL = 4

L_MIN = 2.0 ** -80
L_MAX = 2.0 ** 100
BOUND_MARGIN = 1.0 + 2.0 ** -5

SEG_AQ = 0
SEG_AK = 512
SEG_AV = 1024
SEG_AG = 1536
SEG_CQ = 2048
SEG_KR = 2432
SEG_CKV = 2560
SEG_BG = 2816
IN_COLS_PAD = 3328


def _silu(x):
    return x * (1.0 / (1.0 + jnp.exp(-x)))


def _rms(x, g):
    return x * lax.rsqrt(jnp.mean(x * x, axis=-1, keepdims=True) + NORM_EPS) * g


def _bdot(a, b):
    return jnp.dot(a, b, preferred_element_type=jnp.float32)


def _mod_kernel(c_ref, w_ref, b_ref, o_ref):
    c = _silu(c_ref[...]).astype(jnp.bfloat16)
    o_ref[...] = _bdot(c, w_ref[...].astype(jnp.bfloat16)) + b_ref[...]


def _adaln_mod(c_all, w_ada, b_ada):
    rows = c_all.shape[0]
    n_blk = 1024
    return pl.pallas_call(
        _mod_kernel,
        out_shape=jax.ShapeDtypeStruct((rows, 3 * D_MODEL), jnp.float32),
        grid=(3 * D_MODEL // n_blk,),
        in_specs=[
            pl.BlockSpec((rows, D_MODEL), lambda j: (0, 0)),
            pl.BlockSpec((D_MODEL, n_blk), lambda j: (0, j)),
            pl.BlockSpec((1, n_blk), lambda j: (0, j)),
        ],
        out_specs=pl.BlockSpec((rows, n_blk), lambda j: (0, j)),
        compiler_params=pltpu.CompilerParams(
            dimension_semantics=("arbitrary",), vmem_limit_bytes=VMEM_LIMIT),
        name="adaln_mod",
    )(c_all, w_ada, b_ada.reshape(1, -1))


def _rope(x, c, s_lo, s_hi, half):
    return (x * c + pltpu.roll(x, LANES - half, 1) * s_lo
            + pltpu.roll(x, half, 1) * s_hi)


def _in_proj_kernel(x_ref, mod_ref, gpre_ref, win_ref, gcq_ref, wuq_ref, gckv_ref, wukv_ref,
                    ca_ref, sa_lo_ref, sa_hi_ref, cb_ref, sb_lo_ref, sb_hi_ref,
                    qa_ref, ka_ref, va_ref, qb_ref, kb_ref, vb_ref, sg_ref, kna_ref, knb_ref,
                    *, scale_a, scale_b):
    mod = mod_ref[0]
    shift = mod[:, 0:D_MODEL]
    scale = mod[:, D_MODEL:2 * D_MODEL]
    half_a = A_ROPE_DIM // 2
    half_b = B_ROPE_DIM // 2
    bf = jnp.bfloat16

    h = (_rms(x_ref[0], gpre_ref[...]) * (1.0 + scale) + shift).astype(bf)

    def proj(lo, width):
        return _bdot(h, win_ref[:, lo:lo + width])

    ca, sa_lo, sa_hi = ca_ref[...], sa_lo_ref[...], sa_hi_ref[...]
    cb, sb_lo, sb_hi = cb_ref[...], sb_lo_ref[...], sb_hi_ref[...]

    cq_kr = proj(SEG_CQ, SEG_CKV - SEG_CQ)
    cq = _rms(cq_kr[:, 0:Q_LORA], gcq_ref[...]).astype(bf)
    kr = _rope(cq_kr[:, SEG_KR - SEG_CQ:], cb, sb_lo, sb_hi, half_b).astype(bf)
    ckv = _rms(proj(SEG_CKV, KV_LORA), gckv_ref[...]).astype(bf)

    def sq_norms(k, mask=None):
        kf = k.astype(jnp.float32)
        ksq = kf * kf if mask is None else jnp.where(mask, kf * kf, 0.0)
        return jnp.sum(ksq, axis=1, keepdims=True)

    def max_tile(n2):
        return jnp.broadcast_to(jnp.max(n2, axis=0, keepdims=True), (SUBLANES, LANES))

    aq = proj(SEG_AQ, 512)
    ak = proj(SEG_AK, 512)
    av = proj(SEG_AV, 512)
    feat = lax.broadcasted_iota(jnp.int32, (LANES, Q_TILE_A), 0)
    lane = lax.broadcasted_iota(jnp.int32, (ROW_TILE, LANES), 1)
    for hd in range(A_HEADS):
        sl = slice(hd * LANES, (hd + 1) * LANES)
        qt = (_rope(aq[:, sl], ca, sa_lo, sa_hi, half_a) * scale_a).T.astype(bf)
        for blk in range(ROW_TILE // Q_TILE_A):
            q_blk = qt[:, blk * Q_TILE_A:(blk + 1) * Q_TILE_A]
            zero = jnp.zeros_like(q_blk)
            qa_ref[0, hd, blk, :, 0:Q_TILE_A] = jnp.where(feat < A_QK_DIM, q_blk, zero)
            qa_ref[0, hd, blk, :, Q_TILE_A:2 * Q_TILE_A] = jnp.where(feat >= A_QK_DIM, q_blk, zero)
        k = _rope(ak[:, sl], ca, sa_lo, sa_hi, half_a).astype(bf)
        ka_ref[0, hd] = k
        kna_ref[0, hd, 0, 0] = max_tile(sq_norms(k, lane < A_QK_DIM))
        kna_ref[0, hd, 0, 1] = max_tile(sq_norms(k, lane >= A_QK_DIM))
        va_ref[0, hd, 0] = av[:, sl].T.astype(bf)

    q = _bdot(cq, wuq_ref[...])
    kv = _bdot(ckv, wukv_ref[...])
    kr_norms = sq_norms(kr)
    for hd in range(B_HEADS):
        qn = q[:, hd * LANES:(hd + 1) * LANES]
        qr = q[:, (B_HEADS + hd) * LANES:(B_HEADS + hd + 1) * LANES]
        qb_ref[0, hd, 0, 0:LANES, :] = (qn * scale_b).T.astype(bf)
        qb_ref[0, hd, 0, LANES:2 * LANES, :] = (
            _rope(qr, cb, sb_lo, sb_hi, half_b) * scale_b).T.astype(bf)
        kn = kv[:, 2 * hd * LANES:(2 * hd + 1) * LANES].astype(bf)
        kb_ref[0, hd, :, 0:LANES] = kn
        kb_ref[0, hd, :, LANES:2 * LANES] = kr
        knb_ref[0, hd, 0, 0] = max_tile(sq_norms(kn) + kr_norms)
        vb_ref[0, hd, 0] = kv[:, (2 * hd + 1) * LANES:(2 * hd + 2) * LANES].T.astype(bf)

    sg_ref[0, :, 0:A_WIDTH] = _silu(proj(SEG_AG, A_WIDTH))
    sg_ref[0, :, A_WIDTH:A_WIDTH + B_WIDTH] = _silu(proj(SEG_BG, B_WIDTH))


def _in_proj(x, mod, gpre, win, gcq, wuq, gckv, wukv, tabs):
    b, s, _ = x.shape
    tr = ROW_TILE
    const = lambda shape: pl.BlockSpec(shape, lambda i, j: (0,) * len(shape))
    tab = pl.BlockSpec((tr, LANES), lambda i, j: (j, 0))
    head = lambda d: pl.BlockSpec((1, A_HEADS, tr, d), lambda i, j: (i, 0, j, 0))
    head_t = lambda d: pl.BlockSpec((1, A_HEADS, 1, d, tr), lambda i, j: (i, 0, j, 0, 0))
    norm_tiles = lambda m: pl.BlockSpec((1, A_HEADS, 1, m, SUBLANES, LANES),
                                        lambda i, j: (i, 0, j, 0, 0, 0))
    qa_blocks = tr // Q_TILE_A
    assert tr == Q_TILE_B and tr == qa_blocks * Q_TILE_A
    bf = jnp.bfloat16
    kern = functools.partial(_in_proj_kernel,
                             scale_a=A_QK_DIM ** -0.5 * LOG2E,
                             scale_b=(B_NOPE_DIM + B_ROPE_DIM) ** -0.5 * LOG2E)
    return pl.pallas_call(
        kern,
        out_shape=(
            jax.ShapeDtypeStruct((b, A_HEADS, s // Q_TILE_A, LANES, 2 * Q_TILE_A), bf),
            jax.ShapeDtypeStruct((b, A_HEADS, s, LANES), bf),
            jax.ShapeDtypeStruct((b, A_HEADS, s // tr, A_V_DIM, tr), bf),
            jax.ShapeDtypeStruct((b, B_HEADS, s // tr, B_QK_PAD, tr), bf),
            jax.ShapeDtypeStruct((b, B_HEADS, s, B_QK_PAD), bf),
            jax.ShapeDtypeStruct((b, B_HEADS, s // tr, B_V_DIM, tr), bf),
            jax.ShapeDtypeStruct((b, s, A_WIDTH + B_WIDTH), jnp.float32),
            jax.ShapeDtypeStruct((b, A_HEADS, s // tr, 2, SUBLANES, LANES), jnp.float32),
            jax.ShapeDtypeStruct((b, B_HEADS, s // tr, 1, SUBLANES, LANES), jnp.float32),
        ),
        grid=(b, s // tr),
        in_specs=[
            pl.BlockSpec((1, tr, D_MODEL), lambda i, j: (i, j, 0)),
            pl.BlockSpec((1, 1, 3 * D_MODEL), lambda i, j: (i, 0, 0)),
            const((1, D_MODEL)),
            const((D_MODEL, IN_COLS_PAD)),
            const((1, Q_LORA)),
            const((Q_LORA, 2 * B_HEADS * LANES)),
            const((1, KV_LORA)),
            const((KV_LORA, B_HEADS * (B_NOPE_DIM + B_V_DIM))),
            tab, tab, tab, tab, tab, tab,
        ],
        out_specs=(pl.BlockSpec((1, A_HEADS, qa_blocks, LANES, 2 * Q_TILE_A),
                                lambda i, j: (i, 0, j, 0, 0)),
                   head(LANES), head_t(A_V_DIM), head_t(B_QK_PAD), head(B_QK_PAD),
                   head_t(B_V_DIM),
                   pl.BlockSpec((1, tr, A_WIDTH + B_WIDTH), lambda i, j: (i, j, 0)),
                   norm_tiles(2), norm_tiles(1)),
        compiler_params=pltpu.CompilerParams(
            dimension_semantics=("arbitrary", "arbitrary"), vmem_limit_bytes=VMEM_LIMIT),
        name="in_proj",
    )(x, mod, gpre, win, gcq, wuq, gckv, wukv, *tabs)


def _key_tile(k_ref, t):
    return k_ref[0, 0, pl.ds(pl.multiple_of(t * KV_TILE, KV_TILE), KV_TILE), :]


def _col_bound(qt, kmax_tiles, cols_per_map):
    qsq = qt.astype(jnp.float32)
    qsq = (qsq * qsq).astype(jnp.bfloat16)
    ones = jnp.ones((2 * SUBLANES, qt.shape[0]), jnp.bfloat16)
    q2 = _bdot(ones, qsq)[0:SUBLANES]
    kmax = jnp.concatenate(
        [t for t in kmax_tiles for _ in range(cols_per_map // LANES)], axis=1)
    return jnp.sqrt(q2 * kmax) * BOUND_MARGIN


def _running_max_pass(qt, k_ref, vt_ref, m_ref, l_ref, acc_ref):
    n = qt.shape[1]
    n_kv = k_ref.shape[2] // KV_TILE
    m_ref[...] = jnp.full(m_ref.shape, -jnp.inf, jnp.float32)
    l_ref[...] = jnp.zeros(l_ref.shape, jnp.float32)
    acc_ref[...] = jnp.zeros(acc_ref.shape, jnp.float32)

    def body(t, carry):
        s = _bdot(_key_tile(k_ref, t), qt).reshape(KV_TILE // SUBLANES, SUBLANES, n)
        m_old = m_ref[...]
        m_col = jnp.max(jnp.max(s, axis=0), axis=0, keepdims=True)
        m_new = jnp.maximum(m_old, m_col)
        alpha = jnp.exp2(m_old - m_new)
        p = jnp.exp2(s - m_new)
        l_ref[...] = alpha * l_ref[...] + jnp.sum(p, axis=0)
        pv = _bdot(vt_ref[0, 0, t], p.reshape(KV_TILE, n).astype(jnp.bfloat16))
        acc_ref[...] = alpha[0:1] * acc_ref[...] + pv
        m_ref[...] = m_new
        return carry

    lax.fori_loop(0, n_kv, body, 0)


def _attention_chunk(q_tile, finish, tq, k_ref, vt_ref, kn_ref,
                     bound_ref, pa_ref, pb_ref, m_ref, l_ref, acc_ref):
    nq, _, n = bound_ref.shape
    n_grp = k_ref.shape[2] // (KV_TILE * KV_UNROLL)
    total = n_grp * nq
    log_nq = nq.bit_length() - 1
    assert nq == 1 << log_nq and total % 2 == 0 and total >= 2

    kmax_tiles = [jnp.max(kn_ref[0, 0, :, m], axis=0) for m in range(kn_ref.shape[3])]
    for i in range(nq):
        bound_ref[i] = _col_bound(q_tile(i), kmax_tiles, tq)
    l_ref[...] = jnp.zeros(l_ref.shape, jnp.float32)
    acc_ref[...] = jnp.zeros(acc_ref.shape, jnp.float32)

    def numerators(step, p_ref):
        g, i = step >> log_nq, step & (nq - 1)
        qt = q_tile(i)
        bound = bound_ref[i]
        lsum = l_ref[i]
        rows = KV_UNROLL * KV_TILE
        keys = k_ref[0, 0, pl.ds(pl.multiple_of(g * rows, rows), rows), :]
        s = _bdot(keys, qt)
        p = jnp.exp2(s.reshape(rows // SUBLANES, SUBLANES, n) - bound)
        p_ref[...] = p.reshape(rows, n).astype(jnp.bfloat16)
        l_ref[i] = lsum + jnp.sum(p, axis=0)

    def weighted_values(step, p_ref):
        g, i = step >> log_nq, step & (nq - 1)
        vt = jnp.concatenate([vt_ref[0, 0, g * KV_UNROLL + u] for u in range(KV_UNROLL)], axis=1)
        acc_ref[i] += _bdot(vt, p_ref[...])

    p_refs = (pa_ref, pb_ref)

    def pipeline_step(step, parity):
        numerators(step + 1, p_refs[1 - parity])
        weighted_values(step, p_refs[parity])

    numerators(0, pa_ref)
    n_body = (total - 1) // PIPE_UNROLL

    def body(t, carry):
        for h in range(PIPE_UNROLL):
            pipeline_step(PIPE_UNROLL * t + h, h % 2)
        return carry

    lax.fori_loop(0, n_body, body, 0)
    for step in range(n_body * PIPE_UNROLL, total - 1):
        pipeline_step(step, step % 2)
    weighted_values(total - 1, p_refs[(total - 1) % 2])

    def usable(l_part):
        l_col = jnp.sum(l_part, axis=-2)
        return (jnp.min(l_col) >= L_MIN) & (jnp.max(l_col) <= L_MAX)

    def recompute_if_unusable(i, carry):
        @pl.when(jnp.logical_not(usable(l_ref[i])))
        def _():
            _running_max_pass(q_tile(i), k_ref, vt_ref, m_ref, l_ref.at[i], acc_ref.at[i])

        return carry

    @pl.when(jnp.logical_not(usable(l_ref[...])))
    def _():
        lax.fori_loop(0, nq, recompute_if_unusable, 0)

    def finish_tiles(t, carry):
        for h in range(FINISH_UNROLL):
            i = FINISH_UNROLL * t + h
            finish(i, acc_ref[i], jnp.sum(l_ref[i], axis=0, keepdims=True))
        return carry

    assert nq % FINISH_UNROLL == 0
    lax.fori_loop(0, nq // FINISH_UNROLL, finish_tiles, 0)


def _rows(i, tq):
    return pl.ds(pl.multiple_of(i * tq, tq), tq)


def _attn_a_kernel(lam_ref, gsub_ref, q_ref, k_ref, vt_ref, kn_ref, sg_ref, o_ref,
                   bound_ref, pa_ref, pb_ref, m_ref, l_ref, acc_ref, *, lambda_init):
    tq = Q_TILE_A

    def q_tile(i):
        return q_ref[0, 0, i]

    lam_v = lam_ref[...]
    lam = (jnp.exp(jnp.sum(lam_v[0:1] * lam_v[1:2], axis=-1, keepdims=True))
           - jnp.exp(jnp.sum(lam_v[2:3] * lam_v[3:4], axis=-1, keepdims=True))
           + lambda_init)

    def finish(i, acc, l):
        ot = acc * (1.0 / l)
        o = (ot[:, 0:tq] - lam * ot[:, tq:2 * tq]).T
        o = _rms(o, gsub_ref[...]) * (1.0 - lambda_init)
        o_ref[0, _rows(i, tq), :] = (o * sg_ref[0, _rows(i, tq), :]).astype(o_ref.dtype)

    _attention_chunk(q_tile, finish, tq, k_ref, vt_ref, kn_ref,
                     bound_ref, pa_ref, pb_ref, m_ref, l_ref, acc_ref)


def _attn_b_kernel(q_ref, k_ref, vt_ref, kn_ref, sg_ref, o_ref,
                   bound_ref, pa_ref, pb_ref, m_ref, l_ref, acc_ref):
    tq = Q_TILE_B

    def q_tile(i):
        return q_ref[0, 0, i]

    def finish(i, acc, l):
        o = (acc * (1.0 / l)).T
        o_ref[0, _rows(i, tq), :] = (o * sg_ref[0, _rows(i, tq), :]).astype(o_ref.dtype)

    _attention_chunk(q_tile, finish, tq, k_ref, vt_ref, kn_ref,
                     bound_ref, pa_ref, pb_ref, m_ref, l_ref, acc_ref)


def _query_chunk(s, dq, dv, tq, n):
    group = KV_UNROLL * KV_TILE
    for qc in Q_CHUNKS:
        if s % qc:
            continue
        windows = 2 * (s * dq * 2 + s * dv * 2 + (qc // tq) * dq * n * 2 + qc * dv * 4 + qc * dv * 2)
        scratch = (qc // tq) * (dv + 2 * SUBLANES) * n * 4 + 2 * group * n * 2
        step_temps = 2 * group * n * 4
        if windows + scratch + step_temps <= ATTN_VMEM_BUDGET:
            return qc
    raise ValueError(f"no query chunk fits VMEM for sequence length {s}")


def _attn_call(kern, name, extra_in, extra_specs, q, k, vt, kn, sg, tq, col0):
    b, nh, s, dq = k.shape
    dv = vt.shape[3]
    n_maps = kn.shape[3]
    n = q.shape[4]
    qc = _query_chunk(s, dq, dv, tq, n)
    nq = qc // tq
    assert q.shape == (b, nh, s // tq, dq, n) and n % (n_maps * LANES) == 0
    assert kn.shape == (b, nh, s // KV_TILE, n_maps, SUBLANES, LANES)
    return pl.pallas_call(
        kern,
        out_shape=jax.ShapeDtypeStruct((b, s, nh * dv), jnp.bfloat16),
        grid=(b, nh, s // qc),
        in_specs=extra_specs + [
            pl.BlockSpec((1, 1, nq, dq, n), lambda b_, h, c: (b_, h, c, 0, 0)),
            pl.BlockSpec((1, 1, s, dq), lambda b_, h, c: (b_, h, 0, 0)),
            pl.BlockSpec((1, 1, s // KV_TILE, dv, KV_TILE), lambda b_, h, c: (b_, h, 0, 0, 0)),
            pl.BlockSpec((1, 1, s // KV_TILE, n_maps, SUBLANES, LANES),
                         lambda b_, h, c: (b_, h, 0, 0, 0, 0)),
            pl.BlockSpec((1, qc, dv), lambda b_, h, c: (b_, c, col0 + h)),
        ],
        out_specs=pl.BlockSpec((1, qc, dv), lambda b_, h, c: (b_, c, h)),
        scratch_shapes=[
            pltpu.VMEM((nq, SUBLANES, n), jnp.float32),
            pltpu.VMEM((KV_UNROLL * KV_TILE, n), jnp.bfloat16),
            pltpu.VMEM((KV_UNROLL * KV_TILE, n), jnp.bfloat16),
            pltpu.VMEM((SUBLANES, n), jnp.float32),
            pltpu.VMEM((nq, SUBLANES, n), jnp.float32),
            pltpu.VMEM((nq, dv, n), jnp.float32),
        ],
        compiler_params=pltpu.CompilerParams(
            dimension_semantics=("arbitrary", "arbitrary", "arbitrary"),
            vmem_limit_bytes=VMEM_LIMIT),
        name=name,
    )(*extra_in, q, k, vt, kn, sg)


def _attn_a(lam_vecs, gsub, qa, ka, vta, kna, sg, lambda_init):
    const = lambda shape: pl.BlockSpec(shape, lambda b_, h, c: (0, 0))
    return _attn_call(functools.partial(_attn_a_kernel, lambda_init=lambda_init), "attn_a",
                      (lam_vecs, gsub), [const((4, A_QK_DIM)), const((1, A_V_DIM))],
                      qa, ka, vta, kna, sg, Q_TILE_A, 0)


def _attn_b(qb, kb, vtb, knb, sg):
    return _attn_call(_attn_b_kernel, "attn_b", (), [], qb, kb, vtb, knb, sg, Q_TILE_B, A_HEADS)


def _out_proj_kernel(ya_ref, yb_ref, x_ref, mod_ref, wout_ref, gpost_ref, o_ref):
    out = _bdot(jnp.concatenate([ya_ref[0], yb_ref[0]], axis=1), wout_ref[...])
    gate = mod_ref[0][:, 2 * D_MODEL:3 * D_MODEL]
    o_ref[0] = x_ref[0] + gate * _rms(out, gpost_ref[...])


def _out_proj(ya, yb, x, mod, wout, gpost):
    b, s, _ = x.shape
    tr = OUT_ROW_TILE
    return pl.pallas_call(
        _out_proj_kernel,
        out_shape=jax.ShapeDtypeStruct(x.shape, x.dtype),
        grid=(b, s // tr),
        in_specs=[
            pl.BlockSpec((1, tr, A_WIDTH), lambda i, j: (i, j, 0)),
            pl.BlockSpec((1, tr, B_WIDTH), lambda i, j: (i, j, 0)),
            pl.BlockSpec((1, tr, D_MODEL), lambda i, j: (i, j, 0)),
            pl.BlockSpec((1, 1, 3 * D_MODEL), lambda i, j: (i, 0, 0)),
            pl.BlockSpec((A_WIDTH + B_WIDTH, D_MODEL), lambda i, j: (0, 0)),
            pl.BlockSpec((1, D_MODEL), lambda i, j: (0, 0)),
        ],
        out_specs=pl.BlockSpec((1, tr, D_MODEL), lambda i, j: (i, j, 0)),
        compiler_params=pltpu.CompilerParams(
            dimension_semantics=("arbitrary", "arbitrary"), vmem_limit_bytes=VMEM_LIMIT),
        name="out_proj",
    )(ya, yb, x, mod, wout, gpost)


def _rope_tables(seq, dim):
    half = dim // 2
    inv_freq = jnp.float32(ROPE_THETA) ** (-jnp.arange(0, dim, 2, dtype=jnp.float32) / dim)
    d = np.arange(LANES) % 64
    in_span, lo = d < dim, d < half
    freq = jnp.where(in_span, inv_freq[d % half], 0.0)
    ang = jnp.arange(seq, dtype=jnp.float32)[:, None] * freq[None, :]
    cos, sin = jnp.cos(ang), jnp.sin(ang)
    return cos, jnp.where(lo, -sin, 0.0), jnp.where(in_span & ~lo, sin, 0.0)


def _layout_w_in(w_in):
    aq, ak, av, ag, cq, ckv, kr, bg = jnp.split(
        w_in, [512, 1024, 1536, 2048, 2432, 2688, 2752], axis=1)
    kr = jnp.pad(kr, ((0, 0), (0, LANES - B_ROPE_DIM)))
    return jnp.concatenate([aq, ak, av, ag, cq, kr, ckv, bg], axis=1).astype(jnp.bfloat16)


def _layout_w_uq(w_uq):
    w = w_uq.reshape(Q_LORA, B_HEADS, B_NOPE_DIM + B_ROPE_DIM)
    nope = w[:, :, :B_NOPE_DIM].reshape(Q_LORA, B_HEADS * B_NOPE_DIM)
    rope = jnp.pad(w[:, :, B_NOPE_DIM:], ((0, 0), (0, 0), (0, LANES - B_ROPE_DIM)))
    return jnp.concatenate([nope, rope.reshape(Q_LORA, B_HEADS * LANES)], axis=1).astype(jnp.bfloat16)


def _encoder_layer(x, mod, layer, tabs, g_pre, w_in, lam_vecs, g_subln, g_cq, w_uq, g_ckv, w_ukv,
                   w_out, g_post):
    lambda_init = 0.8 - 0.6 * math.exp(-0.3 * layer)
    qa, ka, vta, qb, kb, vtb, sg, kna, knb = _in_proj(
        x, mod, g_pre, w_in, g_cq, w_uq, g_ckv, w_ukv, tabs)
    ya = _attn_a(lam_vecs, g_subln, qa, ka, vta, kna, sg, lambda_init)
    yb = _attn_b(qb, kb, vtb, knb, sg)
    return _out_proj(ya, yb, x, mod, w_out, g_post)


def kernel(x_prompt, x_sample, c_prompt, c_sample, w_ada, b_ada, g_pre, w_in, lambda_q1, lambda_k1,
           lambda_q2, lambda_k2, g_subln, g_cq, w_uq, g_ckv, w_ukv, w_out, g_post):
    depth = w_ada.shape[0]
    nb_p, nb_s = c_prompt.shape[0], c_sample.shape[0]
    pad_rows = -(nb_p + nb_s) % 8
    c_all = jnp.concatenate(
        [c_prompt, c_sample, jnp.zeros((pad_rows, D_MODEL), jnp.float32)], axis=0)
    max_seq = max(x_prompt.shape[1], x_sample.shape[1])
    tabs = _rope_tables(max_seq, A_ROPE_DIM) + _rope_tables(max_seq, B_ROPE_DIM)
    y_p, y_s = x_prompt, x_sample
    for layer in range(depth):
        mod = _adaln_mod(c_all, w_ada[layer], b_ada[layer])
        mod_p = mod[:nb_p].reshape(nb_p, 1, 3 * D_MODEL)
        mod_s = mod[nb_p:nb_p + nb_s].reshape(nb_s, 1, 3 * D_MODEL)
        lam_vecs = jnp.stack([lambda_q1[layer], lambda_k1[layer], lambda_q2[layer], lambda_k2[layer]])
        params = (g_pre[layer].reshape(1, -1), _layout_w_in(w_in[layer]), lam_vecs,
                  g_subln[layer].reshape(1, -1), g_cq[layer].reshape(1, -1),
                  _layout_w_uq(w_uq[layer]), g_ckv[layer].reshape(1, -1),
                  w_ukv[layer].astype(jnp.bfloat16), w_out[layer].astype(jnp.bfloat16),
                  g_post[layer].reshape(1, -1))
        y_p = _encoder_layer(y_p, mod_p, layer, tabs, *params)
        y_s = _encoder_layer(y_s, mod_s, layer, tabs, *params)
    return (y_p, y_s)
```

```python
import functools
import math

import numpy as np
import jax
import jax.numpy as jnp
from jax import lax
from jax.experimental import pallas as pl
from jax.experimental.pallas import tpu as pltpu

D_MODEL = 1024
A_HEADS = 4
A_QK_DIM = 64
A_V_DIM = 128
A_WIDTH = A_HEADS * A_V_DIM
A_ROPE_DIM = A_QK_DIM // 4
B_HEADS = 4
B_NOPE_DIM = 128
B_ROPE_DIM = 64
B_V_DIM = 128
B_WIDTH = B_HEADS * B_V_DIM
B_QK_PAD = 256
Q_LORA = 384
KV_LORA = 256
ROPE_THETA = 500000.0
NORM_EPS = 1e-6
LOG2E = 1.4426950408889634

LANES = 128
SUBLANES = 8
VMEM_LIMIT = 56 * 1024 * 1024

ROW_TILE = 512
OUT_ROW_TILE = 1024
KV_TILE = ROW_TILE
KV_UNROLL = 2
Q_TILE_A = 256
Q_TILE_B = 512
Q_CHUNKS = (8192, 4096, 2048)
ATTN_VMEM_BUDGET = 48 * 1024 * 1024
PIPE_UNROLL = 32
FINISH_UNROLL = 4

L_MIN = 2.0 ** -80
L_MAX = 2.0 ** 100
BOUND_MARGIN = 1.0 + 2.0 ** -5

SEG_AQ = 0
SEG_AK = 512
SEG_AV = 1024
SEG_AG = 1536
SEG_CQ = 2048
SEG_KR = 2432
SEG_CKV = 2560
SEG_BG = 2816
IN_COLS_PAD = 3328


def _silu(x):
    return x * (1.0 / (1.0 + jnp.exp(-x)))


def _rms(x, g):
    return x * lax.rsqrt(jnp.mean(x * x, axis=-1, keepdims=True) + NORM_EPS) * g


def _bdot(a, b):
    return jnp.dot(a, b, preferred_element_type=jnp.float32)


def _mod_kernel(c_ref, w_ref, b_ref, o_ref):
    c = _silu(c_ref[...]).astype(jnp.bfloat16)
    o_ref[...] = _bdot(c, w_ref[...].astype(jnp.bfloat16)) + b_ref[...]


def _adaln_mod(c_all, w_ada, b_ada):
    rows = c_all.shape[0]
    n_blk = 1024
    return pl.pallas_call(
        _mod_kernel,
        out_shape=jax.ShapeDtypeStruct((rows, 3 * D_MODEL), jnp.float32),
        grid=(3 * D_MODEL // n_blk,),
        in_specs=[
            pl.BlockSpec((rows, D_MODEL), lambda j: (0, 0)),
            pl.BlockSpec((D_MODEL, n_blk), lambda j: (0, j)),
            pl.BlockSpec((1, n_blk), lambda j: (0, j)),
        ],
        out_specs=pl.BlockSpec((rows, n_blk), lambda j: (0, j)),
        compiler_params=pltpu.CompilerParams(
            dimension_semantics=("arbitrary",), vmem_limit_bytes=VMEM_LIMIT),
        name="adaln_mod",
    )(c_all, w_ada, b_ada.reshape(1, -1))


def _rope(x, c, s_lo, s_hi, half):
    return (x * c + pltpu.roll(x, LANES - half, 1) * s_lo
            + pltpu.roll(x, half, 1) * s_hi)


def _in_proj_kernel(x_ref, mod_ref, gpre_ref, win_ref, gcq_ref, wuq_ref, gckv_ref, wukv_ref,
                    ca_ref, sa_lo_ref, sa_hi_ref, cb_ref, sb_lo_ref, sb_hi_ref,
                    qa_ref, ka_ref, va_ref, qb_ref, kb_ref, vb_ref, sg_ref, kna_ref, knb_ref,
                    *, scale_a, scale_b):
    mod = mod_ref[0]
    shift = mod[:, 0:D_MODEL]
    scale = mod[:, D_MODEL:2 * D_MODEL]
    half_a = A_ROPE_DIM // 2
    half_b = B_ROPE_DIM // 2
    bf = jnp.bfloat16

    h = (_rms(x_ref[0], gpre_ref[...]) * (1.0 + scale) + shift).astype(bf)

    def proj(lo, width):
        return _bdot(h, win_ref[:, lo:lo + width])

    ca, sa_lo, sa_hi = ca_ref[...], sa_lo_ref[...], sa_hi_ref[...]
    cb, sb_lo, sb_hi = cb_ref[...], sb_lo_ref[...], sb_hi_ref[...]

    cq_kr = proj(SEG_CQ, SEG_CKV - SEG_CQ)
    cq = _rms(cq_kr[:, 0:Q_LORA], gcq_ref[...]).astype(bf)
    kr = _rope(cq_kr[:, SEG_KR - SEG_CQ:], cb, sb_lo, sb_hi, half_b).astype(bf)
    ckv = _rms(proj(SEG_CKV, KV_LORA), gckv_ref[...]).astype(bf)

    def max_sq_norm(k, sel):
        n2 = _bdot(k * k, sel)
        part = jnp.max(n2.reshape(n2.shape[0] // SUBLANES, SUBLANES, LANES), axis=0)
        return jnp.broadcast_to(jnp.max(part, axis=0, keepdims=True), (SUBLANES, LANES))

    aq = proj(SEG_AQ, 512)
    ak = proj(SEG_AK, 512)
    av = proj(SEG_AV, 512)
    feat = lax.broadcasted_iota(jnp.int32, (LANES, Q_TILE_A), 0)
    sel_r = lax.broadcasted_iota(jnp.int32, (LANES, LANES), 0)
    sel_c = lax.broadcasted_iota(jnp.int32, (LANES, LANES), 1)
    same_map = (((sel_r < A_QK_DIM) & (sel_c < A_QK_DIM))
                | ((sel_r >= A_QK_DIM) & (sel_c >= A_QK_DIM)))
    sel_maps = jnp.where(same_map, 1.0, 0.0).astype(bf)
    lane = lax.broadcasted_iota(jnp.int32, (SUBLANES, LANES), 1)
    for hd in range(A_HEADS):
        sl = slice(hd * LANES, (hd + 1) * LANES)
        qt = (_rope(aq[:, sl], ca, sa_lo, sa_hi, half_a) * scale_a).T.astype(bf)
        for blk in range(ROW_TILE // Q_TILE_A):
            q_blk = qt[:, blk * Q_TILE_A:(blk + 1) * Q_TILE_A]
            zero = jnp.zeros_like(q_blk)
            qa_ref[0, hd, blk, :, 0:Q_TILE_A] = jnp.where(feat < A_QK_DIM, q_blk, zero)
            qa_ref[0, hd, blk, :, Q_TILE_A:2 * Q_TILE_A] = jnp.where(feat >= A_QK_DIM, q_blk, zero)
        k = _rope(ak[:, sl], ca, sa_lo, sa_hi, half_a).astype(bf)
        ka_ref[0, hd] = k
        both = max_sq_norm(k, sel_maps)
        swapped = pltpu.roll(both, A_QK_DIM, 1)
        kna_ref[0, hd, 0, 0] = jnp.where(lane < A_QK_DIM, both, swapped)
        kna_ref[0, hd, 0, 1] = jnp.where(lane < A_QK_DIM, swapped, both)
        va_ref[0, hd, 0] = av[:, sl].T.astype(bf)

    q = _bdot(cq, wuq_ref[...])
    kv = _bdot(ckv, wukv_ref[...])
    sel_all = jnp.ones((B_QK_PAD, LANES), bf)
    for hd in range(B_HEADS):
        qn = q[:, hd * LANES:(hd + 1) * LANES]
        qr = q[:, (B_HEADS + hd) * LANES:(B_HEADS + hd + 1) * LANES]
        qb_ref[0, hd, 0, 0:LANES, :] = (qn * scale_b).T.astype(bf)
        qb_ref[0, hd, 0, LANES:2 * LANES, :] = (
            _rope(qr, cb, sb_lo, sb_hi, half_b) * scale_b).T.astype(bf)
        kn = kv[:, 2 * hd * LANES:(2 * hd + 1) * LANES].astype(bf)
        kb_ref[0, hd, :, 0:LANES] = kn
        kb_ref[0, hd, :, LANES:2 * LANES] = kr
        knb_ref[0, hd, 0, 0] = max_sq_norm(jnp.concatenate([kn, kr], axis=1), sel_all)
        vb_ref[0, hd, 0] = kv[:, (2 * hd + 1) * LANES:(2 * hd + 2) * LANES].T.astype(bf)

    sg_ref[0, :, 0:A_WIDTH] = _silu(proj(SEG_AG, A_WIDTH))
    sg_ref[0, :, A_WIDTH:A_WIDTH + B_WIDTH] = _silu(proj(SEG_BG, B_WIDTH))


def _in_proj(x, mod, gpre, win, gcq, wuq, gckv, wukv, tabs):
    b, s, _ = x.shape
    tr = ROW_TILE
    const = lambda shape: pl.BlockSpec(shape, lambda i, j: (0,) * len(shape))
    tab = pl.BlockSpec((tr, LANES), lambda i, j: (j, 0))
    head = lambda d: pl.BlockSpec((1, A_HEADS, tr, d), lambda i, j: (i, 0, j, 0))
    head_t = lambda d: pl.BlockSpec((1, A_HEADS, 1, d, tr), lambda i, j: (i, 0, j, 0, 0))
    norm_tiles = lambda m: pl.BlockSpec((1, A_HEADS, 1, m, SUBLANES, LANES),
                                        lambda i, j: (i, 0, j, 0, 0, 0))
    qa_blocks = tr // Q_TILE_A
    assert tr == Q_TILE_B and tr == qa_blocks * Q_TILE_A
    bf = jnp.bfloat16
    kern = functools.partial(_in_proj_kernel,
                             scale_a=A_QK_DIM ** -0.5 * LOG2E,
                             scale_b=(B_NOPE_DIM + B_ROPE_DIM) ** -0.5 * LOG2E)
    return pl.pallas_call(
        kern,
        out_shape=(
            jax.ShapeDtypeStruct((b, A_HEADS, s // Q_TILE_A, LANES, 2 * Q_TILE_A), bf),
            jax.ShapeDtypeStruct((b, A_HEADS, s, LANES), bf),
            jax.ShapeDtypeStruct((b, A_HEADS, s // tr, A_V_DIM, tr), bf),
            jax.ShapeDtypeStruct((b, B_HEADS, s // tr, B_QK_PAD, tr), bf),
            jax.ShapeDtypeStruct((b, B_HEADS, s, B_QK_PAD), bf),
            jax.ShapeDtypeStruct((b, B_HEADS, s // tr, B_V_DIM, tr), bf),
            jax.ShapeDtypeStruct((b, s, A_WIDTH + B_WIDTH), jnp.float32),
            jax.ShapeDtypeStruct((b, A_HEADS, s // tr, 2, SUBLANES, LANES), jnp.float32),
            jax.ShapeDtypeStruct((b, B_HEADS, s // tr, 1, SUBLANES, LANES), jnp.float32),
        ),
        grid=(b, s // tr),
        in_specs=[
            pl.BlockSpec((1, tr, D_MODEL), lambda i, j: (i, j, 0)),
            pl.BlockSpec((1, 1, 3 * D_MODEL), lambda i, j: (i, 0, 0)),
            const((1, D_MODEL)),
            const((D_MODEL, IN_COLS_PAD)),
            const((1, Q_LORA)),
            const((Q_LORA, 2 * B_HEADS * LANES)),
            const((1, KV_LORA)),
            const((KV_LORA, B_HEADS * (B_NOPE_DIM + B_V_DIM))),
            tab, tab, tab, tab, tab, tab,
        ],
        out_specs=(pl.BlockSpec((1, A_HEADS, qa_blocks, LANES, 2 * Q_TILE_A),
                                lambda i, j: (i, 0, j, 0, 0)),
                   head(LANES), head_t(A_V_DIM), head_t(B_QK_PAD), head(B_QK_PAD),
                   head_t(B_V_DIM),
                   pl.BlockSpec((1, tr, A_WIDTH + B_WIDTH), lambda i, j: (i, j, 0)),
                   norm_tiles(2), norm_tiles(1)),
        compiler_params=pltpu.CompilerParams(
            dimension_semantics=("arbitrary", "arbitrary"), vmem_limit_bytes=VMEM_LIMIT),
        name="in_proj",
    )(x, mod, gpre, win, gcq, wuq, gckv, wukv, *tabs)


def _key_tile(k_ref, t):
    return k_ref[0, 0, pl.ds(pl.multiple_of(t * KV_TILE, KV_TILE), KV_TILE), :]


def _col_bound(qt, kmax_tiles, cols_per_map):
    qsq = qt.astype(jnp.float32)
    qsq = (qsq * qsq).astype(jnp.bfloat16)
    ones = jnp.ones((2 * SUBLANES, qt.shape[0]), jnp.bfloat16)
    q2 = _bdot(ones, qsq)[0:SUBLANES]
    kmax = jnp.concatenate(
        [t for t in kmax_tiles for _ in range(cols_per_map // LANES)], axis=1)
    return jnp.sqrt(q2 * kmax) * BOUND_MARGIN


def _running_max_pass(qt, k_ref, vt_ref, m_ref, l_ref, acc_ref):
    n = qt.shape[1]
    n_kv = k_ref.shape[2] // KV_TILE
    m_ref[...] = jnp.full(m_ref.shape, -jnp.inf, jnp.float32)
    l_ref[...] = jnp.zeros(l_ref.shape, jnp.float32)
    acc_ref[...] = jnp.zeros(acc_ref.shape, jnp.float32)

    def body(t, carry):
        s = _bdot(_key_tile(k_ref, t), qt).reshape(KV_TILE // SUBLANES, SUBLANES, n)
        m_old = m_ref[...]
        m_col = jnp.max(jnp.max(s, axis=0), axis=0, keepdims=True)
        m_new = jnp.maximum(m_old, m_col)
        alpha = jnp.exp2(m_old - m_new)
        p = jnp.exp2(s - m_new)
        l_ref[...] = alpha * l_ref[...] + jnp.sum(p, axis=0)
        pv = _bdot(vt_ref[0, 0, t], p.reshape(KV_TILE, n).astype(jnp.bfloat16))
        acc_ref[...] = alpha[0:1] * acc_ref[...] + pv
        m_ref[...] = m_new
        return carry

    lax.fori_loop(0, n_kv, body, 0)


def _attention_chunk(q_tile, finish, tq, k_ref, vt_ref, kn_ref,
                     bound_ref, pa_ref, pb_ref, m_ref, l_ref, acc_ref):
    nq, _, n = bound_ref.shape
    n_grp = k_ref.shape[2] // (KV_TILE * KV_UNROLL)
    total = n_grp * nq
    log_nq = nq.bit_length() - 1
    assert nq == 1 << log_nq and total % 2 == 0 and total >= 2

    kmax_tiles = [jnp.max(kn_ref[0, 0, :, m], axis=0) for m in range(kn_ref.shape[3])]
    for i in range(nq):
        bound_ref[i] = _col_bound(q_tile(i), kmax_tiles, tq)
    l_ref[...] = jnp.zeros(l_ref.shape, jnp.float32)
    acc_ref[...] = jnp.zeros(acc_ref.shape, jnp.float32)

    def numerators(step, p_ref):
        g, i = step >> log_nq, step & (nq - 1)
        qt = q_tile(i)
        bound = bound_ref[i]
        lsum = l_ref[i]
        rows = KV_UNROLL * KV_TILE
        keys = k_ref[0, 0, pl.ds(pl.multiple_of(g * rows, rows), rows), :]
        s = _bdot(keys, qt)
        p = jnp.exp2(s.reshape(rows // SUBLANES, SUBLANES, n) - bound)
        p_ref[...] = p.reshape(rows, n).astype(jnp.bfloat16)
        l_ref[i] = lsum + jnp.sum(p, axis=0)

    def weighted_values(step, p_ref):
        g, i = step >> log_nq, step & (nq - 1)
        vt = jnp.concatenate([vt_ref[0, 0, g * KV_UNROLL + u] for u in range(KV_UNROLL)], axis=1)
        acc_ref[i] += _bdot(vt, p_ref[...])

    p_refs = (pa_ref, pb_ref)

    def pipeline_step(step, parity):
        numerators(step + 1, p_refs[1 - parity])
        weighted_values(step, p_refs[parity])

    numerators(0, pa_ref)
    n_body = (total - 1) // PIPE_UNROLL

    def body(t, carry):
        for h in range(PIPE_UNROLL):
            pipeline_step(PIPE_UNROLL * t + h, h % 2)
        return carry

    lax.fori_loop(0, n_body, body, 0)
    for step in range(n_body * PIPE_UNROLL, total - 1):
        pipeline_step(step, step % 2)
    weighted_values(total - 1, p_refs[(total - 1) % 2])

    def usable(l_part):
        l_col = jnp.sum(l_part, axis=-2)
        return (jnp.min(l_col) >= L_MIN) & (jnp.max(l_col) <= L_MAX)

    def recompute_if_unusable(i, carry):
        @pl.when(jnp.logical_not(usable(l_ref[i])))
        def _():
            _running_max_pass(q_tile(i), k_ref, vt_ref, m_ref, l_ref.at[i], acc_ref.at[i])

        return carry

    @pl.when(jnp.logical_not(usable(l_ref[...])))
    def _():
        lax.fori_loop(0, nq, recompute_if_unusable, 0)

    def finish_tiles(t, carry):
        for h in range(FINISH_UNROLL):
            i = FINISH_UNROLL * t + h
            finish(i, acc_ref[i], jnp.sum(l_ref[i], axis=0, keepdims=True))
        return carry

    assert nq % FINISH_UNROLL == 0
    lax.fori_loop(0, nq // FINISH_UNROLL, finish_tiles, 0)


def _rows(i, tq):
    return pl.ds(pl.multiple_of(i * tq, tq), tq)


def _attn_a_kernel(lam_ref, gsub_ref, q_ref, k_ref, vt_ref, kn_ref, sg_ref, o_ref,
                   bound_ref, pa_ref, pb_ref, m_ref, l_ref, acc_ref, *, lambda_init):
    tq = Q_TILE_A

    def q_tile(i):
        return q_ref[0, 0, i]

    lam_v = lam_ref[...]
    lam = (jnp.exp(jnp.sum(lam_v[0:1] * lam_v[1:2], axis=-1, keepdims=True))
           - jnp.exp(jnp.sum(lam_v[2:3] * lam_v[3:4], axis=-1, keepdims=True))
           + lambda_init)

    def finish(i, acc, l):
        ot = acc * (1.0 / l)
        o = (ot[:, 0:tq] - lam * ot[:, tq:2 * tq]).T
        o = _rms(o, gsub_ref[...]) * (1.0 - lambda_init)
        o_ref[0, _rows(i, tq), :] = (o * sg_ref[0, _rows(i, tq), :]).astype(o_ref.dtype)

    _attention_chunk(q_tile, finish, tq, k_ref, vt_ref, kn_ref,
                     bound_ref, pa_ref, pb_ref, m_ref, l_ref, acc_ref)


def _attn_b_kernel(q_ref, k_ref, vt_ref, kn_ref, sg_ref, o_ref,
                   bound_ref, pa_ref, pb_ref, m_ref, l_ref, acc_ref):
    tq = Q_TILE_B

    def q_tile(i):
        return q_ref[0, 0, i]

    def finish(i, acc, l):
        o = (acc * (1.0 / l)).T
        o_ref[0, _rows(i, tq), :] = (o * sg_ref[0, _rows(i, tq), :]).astype(o_ref.dtype)

    _attention_chunk(q_tile, finish, tq, k_ref, vt_ref, kn_ref,
                     bound_ref, pa_ref, pb_ref, m_ref, l_ref, acc_ref)


def _query_chunk(s, dq, dv, tq, n):
    group = KV_UNROLL * KV_TILE
    for qc in Q_CHUNKS:
        if s % qc:
            continue
        windows = 2 * (s * dq * 2 + s * dv * 2 + (qc // tq) * dq * n * 2 + qc * dv * 4 + qc * dv * 2)
        scratch = (qc // tq) * (dv + 2 * SUBLANES) * n * 4 + 2 * group * n * 2
        step_temps = 2 * group * n * 4
        if windows + scratch + step_temps <= ATTN_VMEM_BUDGET:
            return qc
    raise ValueError(f"no query chunk fits VMEM for sequence length {s}")


def _attn_call(kern, name, extra_in, extra_specs, q, k, vt, kn, sg, tq, col0):
    b, nh, s, dq = k.shape
    dv = vt.shape[3]
    n_maps = kn.shape[3]
    n = q.shape[4]
    qc = _query_chunk(s, dq, dv, tq, n)
    nq = qc // tq
    assert q.shape == (b, nh, s // tq, dq, n) and n % (n_maps * LANES) == 0
    assert kn.shape == (b, nh, s // KV_TILE, n_maps, SUBLANES, LANES)
    return pl.pallas_call(
        kern,
        out_shape=jax.ShapeDtypeStruct((b, s, nh * dv), jnp.bfloat16),
        grid=(b, nh, s // qc),
        in_specs=extra_specs + [
            pl.BlockSpec((1, 1, nq, dq, n), lambda b_, h, c: (b_, h, c, 0, 0)),
            pl.BlockSpec((1, 1, s, dq), lambda b_, h, c: (b_, h, 0, 0)),
            pl.BlockSpec((1, 1, s // KV_TILE, dv, KV_TILE), lambda b_, h, c: (b_, h, 0, 0, 0)),
            pl.BlockSpec((1, 1, s // KV_TILE, n_maps, SUBLANES, LANES),
                         lambda b_, h, c: (b_, h, 0, 0, 0, 0)),
            pl.BlockSpec((1, qc, dv), lambda b_, h, c: (b_, c, col0 + h)),
        ],
        out_specs=pl.BlockSpec((1, qc, dv), lambda b_, h, c: (b_, c, h)),
        scratch_shapes=[
            pltpu.VMEM((nq, SUBLANES, n), jnp.float32),
            pltpu.VMEM((KV_UNROLL * KV_TILE, n), jnp.bfloat16),
            pltpu.VMEM((KV_UNROLL * KV_TILE, n), jnp.bfloat16),
            pltpu.VMEM((SUBLANES, n), jnp.float32),
            pltpu.VMEM((nq, SUBLANES, n), jnp.float32),
            pltpu.VMEM((nq, dv, n), jnp.float32),
        ],
        compiler_params=pltpu.CompilerParams(
            dimension_semantics=("arbitrary", "arbitrary", "arbitrary"),
            vmem_limit_bytes=VMEM_LIMIT),
        name=name,
    )(*extra_in, q, k, vt, kn, sg)


def _attn_a(lam_vecs, gsub, qa, ka, vta, kna, sg, lambda_init):
    const = lambda shape: pl.BlockSpec(shape, lambda b_, h, c: (0, 0))
    return _attn_call(functools.partial(_attn_a_kernel, lambda_init=lambda_init), "attn_a",
                      (lam_vecs, gsub), [const((4, A_QK_DIM)), const((1, A_V_DIM))],
                      qa, ka, vta, kna, sg, Q_TILE_A, 0)


def _attn_b(qb, kb, vtb, knb, sg):
    return _attn_call(_attn_b_kernel, "attn_b", (), [], qb, kb, vtb, knb, sg, Q_TILE_B, A_HEADS)


def _out_proj_kernel(ya_ref, yb_ref, x_ref, mod_ref, wout_ref, gpost_ref, o_ref):
    out = _bdot(jnp.concatenate([ya_ref[0], yb_ref[0]], axis=1), wout_ref[...])
    gate = mod_ref[0][:, 2 * D_MODEL:3 * D_MODEL]
    o_ref[0] = x_ref[0] + gate * _rms(out, gpost_ref[...])


def _out_proj(ya, yb, x, mod, wout, gpost):
    b, s, _ = x.shape
    tr = OUT_ROW_TILE
    return pl.pallas_call(
        _out_proj_kernel,
        out_shape=jax.ShapeDtypeStruct(x.shape, x.dtype),
        grid=(b, s // tr),
        in_specs=[
            pl.BlockSpec((1, tr, A_WIDTH), lambda i, j: (i, j, 0)),
            pl.BlockSpec((1, tr, B_WIDTH), lambda i, j: (i, j, 0)),
            pl.BlockSpec((1, tr, D_MODEL), lambda i, j: (i, j, 0)),
            pl.BlockSpec((1, 1, 3 * D_MODEL), lambda i, j: (i, 0, 0)),
            pl.BlockSpec((A_WIDTH + B_WIDTH, D_MODEL), lambda i, j: (0, 0)),
            pl.BlockSpec((1, D_MODEL), lambda i, j: (0, 0)),
        ],
        out_specs=pl.BlockSpec((1, tr, D_MODEL), lambda i, j: (i, j, 0)),
        compiler_params=pltpu.CompilerParams(
            dimension_semantics=("arbitrary", "arbitrary"), vmem_limit_bytes=VMEM_LIMIT),
        name="out_proj",
    )(ya, yb, x, mod, wout, gpost)


def _rope_tables(seq, dim):
    half = dim // 2
    inv_freq = jnp.float32(ROPE_THETA) ** (-jnp.arange(0, dim, 2, dtype=jnp.float32) / dim)
    d = np.arange(LANES) % 64
    in_span, lo = d < dim, d < half
    freq = jnp.where(in_span, inv_freq[d % half], 0.0)
    ang = jnp.arange(seq, dtype=jnp.float32)[:, None] * freq[None, :]
    cos, sin = jnp.cos(ang), jnp.sin(ang)
    return cos, jnp.where(lo, -sin, 0.0), jnp.where(in_span & ~lo, sin, 0.0)


def _layout_w_in(w_in):
    aq, ak, av, ag, cq, ckv, kr, bg = jnp.split(
        w_in, [512, 1024, 1536, 2048, 2432, 2688, 2752], axis=1)
    kr = jnp.pad(kr, ((0, 0), (0, LANES - B_ROPE_DIM)))
    return jnp.concatenate([aq, ak, av, ag, cq, kr, ckv, bg], axis=1).astype(jnp.bfloat16)


def _layout_w_uq(w_uq):
    w = w_uq.reshape(Q_LORA, B_HEADS, B_NOPE_DIM + B_ROPE_DIM)
    nope = w[:, :, :B_NOPE_DIM].reshape(Q_LORA, B_HEADS * B_NOPE_DIM)
    rope = jnp.pad(w[:, :, B_NOPE_DIM:], ((0, 0), (0, 0), (0, LANES - B_ROPE_DIM)))
    return jnp.concatenate([nope, rope.reshape(Q_LORA, B_HEADS * LANES)], axis=1).astype(jnp.bfloat16)


def _encoder_layer(x, mod, layer, tabs, g_pre, w_in, lam_vecs, g_subln, g_cq, w_uq, g_ckv, w_ukv,
                   w_out, g_post):
    lambda_init = 0.8 - 0.6 * math.exp(-0.3 * layer)
    qa, ka, vta, qb, kb, vtb, sg, kna, knb = _in_proj(
        x, mod, g_pre, w_in, g_cq, w_uq, g_ckv, w_ukv, tabs)
    ya = _attn_a(lam_vecs, g_subln, qa, ka, vta, kna, sg, lambda_init)
    yb = _attn_b(qb, kb, vtb, knb, sg)
    return _out_proj(ya, yb, x, mod, w_out, g_post)


def kernel(x_prompt, x_sample, c_prompt, c_sample, w_ada, b_ada, g_pre, w_in, lambda_q1, lambda_k1,
           lambda_q2, lambda_k2, g_subln, g_cq, w_uq, g_ckv, w_ukv, w_out, g_post):
    depth = w_ada.shape[0]
    nb_p, nb_s = c_prompt.shape[0], c_sample.shape[0]
    pad_rows = -(nb_p + nb_s) % 8
    c_all = jnp.concatenate(
        [c_prompt, c_sample, jnp.zeros((pad_rows, D_MODEL), jnp.float32)], axis=0)
    max_seq = max(x_prompt.shape[1], x_sample.shape[1])
    tabs = _rope_tables(max_seq, A_ROPE_DIM) + _rope_tables(max_seq, B_ROPE_DIM)
    y_p, y_s = x_prompt, x_sample
    for layer in range(depth):
        mod = _adaln_mod(c_all, w_ada[layer], b_ada[layer])
        mod_p = mod[:nb_p].reshape(nb_p, 1, 3 * D_MODEL)
        mod_s = mod[nb_p:nb_p + nb_s].reshape(nb_s, 1, 3 * D_MODEL)
        lam_vecs = jnp.stack([lambda_q1[layer], lambda_k1[layer], lambda_q2[layer], lambda_k2[layer]])
        params = (g_pre[layer].reshape(1, -1), _layout_w_in(w_in[layer]), lam_vecs,
                  g_subln[layer].reshape(1, -1), g_cq[layer].reshape(1, -1),
                  _layout_w_uq(w_uq[layer]), g_ckv[layer].reshape(1, -1),
                  w_ukv[layer].astype(jnp.bfloat16), w_out[layer].astype(jnp.bfloat16),
                  g_post[layer].reshape(1, -1))
        y_p = _encoder_layer(y_p, mod_p, layer, tabs, *params)
        y_s = _encoder_layer(y_s, mod_s, layer, tabs, *params)
    return (y_p, y_s)
```

```python
import functools
import math

import numpy as np
import jax
import jax.numpy as jnp
from jax import lax
from jax.experimental import pallas as pl
from jax.experimental.pallas import tpu as pltpu

D_MODEL = 1024
A_HEADS = 4
A_QK_DIM = 64
A_V_DIM = 128
A_WIDTH = A_HEADS * A_V_DIM
A_QK_WIDTH = A_HEADS * 2 * A_QK_DIM
A_ROPE_DIM = A_QK_DIM // 4
B_HEADS = 4
B_NOPE_DIM = 128
B_ROPE_DIM = 64
B_V_DIM = 128
B_WIDTH = B_HEADS * B_V_DIM
B_QK_PAD = 256
Q_LORA = 384
KV_LORA = 256
ROPE_THETA = 500000.0
NORM_EPS = 1e-6
LOG2E = 1.4426950408889634

LANES = 128
SUBLANES = 8
VMEM_LIMIT = 56 * 1024 * 1024

ROW_TILE = 512
OUT_ROW_TILE = 1024
KV_TILE = ROW_TILE
KV_UNROLL = 2
Q_TILE_A = 256
Q_TILE_B = 512
Q_CHUNKS = (8192, 4096, 2048)
ATTN_VMEM_BUDGET = 48 * 1024 * 1024
PIPE_UNROLL = 32
FINISH_UNROLL = 4

L_MIN = 2.0 ** -80
L_MAX = 2.0 ** 100
BOUND_MARGIN = 1.0 + 2.0 ** -5

SEG_AQ = 0
SEG_AK = 512
SEG_AV = 1024
SEG_AG = 1536
SEG_CQ = 2048
SEG_KR = 2432
SEG_CKV = 2560
SEG_BG = 2816
IN_COLS_PAD = 3328


def _silu(x):
    return x * (1.0 / (1.0 + jnp.exp(-x)))


def _rms(x, g):
    return x * lax.rsqrt(jnp.mean(x * x, axis=-1, keepdims=True) + NORM_EPS) * g


def _bdot(a, b):
    return jnp.dot(a, b, preferred_element_type=jnp.float32)


def _mod_kernel(c_ref, w_ref, b_ref, o_ref):
    c = _silu(c_ref[...]).astype(jnp.bfloat16)
    o_ref[...] = _bdot(c, w_ref[...].astype(jnp.bfloat16)) + b_ref[...]


def _adaln_mod(c_all, w_ada, b_ada):
    rows = c_all.shape[0]
    n_blk = 1024
    return pl.pallas_call(
        _mod_kernel,
        out_shape=jax.ShapeDtypeStruct((rows, 3 * D_MODEL), jnp.float32),
        grid=(3 * D_MODEL // n_blk,),
        in_specs=[
            pl.BlockSpec((rows, D_MODEL), lambda j: (0, 0)),
            pl.BlockSpec((D_MODEL, n_blk), lambda j: (0, j)),
            pl.BlockSpec((1, n_blk), lambda j: (0, j)),
        ],
        out_specs=pl.BlockSpec((rows, n_blk), lambda j: (0, j)),
        compiler_params=pltpu.CompilerParams(
            dimension_semantics=("arbitrary",), vmem_limit_bytes=VMEM_LIMIT),
        name="adaln_mod",
    )(c_all, w_ada, b_ada.reshape(1, -1))


def _rope(x, c, s_lo, s_hi, half):
    return (x * c + pltpu.roll(x, LANES - half, 1) * s_lo
            + pltpu.roll(x, half, 1) * s_hi)


def _in_proj_kernel(x_ref, mod_ref, gpre_ref, win_ref, gcq_ref, wuq_ref, gckv_ref, wukv_ref,
                    ca_ref, sa_lo_ref, sa_hi_ref, cb_ref, sb_lo_ref, sb_hi_ref,
                    qa_ref, ka_ref, va_ref, qb_ref, kb_ref, vb_ref, sg_ref, kna_ref, knb_ref,
                    *, scale_a, scale_b):
    mod = mod_ref[0]
    shift = mod[:, 0:D_MODEL]
    scale = mod[:, D_MODEL:2 * D_MODEL]
    half_a = A_ROPE_DIM // 2
    half_b = B_ROPE_DIM // 2
    bf = jnp.bfloat16

    h = (_rms(x_ref[0], gpre_ref[...]) * (1.0 + scale) + shift).astype(bf)

    def proj(lo, width):
        return _bdot(h, win_ref[:, lo:lo + width])

    ca, sa_lo, sa_hi = ca_ref[...], sa_lo_ref[...], sa_hi_ref[...]
    cb, sb_lo, sb_hi = cb_ref[...], sb_lo_ref[...], sb_hi_ref[...]

    cq_kr = proj(SEG_CQ, SEG_CKV - SEG_CQ)
    cq = _rms(cq_kr[:, 0:Q_LORA], gcq_ref[...]).astype(bf)
    kr = _rope(cq_kr[:, SEG_KR - SEG_CQ:], cb, sb_lo, sb_hi, half_b).astype(bf)
    ckv = _rms(proj(SEG_CKV, KV_LORA), gckv_ref[...]).astype(bf)

    def max_sq_norm(k, sel):
        n2 = _bdot(k * k, sel)
        part = jnp.max(n2.reshape(n2.shape[0] // SUBLANES, SUBLANES, LANES), axis=0)
        return jnp.broadcast_to(jnp.max(part, axis=0, keepdims=True), (SUBLANES, LANES))

    aq = proj(SEG_AQ, A_QK_WIDTH)
    ak = proj(SEG_AK, A_QK_WIDTH)
    av = proj(SEG_AV, A_WIDTH)
    feat = lax.broadcasted_iota(jnp.int32, (LANES, Q_TILE_A), 0)
    sel_r = lax.broadcasted_iota(jnp.int32, (LANES, LANES), 0)
    sel_c = lax.broadcasted_iota(jnp.int32, (LANES, LANES), 1)
    same_map = (((sel_r < A_QK_DIM) & (sel_c < A_QK_DIM))
                | ((sel_r >= A_QK_DIM) & (sel_c >= A_QK_DIM)))
    sel_maps = jnp.where(same_map, 1.0, 0.0).astype(bf)
    lane = lax.broadcasted_iota(jnp.int32, (SUBLANES, LANES), 1)
    for hd in range(A_HEADS):
        sl = slice(hd * LANES, (hd + 1) * LANES)
        qt = (_rope(aq[:, sl], ca, sa_lo, sa_hi, half_a) * scale_a).T.astype(bf)
        for blk in range(ROW_TILE // Q_TILE_A):
            q_blk = qt[:, blk * Q_TILE_A:(blk + 1) * Q_TILE_A]
            zero = jnp.zeros_like(q_blk)
            qa_ref[0, hd, blk, :, 0:Q_TILE_A] = jnp.where(feat < A_QK_DIM, q_blk, zero)
            qa_ref[0, hd, blk, :, Q_TILE_A:2 * Q_TILE_A] = jnp.where(feat >= A_QK_DIM, q_blk, zero)
        k = _rope(ak[:, sl], ca, sa_lo, sa_hi, half_a).astype(bf)
        ka_ref[0, hd] = k
        both = max_sq_norm(k, sel_maps)
        swapped = pltpu.roll(both, A_QK_DIM, 1)
        kna_ref[0, hd, 0, 0] = jnp.where(lane < A_QK_DIM, both, swapped)
        kna_ref[0, hd, 0, 1] = jnp.where(lane < A_QK_DIM, swapped, both)
        va_ref[0, hd, 0] = av[:, sl].T.astype(bf)

    q = _bdot(cq, wuq_ref[...])
    kv = _bdot(ckv, wukv_ref[...])
    sel_all = jnp.ones((B_QK_PAD, LANES), bf)
    for hd in range(B_HEADS):
        qn = q[:, hd * LANES:(hd + 1) * LANES]
        qr = q[:, (B_HEADS + hd) * LANES:(B_HEADS + hd + 1) * LANES]
        qb_ref[0, hd, 0, 0:LANES, :] = (qn * scale_b).T.astype(bf)
        qb_ref[0, hd, 0, LANES:2 * LANES, :] = (
            _rope(qr, cb, sb_lo, sb_hi, half_b) * scale_b).T.astype(bf)
        kn = kv[:, 2 * hd * LANES:(2 * hd + 1) * LANES].astype(bf)
        kb_ref[0, hd, :, 0:LANES] = kn
        kb_ref[0, hd, :, LANES:2 * LANES] = kr
        knb_ref[0, hd, 0, 0] = max_sq_norm(jnp.concatenate([kn, kr], axis=1), sel_all)
        vb_ref[0, hd, 0] = kv[:, (2 * hd + 1) * LANES:(2 * hd + 2) * LANES].T.astype(bf)

    sg_ref[0, :, 0:A_WIDTH] = _silu(proj(SEG_AG, A_WIDTH))
    sg_ref[0, :, A_WIDTH:A_WIDTH + B_WIDTH] = _silu(proj(SEG_BG, B_WIDTH))


def _in_proj(x, mod, gpre, win, gcq, wuq, gckv, wukv, tabs):
    b, s, _ = x.shape
    tr = ROW_TILE
    const = lambda shape: pl.BlockSpec(shape, lambda i, j: (0,) * len(shape))
    tab = pl.BlockSpec((tr, LANES), lambda i, j: (j, 0))
    head = lambda d: pl.BlockSpec((1, A_HEADS, tr, d), lambda i, j: (i, 0, j, 0))
    head_t = lambda d: pl.BlockSpec((1, A_HEADS, 1, d, tr), lambda i, j: (i, 0, j, 0, 0))
    norm_tiles = lambda m: pl.BlockSpec((1, A_HEADS, 1, m, SUBLANES, LANES),
                                        lambda i, j: (i, 0, j, 0, 0, 0))
    qa_blocks = tr // Q_TILE_A
    assert tr == Q_TILE_B and tr == qa_blocks * Q_TILE_A and A_HEADS == B_HEADS
    bf = jnp.bfloat16
    kern = functools.partial(_in_proj_kernel,
                             scale_a=A_QK_DIM ** -0.5 * LOG2E,
                             scale_b=(B_NOPE_DIM + B_ROPE_DIM) ** -0.5 * LOG2E)
    return pl.pallas_call(
        kern,
        out_shape=(
            jax.ShapeDtypeStruct((b, A_HEADS, s // Q_TILE_A, LANES, 2 * Q_TILE_A), bf),
            jax.ShapeDtypeStruct((b, A_HEADS, s, LANES), bf),
            jax.ShapeDtypeStruct((b, A_HEADS, s // tr, A_V_DIM, tr), bf),
            jax.ShapeDtypeStruct((b, B_HEADS, s // tr, B_QK_PAD, tr), bf),
            jax.ShapeDtypeStruct((b, B_HEADS, s, B_QK_PAD), bf),
            jax.ShapeDtypeStruct((b, B_HEADS, s // tr, B_V_DIM, tr), bf),
            jax.ShapeDtypeStruct((b, s, A_WIDTH + B_WIDTH), jnp.float32),
            jax.ShapeDtypeStruct((b, A_HEADS, s // tr, 2, SUBLANES, LANES), jnp.float32),
            jax.ShapeDtypeStruct((b, B_HEADS, s // tr, 1, SUBLANES, LANES), jnp.float32),
        ),
        grid=(b, s // tr),
        in_specs=[
            pl.BlockSpec((1, tr, D_MODEL), lambda i, j: (i, j, 0)),
            pl.BlockSpec((1, 1, 3 * D_MODEL), lambda i, j: (i, 0, 0)),
            const((1, D_MODEL)),
            const((D_MODEL, IN_COLS_PAD)),
            const((1, Q_LORA)),
            const((Q_LORA, 2 * B_HEADS * LANES)),
            const((1, KV_LORA)),
            const((KV_LORA, B_HEADS * (B_NOPE_DIM + B_V_DIM))),
            tab, tab, tab, tab, tab, tab,
        ],
        out_specs=(pl.BlockSpec((1, A_HEADS, qa_blocks, LANES, 2 * Q_TILE_A),
                                lambda i, j: (i, 0, j, 0, 0)),
                   head(LANES), head_t(A_V_DIM), head_t(B_QK_PAD), head(B_QK_PAD),
                   head_t(B_V_DIM),
                   pl.BlockSpec((1, tr, A_WIDTH + B_WIDTH), lambda i, j: (i, j, 0)),
                   norm_tiles(2), norm_tiles(1)),
        compiler_params=pltpu.CompilerParams(
            dimension_semantics=("arbitrary", "arbitrary"), vmem_limit_bytes=VMEM_LIMIT),
        name="in_proj",
    )(x, mod, gpre, win, gcq, wuq, gckv, wukv, *tabs)


def _key_tile(k_ref, t):
    return k_ref[0, 0, pl.ds(pl.multiple_of(t * KV_TILE, KV_TILE), KV_TILE), :]


def _col_bound(qt, kmax_tiles, cols_per_map):
    qsq = qt.astype(jnp.float32)
    qsq = (qsq * qsq).astype(jnp.bfloat16)
    ones = jnp.ones((2 * SUBLANES, qt.shape[0]), jnp.bfloat16)
    q2 = _bdot(ones, qsq)[0:SUBLANES]
    kmax = jnp.concatenate(
        [t for t in kmax_tiles for _ in range(cols_per_map // LANES)], axis=1)
    return jnp.sqrt(q2 * kmax) * BOUND_MARGIN


def _running_max_pass(qt, k_ref, vt_ref, m_ref, l_ref, acc_ref):
    n = qt.shape[1]
    n_kv = k_ref.shape[2] // KV_TILE
    m_ref[...] = jnp.full(m_ref.shape, -jnp.inf, jnp.float32)
    l_ref[...] = jnp.zeros(l_ref.shape, jnp.float32)
    acc_ref[...] = jnp.zeros(acc_ref.shape, jnp.float32)

    def body(t, carry):
        s = _bdot(_key_tile(k_ref, t), qt).reshape(KV_TILE // SUBLANES, SUBLANES, n)
        m_old = m_ref[...]
        m_col = jnp.max(jnp.max(s, axis=0), axis=0, keepdims=True)
        m_new = jnp.maximum(m_old, m_col)
        alpha = jnp.exp2(m_old - m_new)
        p = jnp.exp2(s - m_new)
        l_ref[...] = alpha * l_ref[...] + jnp.sum(p, axis=0)
        pv = _bdot(vt_ref[0, 0, t], p.reshape(KV_TILE, n).astype(jnp.bfloat16))
        acc_ref[...] = alpha[0:1] * acc_ref[...] + pv
        m_ref[...] = m_new
        return carry

    lax.fori_loop(0, n_kv, body, 0)


def _attention_chunk(q_tile, finish, tq, k_ref, vt_ref, kn_ref,
                     bound_ref, pa_ref, pb_ref, m_ref, l_ref, acc_ref):
    nq, _, n = bound_ref.shape
    n_grp = k_ref.shape[2] // (KV_TILE * KV_UNROLL)
    total = n_grp * nq
    log_nq = nq.bit_length() - 1
    assert nq == 1 << log_nq and total % 2 == 0 and total >= 2

    kmax_tiles = [jnp.max(kn_ref[0, 0, :, m], axis=0) for m in range(kn_ref.shape[3])]
    for i in range(nq):
        bound_ref[i] = _col_bound(q_tile(i), kmax_tiles, tq)
    l_ref[...] = jnp.zeros(l_ref.shape, jnp.float32)
    acc_ref[...] = jnp.zeros(acc_ref.shape, jnp.float32)

    def numerators(step, p_ref):
        g, i = step >> log_nq, step & (nq - 1)
        qt = q_tile(i)
        bound = bound_ref[i]
        lsum = l_ref[i]
        rows = KV_UNROLL * KV_TILE
        keys = k_ref[0, 0, pl.ds(pl.multiple_of(g * rows, rows), rows), :]
        s = _bdot(keys, qt)
        p = jnp.exp2(s.reshape(rows // SUBLANES, SUBLANES, n) - bound)
        p_ref[...] = p.reshape(rows, n).astype(jnp.bfloat16)
        l_ref[i] = lsum + jnp.sum(p, axis=0)

    def weighted_values(step, p_ref):
        g, i = step >> log_nq, step & (nq - 1)
        vt = jnp.concatenate([vt_ref[0, 0, g * KV_UNROLL + u] for u in range(KV_UNROLL)], axis=1)
        acc_ref[i] += _bdot(vt, p_ref[...])

    p_refs = (pa_ref, pb_ref)

    def pipeline_step(step, parity):
        numerators(step + 1, p_refs[1 - parity])
        weighted_values(step, p_refs[parity])

    numerators(0, pa_ref)
    n_body = (total - 1) // PIPE_UNROLL

    def body(t, carry):
        for h in range(PIPE_UNROLL):
            pipeline_step(PIPE_UNROLL * t + h, h % 2)
        return carry

    lax.fori_loop(0, n_body, body, 0)
    for step in range(n_body * PIPE_UNROLL, total - 1):
        pipeline_step(step, step % 2)
    weighted_values(total - 1, p_refs[(total - 1) % 2])

    def usable(l_part):
        l_col = jnp.sum(l_part, axis=-2)
        return (jnp.min(l_col) >= L_MIN) & (jnp.max(l_col) <= L_MAX)

    def recompute_if_unusable(i, carry):
        @pl.when(jnp.logical_not(usable(l_ref[i])))
        def _():
            _running_max_pass(q_tile(i), k_ref, vt_ref, m_ref, l_ref.at[i], acc_ref.at[i])

        return carry

    @pl.when(jnp.logical_not(usable(l_ref[...])))
    def _():
        lax.fori_loop(0, nq, recompute_if_unusable, 0)

    def finish_tiles(t, carry):
        for h in range(FINISH_UNROLL):
            i = FINISH_UNROLL * t + h
            finish(i, acc_ref[i], jnp.sum(l_ref[i], axis=0, keepdims=True))
        return carry

    assert nq % FINISH_UNROLL == 0
    lax.fori_loop(0, nq // FINISH_UNROLL, finish_tiles, 0)


def _rows(i, tq):
    return pl.ds(pl.multiple_of(i * tq, tq), tq)


def _attn_a_kernel(lam_ref, gsub_ref, q_ref, k_ref, vt_ref, kn_ref, sg_ref, o_ref,
                   bound_ref, pa_ref, pb_ref, m_ref, l_ref, acc_ref, *, lambda_init):
    tq = Q_TILE_A

    def q_tile(i):
        return q_ref[0, 0, i]

    lam_v = lam_ref[...]
    lam = (jnp.exp(jnp.sum(lam_v[0:1] * lam_v[1:2], axis=-1, keepdims=True))
           - jnp.exp(jnp.sum(lam_v[2:3] * lam_v[3:4], axis=-1, keepdims=True))
           + lambda_init)

    def finish(i, acc, l):
        ot = acc * (1.0 / l)
        o = (ot[:, 0:tq] - lam * ot[:, tq:2 * tq]).T
        o = _rms(o, gsub_ref[...]) * (1.0 - lambda_init)
        o_ref[0, _rows(i, tq), :] = (o * sg_ref[0, _rows(i, tq), :]).astype(o_ref.dtype)

    _attention_chunk(q_tile, finish, tq, k_ref, vt_ref, kn_ref,
                     bound_ref, pa_ref, pb_ref, m_ref, l_ref, acc_ref)


def _attn_b_kernel(q_ref, k_ref, vt_ref, kn_ref, sg_ref, o_ref,
                   bound_ref, pa_ref, pb_ref, m_ref, l_ref, acc_ref):
    tq = Q_TILE_B

    def q_tile(i):
        return q_ref[0, 0, i]

    def finish(i, acc, l):
        o = (acc * (1.0 / l)).T
        o_ref[0, _rows(i, tq), :] = (o * sg_ref[0, _rows(i, tq), :]).astype(o_ref.dtype)

    _attention_chunk(q_tile, finish, tq, k_ref, vt_ref, kn_ref,
                     bound_ref, pa_ref, pb_ref, m_ref, l_ref, acc_ref)


def _query_chunk(s, dq, dv, tq, n):
    group = KV_UNROLL * KV_TILE
    for qc in Q_CHUNKS:
        if s % qc:
            continue
        windows = 2 * (s * dq * 2 + s * dv * 2 + (qc // tq) * dq * n * 2 + qc * dv * 4 + qc * dv * 2)
        scratch = (qc // tq) * (dv + 2 * SUBLANES) * n * 4 + 2 * group * n * 2
        step_temps = 2 * group * n * 4
        if windows + scratch + step_temps <= ATTN_VMEM_BUDGET:
            return qc
    raise ValueError(f"no query chunk fits VMEM for sequence length {s}")


def _attn_call(kern, name, extra_in, extra_specs, q, k, vt, kn, sg, tq, col0):
    b, nh, s, dq = k.shape
    dv = vt.shape[3]
    n_maps = kn.shape[3]
    n = q.shape[4]
    qc = _query_chunk(s, dq, dv, tq, n)
    nq = qc // tq
    assert q.shape == (b, nh, s // tq, dq, n) and n % (n_maps * LANES) == 0
    assert kn.shape == (b, nh, s // KV_TILE, n_maps, SUBLANES, LANES)
    return pl.pallas_call(
        kern,
        out_shape=jax.ShapeDtypeStruct((b, s, nh * dv), jnp.bfloat16),
        grid=(b, nh, s // qc),
        in_specs=extra_specs + [
            pl.BlockSpec((1, 1, nq, dq, n), lambda b_, h, c: (b_, h, c, 0, 0)),
            pl.BlockSpec((1, 1, s, dq), lambda b_, h, c: (b_, h, 0, 0)),
            pl.BlockSpec((1, 1, s // KV_TILE, dv, KV_TILE), lambda b_, h, c: (b_, h, 0, 0, 0)),
            pl.BlockSpec((1, 1, s // KV_TILE, n_maps, SUBLANES, LANES),
                         lambda b_, h, c: (b_, h, 0, 0, 0, 0)),
            pl.BlockSpec((1, qc, dv), lambda b_, h, c: (b_, c, col0 + h)),
        ],
        out_specs=pl.BlockSpec((1, qc, dv), lambda b_, h, c: (b_, c, h)),
        scratch_shapes=[
            pltpu.VMEM((nq, SUBLANES, n), jnp.float32),
            pltpu.VMEM((KV_UNROLL * KV_TILE, n), jnp.bfloat16),
            pltpu.VMEM((KV_UNROLL * KV_TILE, n), jnp.bfloat16),
            pltpu.VMEM((SUBLANES, n), jnp.float32),
            pltpu.VMEM((nq, SUBLANES, n), jnp.float32),
            pltpu.VMEM((nq, dv, n), jnp.float32),
        ],
        compiler_params=pltpu.CompilerParams(
            dimension_semantics=("arbitrary", "arbitrary", "arbitrary"),
            vmem_limit_bytes=VMEM_LIMIT),
        name=name,
    )(*extra_in, q, k, vt, kn, sg)


def _attn_a(lam_vecs, gsub, qa, ka, vta, kna, sg, lambda_init):
    const = lambda shape: pl.BlockSpec(shape, lambda b_, h, c: (0, 0))
    return _attn_call(functools.partial(_attn_a_kernel, lambda_init=lambda_init), "attn_a",
                      (lam_vecs, gsub), [const((4, A_QK_DIM)), const((1, A_V_DIM))],
                      qa, ka, vta, kna, sg, Q_TILE_A, 0)


def _attn_b(qb, kb, vtb, knb, sg):
    return _attn_call(_attn_b_kernel, "attn_b", (), [], qb, kb, vtb, knb, sg, Q_TILE_B, A_HEADS)


def _out_proj_kernel(ya_ref, yb_ref, x_ref, mod_ref, wout_ref, gpost_ref, o_ref):
    out = _bdot(jnp.concatenate([ya_ref[0], yb_ref[0]], axis=1), wout_ref[...])
    gate = mod_ref[0][:, 2 * D_MODEL:3 * D_MODEL]
    o_ref[0] = x_ref[0] + gate * _rms(out, gpost_ref[...])


def _out_proj(ya, yb, x, mod, wout, gpost):
    b, s, _ = x.shape
    tr = OUT_ROW_TILE
    return pl.pallas_call(
        _out_proj_kernel,
        out_shape=jax.ShapeDtypeStruct(x.shape, x.dtype),
        grid=(b, s // tr),
        in_specs=[
            pl.BlockSpec((1, tr, A_WIDTH), lambda i, j: (i, j, 0)),
            pl.BlockSpec((1, tr, B_WIDTH), lambda i, j: (i, j, 0)),
            pl.BlockSpec((1, tr, D_MODEL), lambda i, j: (i, j, 0)),
            pl.BlockSpec((1, 1, 3 * D_MODEL), lambda i, j: (i, 0, 0)),
            pl.BlockSpec((A_WIDTH + B_WIDTH, D_MODEL), lambda i, j: (0, 0)),
            pl.BlockSpec((1, D_MODEL), lambda i, j: (0, 0)),
        ],
        out_specs=pl.BlockSpec((1, tr, D_MODEL), lambda i, j: (i, j, 0)),
        compiler_params=pltpu.CompilerParams(
            dimension_semantics=("arbitrary", "arbitrary"), vmem_limit_bytes=VMEM_LIMIT),
        name="out_proj",
    )(ya, yb, x, mod, wout, gpost)


def _rope_tables(seq, dim):
    half = dim // 2
    inv_freq = jnp.float32(ROPE_THETA) ** (-jnp.arange(0, dim, 2, dtype=jnp.float32) / dim)
    d = np.arange(LANES) % 64
    in_span, lo = d < dim, d < half
    freq = jnp.where(in_span, inv_freq[d % half], 0.0)
    ang = jnp.arange(seq, dtype=jnp.float32)[:, None] * freq[None, :]
    cos, sin = jnp.cos(ang), jnp.sin(ang)
    return cos, jnp.where(lo, -sin, 0.0), jnp.where(in_span & ~lo, sin, 0.0)


def _layout_w_in(w_in):
    aq, ak, av, ag, cq, ckv, kr, bg = jnp.split(
        w_in, [512, 1024, 1536, 2048, 2432, 2688, 2752], axis=1)
    kr = jnp.pad(kr, ((0, 0), (0, LANES - B_ROPE_DIM)))
    return jnp.concatenate([aq, ak, av, ag, cq, kr, ckv, bg], axis=1).astype(jnp.bfloat16)


def _layout_w_uq(w_uq):
    w = w_uq.reshape(Q_LORA, B_HEADS, B_NOPE_DIM + B_ROPE_DIM)
    nope = w[:, :, :B_NOPE_DIM].reshape(Q_LORA, B_HEADS * B_NOPE_DIM)
    rope = jnp.pad(w[:, :, B_NOPE_DIM:], ((0, 0), (0, 0), (0, LANES - B_ROPE_DIM)))
    return jnp.concatenate([nope, rope.reshape(Q_LORA, B_HEADS * LANES)], axis=1).astype(jnp.bfloat16)


def _encoder_layer(x, mod, layer, tabs, g_pre, w_in, lam_vecs, g_subln, g_cq, w_uq, g_ckv, w_ukv,
                   w_out, g_post):
    lambda_init = 0.8 - 0.6 * math.exp(-0.3 * layer)
    qa, ka, vta, qb, kb, vtb, sg, kna, knb = _in_proj(
        x, mod, g_pre, w_in, g_cq, w_uq, g_ckv, w_ukv, tabs)
    ya = _attn_a(lam_vecs, g_subln, qa, ka, vta, kna, sg, lambda_init)
    yb = _attn_b(qb, kb, vtb, knb, sg)
    return _out_proj(ya, yb, x, mod, w_out, g_post)


def kernel(x_prompt, x_sample, c_prompt, c_sample, w_ada, b_ada, g_pre, w_in, lambda_q1, lambda_k1,
           lambda_q2, lambda_k2, g_subln, g_cq, w_uq, g_ckv, w_ukv, w_out, g_post):
    depth = w_ada.shape[0]
    nb_p, nb_s = c_prompt.shape[0], c_sample.shape[0]
    pad_rows = -(nb_p + nb_s) % 8
    c_all = jnp.concatenate(
        [c_prompt, c_sample, jnp.zeros((pad_rows, D_MODEL), jnp.float32)], axis=0)
    max_seq = max(x_prompt.shape[1], x_sample.shape[1])
    tabs = _rope_tables(max_seq, A_ROPE_DIM) + _rope_tables(max_seq, B_ROPE_DIM)
    y_p, y_s = x_prompt, x_sample
    for layer in range(depth):
        mod = _adaln_mod(c_all, w_ada[layer], b_ada[layer])
        mod_p = mod[:nb_p].reshape(nb_p, 1, 3 * D_MODEL)
        mod_s = mod[nb_p:nb_p + nb_s].reshape(nb_s, 1, 3 * D_MODEL)
        lam_vecs = jnp.stack([lambda_q1[layer], lambda_k1[layer], lambda_q2[layer], lambda_k2[layer]])
        params = (g_pre[layer].reshape(1, -1), _layout_w_in(w_in[layer]), lam_vecs,
                  g_subln[layer].reshape(1, -1), g_cq[layer].reshape(1, -1),
                  _layout_w_uq(w_uq[layer]), g_ckv[layer].reshape(1, -1),
                  w_ukv[layer].astype(jnp.bfloat16), w_out[layer].astype(jnp.bfloat16),
                  g_post[layer].reshape(1, -1))
        y_p = _encoder_layer(y_p, mod_p, layer, tabs, *params)
        y_s = _encoder_layer(y_s, mod_s, layer, tabs, *params)
    return (y_p, y_s)
```

```python
import functools
import math

import numpy as np
import jax
import jax.numpy as jnp
from jax import lax
from jax.experimental import pallas as pl
from jax.experimental.pallas import tpu as pltpu

D_MODEL = 1024
A_HEADS = 4
A_QK_DIM = 64
A_V_DIM = 128
A_WIDTH = A_HEADS * A_V_DIM
A_ROPE_DIM = A_QK_DIM // 4
B_HEADS = 4
B_NOPE_DIM = 128
B_ROPE_DIM = 64
B_V_DIM = 128
B_WIDTH = B_HEADS * B_V_DIM
B_QK_PAD = 256
Q_LORA = 384
KV_LORA = 256
ROPE_THETA = 500000.0
NORM_EPS = 1e-6
LOG2E = 1.4426950408889634

LANES = 128
SUBLANES = 8
VMEM_LIMIT = 56 * 1024 * 1024

ROW_TILE = 512
OUT_ROW_TILE = 1024
KV_TILE = ROW_TILE
KV_UNROLL = 2
Q_TILE_A = 512
Q_TILE_B = 512
Q_CHUNKS = (8192, 4096, 2048)
ATTN_VMEM_BUDGET = 48 * 1024 * 1024
PIPE_BODY_COLUMNS = 32 * 512
FINISH_UNROLL = 4

L_MIN = 2.0 ** -80
L_MAX = 2.0 ** 100
BOUND_MARGIN = 1.0 + 2.0 ** -5

SEG_AQ = 0
SEG_AK = 512
SEG_AV = 1024
SEG_AG = 1536
SEG_CQ = 2048
SEG_KR = 2432
SEG_CKV = 2560
SEG_BG = 2816
IN_COLS_PAD = 3328


def _silu(x):
    return x * (1.0 / (1.0 + jnp.exp(-x)))


def _rms(x, g):
    return x * lax.rsqrt(jnp.mean(x * x, axis=-1, keepdims=True) + NORM_EPS) * g


def _bdot(a, b):
    return jnp.dot(a, b, preferred_element_type=jnp.float32)


def _mod_kernel(c_ref, w_ref, b_ref, o_ref):
    c = _silu(c_ref[...]).astype(jnp.bfloat16)
    o_ref[...] = _bdot(c, w_ref[...].astype(jnp.bfloat16)) + b_ref[...]


def _adaln_mod(c_all, w_ada, b_ada):
    rows = c_all.shape[0]
    n_blk = 1024
    return pl.pallas_call(
        _mod_kernel,
        out_shape=jax.ShapeDtypeStruct((rows, 3 * D_MODEL), jnp.float32),
        grid=(3 * D_MODEL // n_blk,),
        in_specs=[
            pl.BlockSpec((rows, D_MODEL), lambda j: (0, 0)),
            pl.BlockSpec((D_MODEL, n_blk), lambda j: (0, j)),
            pl.BlockSpec((1, n_blk), lambda j: (0, j)),
        ],
        out_specs=pl.BlockSpec((rows, n_blk), lambda j: (0, j)),
        compiler_params=pltpu.CompilerParams(
            dimension_semantics=("arbitrary",), vmem_limit_bytes=VMEM_LIMIT),
        name="adaln_mod",
    )(c_all, w_ada, b_ada.reshape(1, -1))


def _rope(x, c, s_lo, s_hi, half):
    return (x * c + pltpu.roll(x, LANES - half, 1) * s_lo
            + pltpu.roll(x, half, 1) * s_hi)


def _in_proj_kernel(x_ref, mod_ref, gpre_ref, win_ref, gcq_ref, wuq_ref, gckv_ref, wukv_ref,
                    ca_ref, sa_lo_ref, sa_hi_ref, cb_ref, sb_lo_ref, sb_hi_ref,
                    qa_ref, ka_ref, va_ref, qb_ref, kb_ref, vb_ref, sg_ref, kna_ref, knb_ref,
                    *, scale_a, scale_b):
    mod = mod_ref[0]
    shift = mod[:, 0:D_MODEL]
    scale = mod[:, D_MODEL:2 * D_MODEL]
    half_a = A_ROPE_DIM // 2
    half_b = B_ROPE_DIM // 2
    bf = jnp.bfloat16

    h = (_rms(x_ref[0], gpre_ref[...]) * (1.0 + scale) + shift).astype(bf)

    def proj(lo, width):
        return _bdot(h, win_ref[:, lo:lo + width])

    ca, sa_lo, sa_hi = ca_ref[...], sa_lo_ref[...], sa_hi_ref[...]
    cb, sb_lo, sb_hi = cb_ref[...], sb_lo_ref[...], sb_hi_ref[...]

    cq_kr = proj(SEG_CQ, SEG_CKV - SEG_CQ)
    cq = _rms(cq_kr[:, 0:Q_LORA], gcq_ref[...]).astype(bf)
    kr = _rope(cq_kr[:, SEG_KR - SEG_CQ:], cb, sb_lo, sb_hi, half_b).astype(bf)
    ckv = _rms(proj(SEG_CKV, KV_LORA), gckv_ref[...]).astype(bf)

    def max_sq_norm(k, sel):
        n2 = _bdot(k * k, sel)
        part = jnp.max(n2.reshape(n2.shape[0] // SUBLANES, SUBLANES, LANES), axis=0)
        return jnp.broadcast_to(jnp.max(part, axis=0, keepdims=True), (SUBLANES, LANES))

    aq = proj(SEG_AQ, 512)
    ak = proj(SEG_AK, 512)
    av = proj(SEG_AV, 512)
    feat = lax.broadcasted_iota(jnp.int32, (LANES, Q_TILE_A), 0)
    sel_r = lax.broadcasted_iota(jnp.int32, (LANES, LANES), 0)
    sel_c = lax.broadcasted_iota(jnp.int32, (LANES, LANES), 1)
    same_map = (((sel_r < A_QK_DIM) & (sel_c < A_QK_DIM))
                | ((sel_r >= A_QK_DIM) & (sel_c >= A_QK_DIM)))
    sel_maps = jnp.where(same_map, 1.0, 0.0).astype(bf)
    lane = lax.broadcasted_iota(jnp.int32, (SUBLANES, LANES), 1)
    for hd in range(A_HEADS):
        sl = slice(hd * LANES, (hd + 1) * LANES)
        qt = (_rope(aq[:, sl], ca, sa_lo, sa_hi, half_a) * scale_a).T.astype(bf)
        for blk in range(ROW_TILE // Q_TILE_A):
            q_blk = qt[:, blk * Q_TILE_A:(blk + 1) * Q_TILE_A]
            zero = jnp.zeros_like(q_blk)
            qa_ref[0, hd, blk, :, 0:Q_TILE_A] = jnp.where(feat < A_QK_DIM, q_blk, zero)
            qa_ref[0, hd, blk, :, Q_TILE_A:2 * Q_TILE_A] = jnp.where(feat >= A_QK_DIM, q_blk, zero)
        k = _rope(ak[:, sl], ca, sa_lo, sa_hi, half_a).astype(bf)
        ka_ref[0, hd] = k
        both = max_sq_norm(k, sel_maps)
        swapped = pltpu.roll(both, A_QK_DIM, 1)
        kna_ref[0, hd, 0, 0] = jnp.where(lane < A_QK_DIM, both, swapped)
        kna_ref[0, hd, 0, 1] = jnp.where(lane < A_QK_DIM, swapped, both)
        va_ref[0, hd, 0] = av[:, sl].T.astype(bf)

    q = _bdot(cq, wuq_ref[...])
    kv = _bdot(ckv, wukv_ref[...])
    sel_all = jnp.ones((B_QK_PAD, LANES), bf)
    for hd in range(B_HEADS):
        qn = q[:, hd * LANES:(hd + 1) * LANES]
        qr = q[:, (B_HEADS + hd) * LANES:(B_HEADS + hd + 1) * LANES]
        qb_ref[0, hd, 0, 0:LANES, :] = (qn * scale_b).T.astype(bf)
        qb_ref[0, hd, 0, LANES:2 * LANES, :] = (
            _rope(qr, cb, sb_lo, sb_hi, half_b) * scale_b).T.astype(bf)
        kn = kv[:, 2 * hd * LANES:(2 * hd + 1) * LANES].astype(bf)
        kb_ref[0, hd, :, 0:LANES] = kn
        kb_ref[0, hd, :, LANES:2 * LANES] = kr
        knb_ref[0, hd, 0, 0] = max_sq_norm(jnp.concatenate([kn, kr], axis=1), sel_all)
        vb_ref[0, hd, 0] = kv[:, (2 * hd + 1) * LANES:(2 * hd + 2) * LANES].T.astype(bf)

    sg_ref[0, :, 0:A_WIDTH] = _silu(proj(SEG_AG, A_WIDTH))
    sg_ref[0, :, A_WIDTH:A_WIDTH + B_WIDTH] = _silu(proj(SEG_BG, B_WIDTH))


def _in_proj(x, mod, gpre, win, gcq, wuq, gckv, wukv, tabs):
    b, s, _ = x.shape
    tr = ROW_TILE
    const = lambda shape: pl.BlockSpec(shape, lambda i, j: (0,) * len(shape))
    tab = pl.BlockSpec((tr, LANES), lambda i, j: (j, 0))
    head = lambda d: pl.BlockSpec((1, A_HEADS, tr, d), lambda i, j: (i, 0, j, 0))
    head_t = lambda d: pl.BlockSpec((1, A_HEADS, 1, d, tr), lambda i, j: (i, 0, j, 0, 0))
    norm_tiles = lambda m: pl.BlockSpec((1, A_HEADS, 1, m, SUBLANES, LANES),
                                        lambda i, j: (i, 0, j, 0, 0, 0))
    qa_blocks = tr // Q_TILE_A
    assert tr == Q_TILE_B and tr == qa_blocks * Q_TILE_A
    bf = jnp.bfloat16
    kern = functools.partial(_in_proj_kernel,
                             scale_a=A_QK_DIM ** -0.5 * LOG2E,
                             scale_b=(B_NOPE_DIM + B_ROPE_DIM) ** -0.5 * LOG2E)
    return pl.pallas_call(
        kern,
        out_shape=(
            jax.ShapeDtypeStruct((b, A_HEADS, s // Q_TILE_A, LANES, 2 * Q_TILE_A), bf),
            jax.ShapeDtypeStruct((b, A_HEADS, s, LANES), bf),
            jax.ShapeDtypeStruct((b, A_HEADS, s // tr, A_V_DIM, tr), bf),
            jax.ShapeDtypeStruct((b, B_HEADS, s // tr, B_QK_PAD, tr), bf),
            jax.ShapeDtypeStruct((b, B_HEADS, s, B_QK_PAD), bf),
            jax.ShapeDtypeStruct((b, B_HEADS, s // tr, B_V_DIM, tr), bf),
            jax.ShapeDtypeStruct((b, s, A_WIDTH + B_WIDTH), jnp.float32),
            jax.ShapeDtypeStruct((b, A_HEADS, s // tr, 2, SUBLANES, LANES), jnp.float32),
            jax.ShapeDtypeStruct((b, B_HEADS, s // tr, 1, SUBLANES, LANES), jnp.float32),
        ),
        grid=(b, s // tr),
        in_specs=[
            pl.BlockSpec((1, tr, D_MODEL), lambda i, j: (i, j, 0)),
            pl.BlockSpec((1, 1, 3 * D_MODEL), lambda i, j: (i, 0, 0)),
            const((1, D_MODEL)),
            const((D_MODEL, IN_COLS_PAD)),
            const((1, Q_LORA)),
            const((Q_LORA, 2 * B_HEADS * LANES)),
            const((1, KV_LORA)),
            const((KV_LORA, B_HEADS * (B_NOPE_DIM + B_V_DIM))),
            tab, tab, tab, tab, tab, tab,
        ],
        out_specs=(pl.BlockSpec((1, A_HEADS, qa_blocks, LANES, 2 * Q_TILE_A),
                                lambda i, j: (i, 0, j, 0, 0)),
                   head(LANES), head_t(A_V_DIM), head_t(B_QK_PAD), head(B_QK_PAD),
                   head_t(B_V_DIM),
                   pl.BlockSpec((1, tr, A_WIDTH + B_WIDTH), lambda i, j: (i, j, 0)),
                   norm_tiles(2), norm_tiles(1)),
        compiler_params=pltpu.CompilerParams(
            dimension_semantics=("arbitrary", "arbitrary"), vmem_limit_bytes=VMEM_LIMIT),
        name="in_proj",
    )(x, mod, gpre, win, gcq, wuq, gckv, wukv, *tabs)


def _key_tile(k_ref, t):
    return k_ref[0, 0, pl.ds(pl.multiple_of(t * KV_TILE, KV_TILE), KV_TILE), :]


def _col_bound(qt, kmax_tiles, cols_per_map):
    qsq = qt.astype(jnp.float32)
    qsq = (qsq * qsq).astype(jnp.bfloat16)
    ones = jnp.ones((2 * SUBLANES, qt.shape[0]), jnp.bfloat16)
    q2 = _bdot(ones, qsq)[0:SUBLANES]
    kmax = jnp.concatenate(
        [t for t in kmax_tiles for _ in range(cols_per_map // LANES)], axis=1)
    return jnp.sqrt(q2 * kmax) * BOUND_MARGIN


def _running_max_pass(qt, k_ref, vt_ref, m_ref, l_ref, acc_ref):
    n = qt.shape[1]
    n_kv = k_ref.shape[2] // KV_TILE
    m_ref[...] = jnp.full(m_ref.shape, -jnp.inf, jnp.float32)
    l_ref[...] = jnp.zeros(l_ref.shape, jnp.float32)
    acc_ref[...] = jnp.zeros(acc_ref.shape, jnp.float32)

    def body(t, carry):
        s = _bdot(_key_tile(k_ref, t), qt).reshape(KV_TILE // SUBLANES, SUBLANES, n)
        m_old = m_ref[...]
        m_col = jnp.max(jnp.max(s, axis=0), axis=0, keepdims=True)
        m_new = jnp.maximum(m_old, m_col)
        alpha = jnp.exp2(m_old - m_new)
        p = jnp.exp2(s - m_new)
        l_ref[...] = alpha * l_ref[...] + jnp.sum(p, axis=0)
        pv = _bdot(vt_ref[0, 0, t], p.reshape(KV_TILE, n).astype(jnp.bfloat16))
        acc_ref[...] = alpha[0:1] * acc_ref[...] + pv
        m_ref[...] = m_new
        return carry

    lax.fori_loop(0, n_kv, body, 0)


def _attention_chunk(q_tile, finish, tq, k_ref, vt_ref, kn_ref,
                     bound_ref, pa_ref, pb_ref, m_ref, l_ref, acc_ref):
    nq, _, n = bound_ref.shape
    n_grp = k_ref.shape[2] // (KV_TILE * KV_UNROLL)
    total = n_grp * nq
    log_nq = nq.bit_length() - 1
    assert nq == 1 << log_nq and total % 2 == 0 and total >= 2

    kmax_tiles = [jnp.max(kn_ref[0, 0, :, m], axis=0) for m in range(kn_ref.shape[3])]
    for i in range(nq):
        bound_ref[i] = _col_bound(q_tile(i), kmax_tiles, tq)
    l_ref[...] = jnp.zeros(l_ref.shape, jnp.float32)
    acc_ref[...] = jnp.zeros(acc_ref.shape, jnp.float32)

    def numerators(step, p_ref):
        g, i = step >> log_nq, step & (nq - 1)
        qt = q_tile(i)
        bound = bound_ref[i]
        lsum = l_ref[i]
        rows = KV_UNROLL * KV_TILE
        keys = k_ref[0, 0, pl.ds(pl.multiple_of(g * rows, rows), rows), :]
        s = _bdot(keys, qt)
        p = jnp.exp2(s.reshape(rows // SUBLANES, SUBLANES, n) - bound)
        p_ref[...] = p.reshape(rows, n).astype(jnp.bfloat16)
        l_ref[i] = lsum + jnp.sum(p, axis=0)

    def weighted_values(step, p_ref):
        g, i = step >> log_nq, step & (nq - 1)
        vt = jnp.concatenate([vt_ref[0, 0, g * KV_UNROLL + u] for u in range(KV_UNROLL)], axis=1)
        acc_ref[i] += _bdot(vt, p_ref[...])

    p_refs = (pa_ref, pb_ref)

    def pipeline_step(step, parity):
        numerators(step + 1, p_refs[1 - parity])
        weighted_values(step, p_refs[parity])

    numerators(0, pa_ref)
    unroll = PIPE_BODY_COLUMNS // n
    assert unroll % 2 == 0
    n_body = (total - 1) // unroll

    def body(t, carry):
        for h in range(unroll):
            pipeline_step(unroll * t + h, h % 2)
        return carry

    lax.fori_loop(0, n_body, body, 0)
    for step in range(n_body * unroll, total - 1):
        pipeline_step(step, step % 2)
    weighted_values(total - 1, p_refs[(total - 1) % 2])

    def usable(l_part):
        l_col = jnp.sum(l_part, axis=-2)
        return (jnp.min(l_col) >= L_MIN) & (jnp.max(l_col) <= L_MAX)

    def recompute_if_unusable(i, carry):
        @pl.when(jnp.logical_not(usable(l_ref[i])))
        def _():
            _running_max_pass(q_tile(i), k_ref, vt_ref, m_ref, l_ref.at[i], acc_ref.at[i])

        return carry

    @pl.when(jnp.logical_not(usable(l_ref[...])))
    def _():
        lax.fori_loop(0, nq, recompute_if_unusable, 0)

    def finish_tiles(t, carry):
        for h in range(FINISH_UNROLL):
            i = FINISH_UNROLL * t + h
            finish(i, acc_ref[i], jnp.sum(l_ref[i], axis=0, keepdims=True))
        return carry

    assert nq % FINISH_UNROLL == 0
    lax.fori_loop(0, nq // FINISH_UNROLL, finish_tiles, 0)


def _rows(i, tq):
    return pl.ds(pl.multiple_of(i * tq, tq), tq)


def _attn_a_kernel(lam_ref, gsub_ref, q_ref, k_ref, vt_ref, kn_ref, sg_ref, o_ref,
                   bound_ref, pa_ref, pb_ref, m_ref, l_ref, acc_ref, *, lambda_init):
    tq = Q_TILE_A

    def q_tile(i):
        return q_ref[0, 0, i]

    lam_v = lam_ref[...]
    lam = (jnp.exp(jnp.sum(lam_v[0:1] * lam_v[1:2], axis=-1, keepdims=True))
           - jnp.exp(jnp.sum(lam_v[2:3] * lam_v[3:4], axis=-1, keepdims=True))
           + lambda_init)

    def finish(i, acc, l):
        ot = acc * (1.0 / l)
        o = (ot[:, 0:tq] - lam * ot[:, tq:2 * tq]).T
        o = _rms(o, gsub_ref[...]) * (1.0 - lambda_init)
        o_ref[0, _rows(i, tq), :] = (o * sg_ref[0, _rows(i, tq), :]).astype(o_ref.dtype)

    _attention_chunk(q_tile, finish, tq, k_ref, vt_ref, kn_ref,
                     bound_ref, pa_ref, pb_ref, m_ref, l_ref, acc_ref)


def _attn_b_kernel(q_ref, k_ref, vt_ref, kn_ref, sg_ref, o_ref,
                   bound_ref, pa_ref, pb_ref, m_ref, l_ref, acc_ref):
    tq = Q_TILE_B

    def q_tile(i):
        return q_ref[0, 0, i]

    def finish(i, acc, l):
        o = (acc * (1.0 / l)).T
        o_ref[0, _rows(i, tq), :] = (o * sg_ref[0, _rows(i, tq), :]).astype(o_ref.dtype)

    _attention_chunk(q_tile, finish, tq, k_ref, vt_ref, kn_ref,
                     bound_ref, pa_ref, pb_ref, m_ref, l_ref, acc_ref)


def _query_chunk(s, dq, dv, tq, n):
    group = KV_UNROLL * KV_TILE
    for qc in Q_CHUNKS:
        if s % qc:
            continue
        windows = 2 * (s * dq * 2 + s * dv * 2 + (qc // tq) * dq * n * 2 + qc * dv * 4 + qc * dv * 2)
        scratch = (qc // tq) * (dv + 2 * SUBLANES) * n * 4 + 2 * group * n * 2
        step_temps = 2 * group * n * 4
        if windows + scratch + step_temps <= ATTN_VMEM_BUDGET:
            return qc
    raise ValueError(f"no query chunk fits VMEM for sequence length {s}")


def _attn_call(kern, name, extra_in, extra_specs, q, k, vt, kn, sg, tq, col0):
    b, nh, s, dq = k.shape
    dv = vt.shape[3]
    n_maps = kn.shape[3]
    n = q.shape[4]
    qc = _query_chunk(s, dq, dv, tq, n)
    nq = qc // tq
    assert q.shape == (b, nh, s // tq, dq, n) and n % (n_maps * LANES) == 0
    assert kn.shape == (b, nh, s // KV_TILE, n_maps, SUBLANES, LANES)
    return pl.pallas_call(
        kern,
        out_shape=jax.ShapeDtypeStruct((b, s, nh * dv), jnp.bfloat16),
        grid=(b, nh, s // qc),
        in_specs=extra_specs + [
            pl.BlockSpec((1, 1, nq, dq, n), lambda b_, h, c: (b_, h, c, 0, 0)),
            pl.BlockSpec((1, 1, s, dq), lambda b_, h, c: (b_, h, 0, 0)),
            pl.BlockSpec((1, 1, s // KV_TILE, dv, KV_TILE), lambda b_, h, c: (b_, h, 0, 0, 0)),
            pl.BlockSpec((1, 1, s // KV_TILE, n_maps, SUBLANES, LANES),
                         lambda b_, h, c: (b_, h, 0, 0, 0, 0)),
            pl.BlockSpec((1, qc, dv), lambda b_, h, c: (b_, c, col0 + h)),
        ],
        out_specs=pl.BlockSpec((1, qc, dv), lambda b_, h, c: (b_, c, h)),
        scratch_shapes=[
            pltpu.VMEM((nq, SUBLANES, n), jnp.float32),
            pltpu.VMEM((KV_UNROLL * KV_TILE, n), jnp.bfloat16),
            pltpu.VMEM((KV_UNROLL * KV_TILE, n), jnp.bfloat16),
            pltpu.VMEM((SUBLANES, n), jnp.float32),
            pltpu.VMEM((nq, SUBLANES, n), jnp.float32),
            pltpu.VMEM((nq, dv, n), jnp.float32),
        ],
        compiler_params=pltpu.CompilerParams(
            dimension_semantics=("arbitrary", "arbitrary", "arbitrary"),
            vmem_limit_bytes=VMEM_LIMIT),
        name=name,
    )(*extra_in, q, k, vt, kn, sg)


def _attn_a(lam_vecs, gsub, qa, ka, vta, kna, sg, lambda_init):
    const = lambda shape: pl.BlockSpec(shape, lambda b_, h, c: (0, 0))
    return _attn_call(functools.partial(_attn_a_kernel, lambda_init=lambda_init), "attn_a",
                      (lam_vecs, gsub), [const((4, A_QK_DIM)), const((1, A_V_DIM))],
                      qa, ka, vta, kna, sg, Q_TILE_A, 0)


def _attn_b(qb, kb, vtb, knb, sg):
    return _attn_call(_attn_b_kernel, "attn_b", (), [], qb, kb, vtb, knb, sg, Q_TILE_B, A_HEADS)


def _out_proj_kernel(ya_ref, yb_ref, x_ref, mod_ref, wout_ref, gpost_ref, o_ref):
    out = _bdot(jnp.concatenate([ya_ref[0], yb_ref[0]], axis=1), wout_ref[...])
    gate = mod_ref[0][:, 2 * D_MODEL:3 * D_MODEL]
    o_ref[0] = x_ref[0] + gate * _rms(out, gpost_ref[...])


def _out_proj(ya, yb, x, mod, wout, gpost):
    b, s, _ = x.shape
    tr = OUT_ROW_TILE
    return pl.pallas_call(
        _out_proj_kernel,
        out_shape=jax.ShapeDtypeStruct(x.shape, x.dtype),
        grid=(b, s // tr),
        in_specs=[
            pl.BlockSpec((1, tr, A_WIDTH), lambda i, j: (i, j, 0)),
            pl.BlockSpec((1, tr, B_WIDTH), lambda i, j: (i, j, 0)),
            pl.BlockSpec((1, tr, D_MODEL), lambda i, j: (i, j, 0)),
            pl.BlockSpec((1, 1, 3 * D_MODEL), lambda i, j: (i, 0, 0)),
            pl.BlockSpec((A_WIDTH + B_WIDTH, D_MODEL), lambda i, j: (0, 0)),
            pl.BlockSpec((1, D_MODEL), lambda i, j: (0, 0)),
        ],
        out_specs=pl.BlockSpec((1, tr, D_MODEL), lambda i, j: (i, j, 0)),
        compiler_params=pltpu.CompilerParams(
            dimension_semantics=("arbitrary", "arbitrary"), vmem_limit_bytes=VMEM_LIMIT),
        name="out_proj",
    )(ya, yb, x, mod, wout, gpost)


def _rope_tables(seq, dim):
    half = dim // 2
    inv_freq = jnp.float32(ROPE_THETA) ** (-jnp.arange(0, dim, 2, dtype=jnp.float32) / dim)
    d = np.arange(LANES) % 64
    in_span, lo = d < dim, d < half
    freq = jnp.where(in_span, inv_freq[d % half], 0.0)
    ang = jnp.arange(seq, dtype=jnp.float32)[:, None] * freq[None, :]
    cos, sin = jnp.cos(ang), jnp.sin(ang)
    return cos, jnp.where(lo, -sin, 0.0), jnp.where(in_span & ~lo, sin, 0.0)


def _layout_w_in(w_in):
    aq, ak, av, ag, cq, ckv, kr, bg = jnp.split(
        w_in, [512, 1024, 1536, 2048, 2432, 2688, 2752], axis=1)
    kr = jnp.pad(kr, ((0, 0), (0, LANES - B_ROPE_DIM)))
    return jnp.concatenate([aq, ak, av, ag, cq, kr, ckv, bg], axis=1).astype(jnp.bfloat16)


def _layout_w_uq(w_uq):
    w = w_uq.reshape(Q_LORA, B_HEADS, B_NOPE_DIM + B_ROPE_DIM)
    nope = w[:, :, :B_NOPE_DIM].reshape(Q_LORA, B_HEADS * B_NOPE_DIM)
    rope = jnp.pad(w[:, :, B_NOPE_DIM:], ((0, 0), (0, 0), (0, LANES - B_ROPE_DIM)))
    return jnp.concatenate([nope, rope.reshape(Q_LORA, B_HEADS * LANES)], axis=1).astype(jnp.bfloat16)


def _encoder_layer(x, mod, layer, tabs, g_pre, w_in, lam_vecs, g_subln, g_cq, w_uq, g_ckv, w_ukv,
                   w_out, g_post):
    lambda_init = 0.8 - 0.6 * math.exp(-0.3 * layer)
    qa, ka, vta, qb, kb, vtb, sg, kna, knb = _in_proj(
        x, mod, g_pre, w_in, g_cq, w_uq, g_ckv, w_ukv, tabs)
    ya = _attn_a(lam_vecs, g_subln, qa, ka, vta, kna, sg, lambda_init)
    yb = _attn_b(qb, kb, vtb, knb, sg)
    return _out_proj(ya, yb, x, mod, w_out, g_post)


def kernel(x_prompt, x_sample, c_prompt, c_sample, w_ada, b_ada, g_pre, w_in, lambda_q1, lambda_k1,
           lambda_q2, lambda_k2, g_subln, g_cq, w_uq, g_ckv, w_ukv, w_out, g_post):
    depth = w_ada.shape[0]
    nb_p, nb_s = c_prompt.shape[0], c_sample.shape[0]
    pad_rows = -(nb_p + nb_s) % 8
    c_all = jnp.concatenate(
        [c_prompt, c_sample, jnp.zeros((pad_rows, D_MODEL), jnp.float32)], axis=0)
    max_seq = max(x_prompt.shape[1], x_sample.shape[1])
    tabs = _rope_tables(max_seq, A_ROPE_DIM) + _rope_tables(max_seq, B_ROPE_DIM)
    y_p, y_s = x_prompt, x_sample
    for layer in range(depth):
        mod = _adaln_mod(c_all, w_ada[layer], b_ada[layer])
        mod_p = mod[:nb_p].reshape(nb_p, 1, 3 * D_MODEL)
        mod_s = mod[nb_p:nb_p + nb_s].reshape(nb_s, 1, 3 * D_MODEL)
        lam_vecs = jnp.stack([lambda_q1[layer], lambda_k1[layer], lambda_q2[layer], lambda_k2[layer]])
        params = (g_pre[layer].reshape(1, -1), _layout_w_in(w_in[layer]), lam_vecs,
                  g_subln[layer].reshape(1, -1), g_cq[layer].reshape(1, -1),
                  _layout_w_uq(w_uq[layer]), g_ckv[layer].reshape(1, -1),
                  w_ukv[layer].astype(jnp.bfloat16), w_out[layer].astype(jnp.bfloat16),
                  g_post[layer].reshape(1, -1))
        y_p = _encoder_layer(y_p, mod_p, layer, tabs, *params)
        y_s = _encoder_layer(y_s, mod_s, layer, tabs, *params)
    return (y_p, y_s)
```

```python
import functools
import math

import numpy as np
import jax
import jax.numpy as jnp
from jax import lax
from jax.experimental import pallas as pl
from jax.experimental.pallas import tpu as pltpu

D_MODEL = 1024
A_HEADS = 4
A_QK_DIM = 64
A_V_DIM = 128
A_WIDTH = A_HEADS * A_V_DIM
A_ROPE_DIM = A_QK_DIM // 4
B_HEADS = 4
B_NOPE_DIM = 128
B_ROPE_DIM = 64
B_V_DIM = 128
B_WIDTH = B_HEADS * B_V_DIM
B_QK_PAD = 256
Q_LORA = 384
KV_LORA = 256
ROPE_THETA = 500000.0
NORM_EPS = 1e-6
LOG2E = 1.4426950408889634

LANES = 128
SUBLANES = 8
VMEM_LIMIT = 56 * 1024 * 1024

ROW_TILE = 512
OUT_ROW_TILE = 1024
KV_TILE = ROW_TILE
KV_UNROLL = 2
Q_TILE_A = 512
Q_TILE_B = 512
Q_CHUNKS = (8192, 4096, 2048)
ATTN_VMEM_BUDGET = 48 * 1024 * 1024
PIPE_BODY_COLUMNS = 32 * 512
FINISH_UNROLL = 4

L_MIN = 2.0 ** -80
L_MAX = 2.0 ** 100
BOUND_MARGIN = 1.0 + 2.0 ** -5

SEG_AQ = 0
SEG_AK = 512
SEG_AV = 1024
SEG_AG = 1536
SEG_CQ = 2048
SEG_KR = 2432
SEG_CKV = 2560
SEG_BG = 2816
IN_COLS_PAD = 3328


def _silu(x):
    return x * (1.0 / (1.0 + jnp.exp(-x)))


def _rms(x, g):
    return x * lax.rsqrt(jnp.mean(x * x, axis=-1, keepdims=True) + NORM_EPS) * g


def _bdot(a, b):
    return jnp.dot(a, b, preferred_element_type=jnp.float32)


def _mod_kernel(c_ref, w_ref, b_ref, o_ref):
    c = _silu(c_ref[...]).astype(jnp.bfloat16)
    o_ref[...] = _bdot(c, w_ref[...].astype(jnp.bfloat16)) + b_ref[...]


def _adaln_mod(c_all, w_ada, b_ada):
    rows = c_all.shape[0]
    n_blk = 1024
    return pl.pallas_call(
        _mod_kernel,
        out_shape=jax.ShapeDtypeStruct((rows, 3 * D_MODEL), jnp.float32),
        grid=(3 * D_MODEL // n_blk,),
        in_specs=[
            pl.BlockSpec((rows, D_MODEL), lambda j: (0, 0)),
            pl.BlockSpec((D_MODEL, n_blk), lambda j: (0, j)),
            pl.BlockSpec((1, n_blk), lambda j: (0, j)),
        ],
        out_specs=pl.BlockSpec((rows, n_blk), lambda j: (0, j)),
        compiler_params=pltpu.CompilerParams(
            dimension_semantics=("arbitrary",), vmem_limit_bytes=VMEM_LIMIT),
        name="adaln_mod",
    )(c_all, w_ada, b_ada.reshape(1, -1))


def _rope(x, c, s_lo, s_hi, half):
    return (x * c + pltpu.roll(x, LANES - half, 1) * s_lo
            + pltpu.roll(x, half, 1) * s_hi)


def _in_proj_kernel(x_ref, mod_ref, gpre_ref, win_ref, gcq_ref, wuq_ref, gckv_ref, wukv_ref,
                    ca_ref, sa_lo_ref, sa_hi_ref, cb_ref, sb_lo_ref, sb_hi_ref,
                    qa_ref, ka_ref, va_ref, qb_ref, kb_ref, vb_ref, sg_ref, kna_ref, knb_ref,
                    *, scale_a, scale_b):
    mod = mod_ref[0]
    shift = mod[:, 0:D_MODEL]
    scale = mod[:, D_MODEL:2 * D_MODEL]
    half_a = A_ROPE_DIM // 2
    half_b = B_ROPE_DIM // 2
    bf = jnp.bfloat16

    h = (_rms(x_ref[0], gpre_ref[...]) * (1.0 + scale) + shift).astype(bf)

    def proj(lo, width):
        return _bdot(h, win_ref[:, lo:lo + width])

    ca, sa_lo, sa_hi = ca_ref[...], sa_lo_ref[...], sa_hi_ref[...]
    cb, sb_lo, sb_hi = cb_ref[...], sb_lo_ref[...], sb_hi_ref[...]

    cq_kr = proj(SEG_CQ, SEG_CKV - SEG_CQ)
    cq = _rms(cq_kr[:, 0:Q_LORA], gcq_ref[...]).astype(bf)
    kr = _rope(cq_kr[:, SEG_KR - SEG_CQ:], cb, sb_lo, sb_hi, half_b).astype(bf)
    ckv = _rms(proj(SEG_CKV, KV_LORA), gckv_ref[...]).astype(bf)

    def max_sq_norm(k, sel):
        n2 = _bdot(k * k, sel)
        part = jnp.max(n2.reshape(n2.shape[0] // SUBLANES, SUBLANES, LANES), axis=0)
        return jnp.broadcast_to(jnp.max(part, axis=0, keepdims=True), (SUBLANES, LANES))

    aq = proj(SEG_AQ, 512)
    ak = proj(SEG_AK, 512)
    av = proj(SEG_AV, 512)
    feat = lax.broadcasted_iota(jnp.int32, (LANES, Q_TILE_A), 0)
    sel_r = lax.broadcasted_iota(jnp.int32, (LANES, LANES), 0)
    sel_c = lax.broadcasted_iota(jnp.int32, (LANES, LANES), 1)
    same_map = (((sel_r < A_QK_DIM) & (sel_c < A_QK_DIM))
                | ((sel_r >= A_QK_DIM) & (sel_c >= A_QK_DIM)))
    sel_maps = jnp.where(same_map, 1.0, 0.0).astype(bf)
    lane = lax.broadcasted_iota(jnp.int32, (SUBLANES, LANES), 1)
    for hd in range(A_HEADS):
        sl = slice(hd * LANES, (hd + 1) * LANES)
        qt = (_rope(aq[:, sl], ca, sa_lo, sa_hi, half_a) * scale_a).T.astype(bf)
        for blk in range(ROW_TILE // Q_TILE_A):
            q_blk = qt[:, blk * Q_TILE_A:(blk + 1) * Q_TILE_A]
            zero = jnp.zeros_like(q_blk)
            qa_ref[0, hd, blk, :, 0:Q_TILE_A] = jnp.where(feat < A_QK_DIM, q_blk, zero)
            qa_ref[0, hd, blk, :, Q_TILE_A:2 * Q_TILE_A] = jnp.where(feat >= A_QK_DIM, q_blk, zero)
        k = _rope(ak[:, sl], ca, sa_lo, sa_hi, half_a).astype(bf)
        ka_ref[0, hd] = k
        both = max_sq_norm(k, sel_maps)
        swapped = pltpu.roll(both, A_QK_DIM, 1)
        kna_ref[0, hd, 0, 0] = jnp.where(lane < A_QK_DIM, both, swapped)
        kna_ref[0, hd, 0, 1] = jnp.where(lane < A_QK_DIM, swapped, both)
        va_ref[0, hd, 0] = av[:, sl].T.astype(bf)

    q = _bdot(cq, wuq_ref[...])
    kv = _bdot(ckv, wukv_ref[...])
    sel_all = jnp.ones((B_QK_PAD, LANES), bf)
    for hd in range(B_HEADS):
        qn = q[:, hd * LANES:(hd + 1) * LANES]
        qr = q[:, (B_HEADS + hd) * LANES:(B_HEADS + hd + 1) * LANES]
        qb_ref[0, hd, 0, 0:LANES, :] = (qn * scale_b).T.astype(bf)
        qb_ref[0, hd, 0, LANES:2 * LANES, :] = (
            _rope(qr, cb, sb_lo, sb_hi, half_b) * scale_b).T.astype(bf)
        kn = kv[:, 2 * hd * LANES:(2 * hd + 1) * LANES].astype(bf)
        kb_ref[0, hd, :, 0:LANES] = kn
        kb_ref[0, hd, :, LANES:2 * LANES] = kr
        knb_ref[0, hd, 0, 0] = max_sq_norm(jnp.concatenate([kn, kr], axis=1), sel_all)
        vb_ref[0, hd, 0] = kv[:, (2 * hd + 1) * LANES:(2 * hd + 2) * LANES].T.astype(bf)

    sg_ref[0, :, 0:A_WIDTH] = _silu(proj(SEG_AG, A_WIDTH))
    sg_ref[0, :, A_WIDTH:A_WIDTH + B_WIDTH] = _silu(proj(SEG_BG, B_WIDTH))


def _in_proj(x, mod, gpre, win, gcq, wuq, gckv, wukv, tabs):
    b, s, _ = x.shape
    tr = ROW_TILE
    const = lambda shape: pl.BlockSpec(shape, lambda i, j: (0,) * len(shape))
    tab = pl.BlockSpec((tr, LANES), lambda i, j: (j, 0))
    head = lambda d: pl.BlockSpec((1, A_HEADS, tr, d), lambda i, j: (i, 0, j, 0))
    head_t = lambda d: pl.BlockSpec((1, A_HEADS, 1, d, tr), lambda i, j: (i, 0, j, 0, 0))
    norm_tiles = lambda m: pl.BlockSpec((1, A_HEADS, 1, m, SUBLANES, LANES),
                                        lambda i, j: (i, 0, j, 0, 0, 0))
    qa_blocks = tr // Q_TILE_A
    assert tr == Q_TILE_B and tr == qa_blocks * Q_TILE_A
    bf = jnp.bfloat16
    kern = functools.partial(_in_proj_kernel,
                             scale_a=A_QK_DIM ** -0.5 * LOG2E,
                             scale_b=(B_NOPE_DIM + B_ROPE_DIM) ** -0.5 * LOG2E)
    return pl.pallas_call(
        kern,
        out_shape=(
            jax.ShapeDtypeStruct((b, A_HEADS, s // Q_TILE_A, LANES, 2 * Q_TILE_A), bf),
            jax.ShapeDtypeStruct((b, A_HEADS, s, LANES), bf),
            jax.ShapeDtypeStruct((b, A_HEADS, s // tr, A_V_DIM, tr), bf),
            jax.ShapeDtypeStruct((b, B_HEADS, s // tr, B_QK_PAD, tr), bf),
            jax.ShapeDtypeStruct((b, B_HEADS, s, B_QK_PAD), bf),
            jax.ShapeDtypeStruct((b, B_HEADS, s // tr, B_V_DIM, tr), bf),
            jax.ShapeDtypeStruct((b, s, A_WIDTH + B_WIDTH), jnp.float32),
            jax.ShapeDtypeStruct((b, A_HEADS, s // tr, 2, SUBLANES, LANES), jnp.float32),
            jax.ShapeDtypeStruct((b, B_HEADS, s // tr, 1, SUBLANES, LANES), jnp.float32),
        ),
        grid=(b, s // tr),
        in_specs=[
            pl.BlockSpec((1, tr, D_MODEL), lambda i, j: (i, j, 0)),
            pl.BlockSpec((1, 1, 3 * D_MODEL), lambda i, j: (i, 0, 0)),
            const((1, D_MODEL)),
            const((D_MODEL, IN_COLS_PAD)),
            const((1, Q_LORA)),
            const((Q_LORA, 2 * B_HEADS * LANES)),
            const((1, KV_LORA)),
            const((KV_LORA, B_HEADS * (B_NOPE_DIM + B_V_DIM))),
            tab, tab, tab, tab, tab, tab,
        ],
        out_specs=(pl.BlockSpec((1, A_HEADS, qa_blocks, LANES, 2 * Q_TILE_A),
                                lambda i, j: (i, 0, j, 0, 0)),
                   head(LANES), head_t(A_V_DIM), head_t(B_QK_PAD), head(B_QK_PAD),
                   head_t(B_V_DIM),
                   pl.BlockSpec((1, tr, A_WIDTH + B_WIDTH), lambda i, j: (i, j, 0)),
                   norm_tiles(2), norm_tiles(1)),
        compiler_params=pltpu.CompilerParams(
            dimension_semantics=("arbitrary", "arbitrary"), vmem_limit_bytes=VMEM_LIMIT),
        name="in_proj",
    )(x, mod, gpre, win, gcq, wuq, gckv, wukv, *tabs)


def _key_tile(k_ref, t):
    return k_ref[0, 0, pl.ds(pl.multiple_of(t * KV_TILE, KV_TILE), KV_TILE), :]


def _col_bound(qt, kmax_tiles, cols_per_map):
    qsq = qt.astype(jnp.float32)
    qsq = (qsq * qsq).astype(jnp.bfloat16)
    ones = jnp.ones((2 * SUBLANES, qt.shape[0]), jnp.bfloat16)
    q2 = _bdot(ones, qsq)[0:SUBLANES]
    kmax = jnp.concatenate(
        [t for t in kmax_tiles for _ in range(cols_per_map // LANES)], axis=1)
    return jnp.sqrt(q2 * kmax) * BOUND_MARGIN


def _running_max_pass(qt, k_ref, vt_ref, m_ref, l_ref, acc_ref):
    n = qt.shape[1]
    n_kv = k_ref.shape[2] // KV_TILE
    m_ref[...] = jnp.full(m_ref.shape, -jnp.inf, jnp.float32)
    l_ref[...] = jnp.zeros(l_ref.shape, jnp.float32)
    acc_ref[...] = jnp.zeros(acc_ref.shape, jnp.float32)

    def body(t, carry):
        s = _bdot(_key_tile(k_ref, t), qt).reshape(KV_TILE // SUBLANES, SUBLANES, n)
        m_old = m_ref[...]
        m_col = jnp.max(jnp.max(s, axis=0), axis=0, keepdims=True)
        m_new = jnp.maximum(m_old, m_col)
        alpha = jnp.exp2(m_old - m_new)
        p = jnp.exp2(s - m_new)
        l_ref[...] = alpha * l_ref[...] + jnp.sum(p, axis=0)
        pv = _bdot(vt_ref[0, 0, t], p.reshape(KV_TILE, n).astype(jnp.bfloat16))
        acc_ref[...] = alpha[0:1] * acc_ref[...] + pv
        m_ref[...] = m_new
        return carry

    lax.fori_loop(0, n_kv, body, 0)


def _attention_chunk(q_tile, finish, tq, k_ref, vt_ref, kn_ref,
                     bound_ref, pa_ref, pb_ref, m_ref, l_ref, acc_ref):
    nq, _, n = bound_ref.shape
    n_grp = k_ref.shape[2] // (KV_TILE * KV_UNROLL)
    total = n_grp * nq
    log_nq = nq.bit_length() - 1
    assert nq == 1 << log_nq and total % 2 == 0 and total >= 2

    kmax_tiles = [jnp.max(kn_ref[0, 0, :, m], axis=0) for m in range(kn_ref.shape[3])]
    for i in range(nq):
        bound_ref[i] = _col_bound(q_tile(i), kmax_tiles, tq)
    l_ref[...] = jnp.zeros(l_ref.shape, jnp.float32)
    acc_ref[...] = jnp.zeros(acc_ref.shape, jnp.float32)

    def numerators(step, p_ref):
        g, i = step >> log_nq, step & (nq - 1)
        qt = q_tile(i)
        bound = bound_ref[i]
        lsum = l_ref[i]
        rows = KV_UNROLL * KV_TILE
        keys = k_ref[0, 0, pl.ds(pl.multiple_of(g * rows, rows), rows), :]
        s = _bdot(keys, qt)
        p = jnp.exp2(s.reshape(rows // SUBLANES, SUBLANES, n) - bound)
        p_ref[...] = p.reshape(rows, n).astype(jnp.bfloat16)
        l_ref[i] = lsum + jnp.sum(p, axis=0)

    def weighted_values(step, p_ref):
        g, i = step >> log_nq, step & (nq - 1)
        vt = jnp.concatenate([vt_ref[0, 0, g * KV_UNROLL + u] for u in range(KV_UNROLL)], axis=1)
        acc_ref[i] += _bdot(vt, p_ref[...])

    p_refs = (pa_ref, pb_ref)

    def pipeline_step(step, parity):
        numerators(step + 1, p_refs[1 - parity])
        weighted_values(step, p_refs[parity])

    numerators(0, pa_ref)
    unroll = PIPE_BODY_COLUMNS // n
    assert unroll % 2 == 0
    n_body = (total - 1) // unroll

    def body(t, carry):
        for h in range(unroll):
            pipeline_step(unroll * t + h, h % 2)
        return carry

    lax.fori_loop(0, n_body, body, 0)
    for step in range(n_body * unroll, total - 1):
        pipeline_step(step, step % 2)
    weighted_values(total - 1, p_refs[(total - 1) % 2])

    def usable(l_part):
        l_col = jnp.sum(l_part, axis=-2)
        return (jnp.min(l_col) >= L_MIN) & (jnp.max(l_col) <= L_MAX)

    def recompute_if_unusable(i, carry):
        @pl.when(jnp.logical_not(usable(l_ref[i])))
        def _():
            _running_max_pass(q_tile(i), k_ref, vt_ref, m_ref, l_ref.at[i], acc_ref.at[i])

        return carry

    @pl.when(jnp.logical_not(usable(l_ref[...])))
    def _():
        lax.fori_loop(0, nq, recompute_if_unusable, 0)

    def finish_tiles(t, carry):
        for h in range(FINISH_UNROLL):
            i = FINISH_UNROLL * t + h
            finish(i, acc_ref[i], jnp.sum(l_ref[i], axis=0, keepdims=True))
        return carry

    assert nq % FINISH_UNROLL == 0
    lax.fori_loop(0, nq // FINISH_UNROLL, finish_tiles, 0)


def _rows(i, tq):
    return pl.ds(pl.multiple_of(i * tq, tq), tq)


def _attn_a_kernel(par_ref, q_ref, k_ref, vt_ref, kn_ref, sg_ref, o_ref,
                   bound_ref, pa_ref, pb_ref, m_ref, l_ref, acc_ref, *, lambda_init):
    tq = Q_TILE_A

    def q_tile(i):
        return q_ref[0, 0, i]

    lam_v = par_ref[0:4, 0:A_QK_DIM]
    gsub = par_ref[SUBLANES:SUBLANES + 1, :]
    lam = (jnp.exp(jnp.sum(lam_v[0:1] * lam_v[1:2], axis=-1, keepdims=True))
           - jnp.exp(jnp.sum(lam_v[2:3] * lam_v[3:4], axis=-1, keepdims=True))
           + lambda_init)

    def finish(i, acc, l):
        ot = acc * (1.0 / l)
        o = (ot[:, 0:tq] - lam * ot[:, tq:2 * tq]).T
        o = _rms(o, gsub) * (1.0 - lambda_init)
        o_ref[0, _rows(i, tq), :] = (o * sg_ref[0, _rows(i, tq), :]).astype(o_ref.dtype)

    _attention_chunk(q_tile, finish, tq, k_ref, vt_ref, kn_ref,
                     bound_ref, pa_ref, pb_ref, m_ref, l_ref, acc_ref)


def _attn_b_kernel(q_ref, k_ref, vt_ref, kn_ref, sg_ref, o_ref,
                   bound_ref, pa_ref, pb_ref, m_ref, l_ref, acc_ref):
    tq = Q_TILE_B

    def q_tile(i):
        return q_ref[0, 0, i]

    def finish(i, acc, l):
        o = (acc * (1.0 / l)).T
        o_ref[0, _rows(i, tq), :] = (o * sg_ref[0, _rows(i, tq), :]).astype(o_ref.dtype)

    _attention_chunk(q_tile, finish, tq, k_ref, vt_ref, kn_ref,
                     bound_ref, pa_ref, pb_ref, m_ref, l_ref, acc_ref)


def _query_chunk(s, dq, dv, tq, n):
    group = KV_UNROLL * KV_TILE
    for qc in Q_CHUNKS:
        if s % qc:
            continue
        windows = 2 * (s * dq * 2 + s * dv * 2 + (qc // tq) * dq * n * 2 + qc * dv * 4 + qc * dv * 2)
        scratch = (qc // tq) * (dv + 2 * SUBLANES) * n * 4 + 2 * group * n * 2
        step_temps = 2 * group * n * 4
        if windows + scratch + step_temps <= ATTN_VMEM_BUDGET:
            return qc
    raise ValueError(f"no query chunk fits VMEM for sequence length {s}")


def _attn_call(kern, name, extra_in, extra_specs, q, k, vt, kn, sg, tq, col0):
    b, nh, s, dq = k.shape
    dv = vt.shape[3]
    n_maps = kn.shape[3]
    n = q.shape[4]
    qc = _query_chunk(s, dq, dv, tq, n)
    nq = qc // tq
    assert q.shape == (b, nh, s // tq, dq, n) and n % (n_maps * LANES) == 0
    assert kn.shape == (b, nh, s // KV_TILE, n_maps, SUBLANES, LANES)
    return pl.pallas_call(
        kern,
        out_shape=jax.ShapeDtypeStruct((b, s, nh * dv), jnp.bfloat16),
        grid=(b, nh, s // qc),
        in_specs=extra_specs + [
            pl.BlockSpec((1, 1, nq, dq, n), lambda b_, h, c: (b_, h, c, 0, 0)),
            pl.BlockSpec((1, 1, s, dq), lambda b_, h, c: (b_, h, 0, 0)),
            pl.BlockSpec((1, 1, s // KV_TILE, dv, KV_TILE), lambda b_, h, c: (b_, h, 0, 0, 0)),
            pl.BlockSpec((1, 1, s // KV_TILE, n_maps, SUBLANES, LANES),
                         lambda b_, h, c: (b_, h, 0, 0, 0, 0)),
            pl.BlockSpec((1, qc, dv), lambda b_, h, c: (b_, c, col0 + h)),
        ],
        out_specs=pl.BlockSpec((1, qc, dv), lambda b_, h, c: (b_, c, h)),
        scratch_shapes=[
            pltpu.VMEM((nq, SUBLANES, n), jnp.float32),
            pltpu.VMEM((KV_UNROLL * KV_TILE, n), jnp.bfloat16),
            pltpu.VMEM((KV_UNROLL * KV_TILE, n), jnp.bfloat16),
            pltpu.VMEM((SUBLANES, n), jnp.float32),
            pltpu.VMEM((nq, SUBLANES, n), jnp.float32),
            pltpu.VMEM((nq, dv, n), jnp.float32),
        ],
        compiler_params=pltpu.CompilerParams(
            dimension_semantics=("arbitrary", "arbitrary", "arbitrary"),
            vmem_limit_bytes=VMEM_LIMIT),
        name=name,
    )(*extra_in, q, k, vt, kn, sg)


def _attn_a(lam_vecs, gsub, qa, ka, vta, kna, sg, lambda_init):
    par = jnp.zeros((4 * SUBLANES, LANES), jnp.float32)
    par = par.at[0:4, 0:A_QK_DIM].set(lam_vecs).at[SUBLANES:SUBLANES + 1, :].set(gsub)
    spec = pl.BlockSpec(par.shape, lambda b_, h, c: (0, 0))
    return _attn_call(functools.partial(_attn_a_kernel, lambda_init=lambda_init), "attn_a",
                      (par,), [spec], qa, ka, vta, kna, sg, Q_TILE_A, 0)


def _attn_b(qb, kb, vtb, knb, sg):
    return _attn_call(_attn_b_kernel, "attn_b", (), [], qb, kb, vtb, knb, sg, Q_TILE_B, A_HEADS)


def _out_proj_kernel(ya_ref, yb_ref, x_ref, mod_ref, wout_ref, gpost_ref, o_ref):
    out = _bdot(jnp.concatenate([ya_ref[0], yb_ref[0]], axis=1), wout_ref[...])
    gate = mod_ref[0][:, 2 * D_MODEL:3 * D_MODEL]
    o_ref[0] = x_ref[0] + gate * _rms(out, gpost_ref[...])


def _out_proj(ya, yb, x, mod, wout, gpost):
    b, s, _ = x.shape
    tr = OUT_ROW_TILE
    return pl.pallas_call(
        _out_proj_kernel,
        out_shape=jax.ShapeDtypeStruct(x.shape, x.dtype),
        grid=(b, s // tr),
        in_specs=[
            pl.BlockSpec((1, tr, A_WIDTH), lambda i, j: (i, j, 0)),
            pl.BlockSpec((1, tr, B_WIDTH), lambda i, j: (i, j, 0)),
            pl.BlockSpec((1, tr, D_MODEL), lambda i, j: (i, j, 0)),
            pl.BlockSpec((1, 1, 3 * D_MODEL), lambda i, j: (i, 0, 0)),
            pl.BlockSpec((A_WIDTH + B_WIDTH, D_MODEL), lambda i, j: (0, 0)),
            pl.BlockSpec((1, D_MODEL), lambda i, j: (0, 0)),
        ],
        out_specs=pl.BlockSpec((1, tr, D_MODEL), lambda i, j: (i, j, 0)),
        compiler_params=pltpu.CompilerParams(
            dimension_semantics=("arbitrary", "arbitrary"), vmem_limit_bytes=VMEM_LIMIT),
        name="out_proj",
    )(ya, yb, x, mod, wout, gpost)


def _rope_tables(seq, dim):
    half = dim // 2
    inv_freq = jnp.float32(ROPE_THETA) ** (-jnp.arange(0, dim, 2, dtype=jnp.float32) / dim)
    d = np.arange(LANES) % 64
    in_span, lo = d < dim, d < half
    freq = jnp.where(in_span, inv_freq[d % half], 0.0)
    ang = jnp.arange(seq, dtype=jnp.float32)[:, None] * freq[None, :]
    cos, sin = jnp.cos(ang), jnp.sin(ang)
    return cos, jnp.where(lo, -sin, 0.0), jnp.where(in_span & ~lo, sin, 0.0)


def _layout_w_in(w_in):
    aq, ak, av, ag, cq, ckv, kr, bg = jnp.split(
        w_in, [512, 1024, 1536, 2048, 2432, 2688, 2752], axis=1)
    kr = jnp.pad(kr, ((0, 0), (0, LANES - B_ROPE_DIM)))
    return jnp.concatenate([aq, ak, av, ag, cq, kr, ckv, bg], axis=1).astype(jnp.bfloat16)


def _layout_w_uq(w_uq):
    w = w_uq.reshape(Q_LORA, B_HEADS, B_NOPE_DIM + B_ROPE_DIM)
    nope = w[:, :, :B_NOPE_DIM].reshape(Q_LORA, B_HEADS * B_NOPE_DIM)
    rope = jnp.pad(w[:, :, B_NOPE_DIM:], ((0, 0), (0, 0), (0, LANES - B_ROPE_DIM)))
    return jnp.concatenate([nope, rope.reshape(Q_LORA, B_HEADS * LANES)], axis=1).astype(jnp.bfloat16)


def _encoder_layer(x, mod, layer, tabs, g_pre, w_in, lam_vecs, g_subln, g_cq, w_uq, g_ckv, w_ukv,
                   w_out, g_post):
    lambda_init = 0.8 - 0.6 * math.exp(-0.3 * layer)
    qa, ka, vta, qb, kb, vtb, sg, kna, knb = _in_proj(
        x, mod, g_pre, w_in, g_cq, w_uq, g_ckv, w_ukv, tabs)
    ya = _attn_a(lam_vecs, g_subln, qa, ka, vta, kna, sg, lambda_init)
    yb = _attn_b(qb, kb, vtb, knb, sg)
    return _out_proj(ya, yb, x, mod, w_out, g_post)


def kernel(x_prompt, x_sample, c_prompt, c_sample, w_ada, b_ada, g_pre, w_in, lambda_q1, lambda_k1,
           lambda_q2, lambda_k2, g_subln, g_cq, w_uq, g_ckv, w_ukv, w_out, g_post):
    depth = w_ada.shape[0]
    nb_p, nb_s = c_prompt.shape[0], c_sample.shape[0]
    pad_rows = -(nb_p + nb_s) % 8
    c_all = jnp.concatenate(
        [c_prompt, c_sample, jnp.zeros((pad_rows, D_MODEL), jnp.float32)], axis=0)
    max_seq = max(x_prompt.shape[1], x_sample.shape[1])
    tabs = _rope_tables(max_seq, A_ROPE_DIM) + _rope_tables(max_seq, B_ROPE_DIM)
    y_p, y_s = x_prompt, x_sample
    for layer in range(depth):
        mod = _adaln_mod(c_all, w_ada[layer], b_ada[layer])
        mod_p = mod[:nb_p].reshape(nb_p, 1, 3 * D_MODEL)
        mod_s = mod[nb_p:nb_p + nb_s].reshape(nb_s, 1, 3 * D_MODEL)
        lam_vecs = jnp.stack([lambda_q1[layer], lambda_k1[layer], lambda_q2[layer], lambda_k2[layer]])
        params = (g_pre[layer].reshape(1, -1), _layout_w_in(w_in[layer]), lam_vecs,
                  g_subln[layer].reshape(1, -1), g_cq[layer].reshape(1, -1),
                  _layout_w_uq(w_uq[layer]), g_ckv[layer].reshape(1, -1),
                  w_ukv[layer].astype(jnp.bfloat16), w_out[layer].astype(jnp.bfloat16),
                  g_post[layer].reshape(1, -1))
        y_p = _encoder_layer(y_p, mod_p, layer, tabs, *params)
        y_s = _encoder_layer(y_s, mod_s, layer, tabs, *params)
    return (y_p, y_s)
```

```python
import functools
import math

import numpy as np
import jax
import jax.numpy as jnp
from jax import lax
from jax.experimental import pallas as pl
from jax.experimental.pallas import tpu as pltpu

D_MODEL = 1024
A_HEADS = 4
A_QK_DIM = 64
A_V_DIM = 128
A_WIDTH = A_HEADS * A_V_DIM
A_ROPE_DIM = A_QK_DIM // 4
B_HEADS = 4
B_NOPE_DIM = 128
B_ROPE_DIM = 64
B_V_DIM = 128
B_WIDTH = B_HEADS * B_V_DIM
B_QK_PAD = 256
Q_LORA = 384
KV_LORA = 256
ROPE_THETA = 500000.0
NORM_EPS = 1e-6
LOG2E = 1.4426950408889634

LANES = 128
SUBLANES = 8
VMEM_LIMIT = 56 * 1024 * 1024

ROW_TILE = 512
OUT_ROW_TILE = 1024
KV_TILE = ROW_TILE
KV_UNROLL = 2
Q_TILE_A = 512
Q_TILE_B = 512
Q_CHUNKS = (8192, 4096, 2048)
ATTN_VMEM_BUDGET = 48 * 1024 * 1024
PIPE_BODY_COLUMNS = 32 * 512
FINISH_UNROLL = 4

L_MIN = 2.0 ** -80
L_MAX = 2.0 ** 100
BOUND_MARGIN = 1.0 + 2.0 ** -5

SEG_AQ = 0
SEG_AK = 512
SEG_AV = 1024
SEG_AG = 1536
SEG_CQ = 2048
SEG_KR = 2432
SEG_CKV = 2560
SEG_BG = 2816
IN_COLS_PAD = 3328


def _silu(x):
    return x * (1.0 / (1.0 + jnp.exp(-x)))


def _rms(x, g):
    return x * lax.rsqrt(jnp.mean(x * x, axis=-1, keepdims=True) + NORM_EPS) * g


def _bdot(a, b):
    return jnp.dot(a, b, preferred_element_type=jnp.float32)


def _mod_kernel(c_ref, w_ref, b_ref, o_ref):
    c = _silu(c_ref[...]).astype(jnp.bfloat16)
    o_ref[...] = _bdot(c, w_ref[...].astype(jnp.bfloat16)) + b_ref[...]


def _adaln_mod(c_all, w_ada, b_ada):
    rows = c_all.shape[0]
    n_blk = 1024
    return pl.pallas_call(
        _mod_kernel,
        out_shape=jax.ShapeDtypeStruct((rows, 3 * D_MODEL), jnp.float32),
        grid=(3 * D_MODEL // n_blk,),
        in_specs=[
            pl.BlockSpec((rows, D_MODEL), lambda j: (0, 0)),
            pl.BlockSpec((D_MODEL, n_blk), lambda j: (0, j)),
            pl.BlockSpec((1, n_blk), lambda j: (0, j)),
        ],
        out_specs=pl.BlockSpec((rows, n_blk), lambda j: (0, j)),
        compiler_params=pltpu.CompilerParams(
            dimension_semantics=("arbitrary",), vmem_limit_bytes=VMEM_LIMIT),
        name="adaln_mod",
    )(c_all, w_ada, b_ada.reshape(1, -1))


def _rope(x, c, s_lo, s_hi, half):
    return (x * c + pltpu.roll(x, LANES - half, 1) * s_lo
            + pltpu.roll(x, half, 1) * s_hi)


def _in_proj_kernel(x_ref, mod_ref, gpre_ref, win_ref, gcq_ref, wuq_ref, gckv_ref, wukv_ref,
                    ca_ref, sa_lo_ref, sa_hi_ref, cb_ref, sb_lo_ref, sb_hi_ref,
                    qa_ref, ka_ref, va_ref, qb_ref, kb_ref, vb_ref, sg_ref, kna_ref, knb_ref,
                    *, scale_a, scale_b):
    mod = mod_ref[0][0:1, :]
    shift = mod[:, 0:D_MODEL]
    scale = mod[:, D_MODEL:2 * D_MODEL]
    half_a = A_ROPE_DIM // 2
    half_b = B_ROPE_DIM // 2
    bf = jnp.bfloat16

    h = (_rms(x_ref[0], gpre_ref[0:1, :]) * (1.0 + scale) + shift).astype(bf)

    def proj(lo, width):
        return _bdot(h, win_ref[:, lo:lo + width])

    ca, sa_lo, sa_hi = ca_ref[...], sa_lo_ref[...], sa_hi_ref[...]
    cb, sb_lo, sb_hi = cb_ref[...], sb_lo_ref[...], sb_hi_ref[...]

    cq_kr = proj(SEG_CQ, SEG_CKV - SEG_CQ)
    cq = _rms(cq_kr[:, 0:Q_LORA], gcq_ref[0:1, :]).astype(bf)
    kr = _rope(cq_kr[:, SEG_KR - SEG_CQ:], cb, sb_lo, sb_hi, half_b).astype(bf)
    ckv = _rms(proj(SEG_CKV, KV_LORA), gckv_ref[0:1, :]).astype(bf)

    def max_sq_norm(k, sel):
        n2 = _bdot(k * k, sel)
        part = jnp.max(n2.reshape(n2.shape[0] // SUBLANES, SUBLANES, LANES), axis=0)
        return jnp.broadcast_to(jnp.max(part, axis=0, keepdims=True), (SUBLANES, LANES))

    aq = proj(SEG_AQ, 512)
    ak = proj(SEG_AK, 512)
    av = proj(SEG_AV, 512)
    feat = lax.broadcasted_iota(jnp.int32, (LANES, Q_TILE_A), 0)
    sel_r = lax.broadcasted_iota(jnp.int32, (LANES, LANES), 0)
    sel_c = lax.broadcasted_iota(jnp.int32, (LANES, LANES), 1)
    same_map = (((sel_r < A_QK_DIM) & (sel_c < A_QK_DIM))
                | ((sel_r >= A_QK_DIM) & (sel_c >= A_QK_DIM)))
    sel_maps = jnp.where(same_map, 1.0, 0.0).astype(bf)
    lane = lax.broadcasted_iota(jnp.int32, (SUBLANES, LANES), 1)
    for hd in range(A_HEADS):
        sl = slice(hd * LANES, (hd + 1) * LANES)
        qt = (_rope(aq[:, sl], ca, sa_lo, sa_hi, half_a) * scale_a).T.astype(bf)
        for blk in range(ROW_TILE // Q_TILE_A):
            q_blk = qt[:, blk * Q_TILE_A:(blk + 1) * Q_TILE_A]
            zero = jnp.zeros_like(q_blk)
            qa_ref[0, hd, blk, :, 0:Q_TILE_A] = jnp.where(feat < A_QK_DIM, q_blk, zero)
            qa_ref[0, hd, blk, :, Q_TILE_A:2 * Q_TILE_A] = jnp.where(feat >= A_QK_DIM, q_blk, zero)
        k = _rope(ak[:, sl], ca, sa_lo, sa_hi, half_a).astype(bf)
        ka_ref[0, hd] = k
        both = max_sq_norm(k, sel_maps)
        swapped = pltpu.roll(both, A_QK_DIM, 1)
        kna_ref[0, hd, 0, 0] = jnp.where(lane < A_QK_DIM, both, swapped)
        kna_ref[0, hd, 0, 1] = jnp.where(lane < A_QK_DIM, swapped, both)
        va_ref[0, hd, 0] = av[:, sl].T.astype(bf)

    q = _bdot(cq, wuq_ref[...])
    kv = _bdot(ckv, wukv_ref[...])
    sel_all = jnp.ones((B_QK_PAD, LANES), bf)
    for hd in range(B_HEADS):
        qn = q[:, hd * LANES:(hd + 1) * LANES]
        qr = q[:, (B_HEADS + hd) * LANES:(B_HEADS + hd + 1) * LANES]
        qb_ref[0, hd, 0, 0:LANES, :] = (qn * scale_b).T.astype(bf)
        qb_ref[0, hd, 0, LANES:2 * LANES, :] = (
            _rope(qr, cb, sb_lo, sb_hi, half_b) * scale_b).T.astype(bf)
        kn = kv[:, 2 * hd * LANES:(2 * hd + 1) * LANES].astype(bf)
        kb_ref[0, hd, :, 0:LANES] = kn
        kb_ref[0, hd, :, LANES:2 * LANES] = kr
        knb_ref[0, hd, 0, 0] = max_sq_norm(jnp.concatenate([kn, kr], axis=1), sel_all)
        vb_ref[0, hd, 0] = kv[:, (2 * hd + 1) * LANES:(2 * hd + 2) * LANES].T.astype(bf)

    sg_ref[0, :, 0:A_WIDTH] = _silu(proj(SEG_AG, A_WIDTH))
    sg_ref[0, :, A_WIDTH:A_WIDTH + B_WIDTH] = _silu(proj(SEG_BG, B_WIDTH))


def _in_proj(x, mod, gpre, win, gcq, wuq, gckv, wukv, tabs):
    b, s, _ = x.shape
    tr = ROW_TILE
    const = lambda shape: pl.BlockSpec(shape, lambda i, j: (0,) * len(shape))
    tab = pl.BlockSpec((tr, LANES), lambda i, j: (j, 0))
    head = lambda d: pl.BlockSpec((1, A_HEADS, tr, d), lambda i, j: (i, 0, j, 0))
    head_t = lambda d: pl.BlockSpec((1, A_HEADS, 1, d, tr), lambda i, j: (i, 0, j, 0, 0))
    norm_tiles = lambda m: pl.BlockSpec((1, A_HEADS, 1, m, SUBLANES, LANES),
                                        lambda i, j: (i, 0, j, 0, 0, 0))
    qa_blocks = tr // Q_TILE_A
    assert tr == Q_TILE_B and tr == qa_blocks * Q_TILE_A
    bf = jnp.bfloat16
    kern = functools.partial(_in_proj_kernel,
                             scale_a=A_QK_DIM ** -0.5 * LOG2E,
                             scale_b=(B_NOPE_DIM + B_ROPE_DIM) ** -0.5 * LOG2E)
    return pl.pallas_call(
        kern,
        out_shape=(
            jax.ShapeDtypeStruct((b, A_HEADS, s // Q_TILE_A, LANES, 2 * Q_TILE_A), bf),
            jax.ShapeDtypeStruct((b, A_HEADS, s, LANES), bf),
            jax.ShapeDtypeStruct((b, A_HEADS, s // tr, A_V_DIM, tr), bf),
            jax.ShapeDtypeStruct((b, B_HEADS, s // tr, B_QK_PAD, tr), bf),
            jax.ShapeDtypeStruct((b, B_HEADS, s, B_QK_PAD), bf),
            jax.ShapeDtypeStruct((b, B_HEADS, s // tr, B_V_DIM, tr), bf),
            jax.ShapeDtypeStruct((b, s, A_WIDTH + B_WIDTH), jnp.float32),
            jax.ShapeDtypeStruct((b, A_HEADS, s // tr, 2, SUBLANES, LANES), jnp.float32),
            jax.ShapeDtypeStruct((b, B_HEADS, s // tr, 1, SUBLANES, LANES), jnp.float32),
        ),
        grid=(b, s // tr),
        in_specs=[
            pl.BlockSpec((1, tr, D_MODEL), lambda i, j: (i, j, 0)),
            pl.BlockSpec((1, SUBLANES, 3 * D_MODEL), lambda i, j: (i, 0, 0)),
            const((SUBLANES, D_MODEL)),
            const((D_MODEL, IN_COLS_PAD)),
            const((4 * SUBLANES, Q_LORA)),
            const((Q_LORA, 2 * B_HEADS * LANES)),
            const((2 * SUBLANES, KV_LORA)),
            const((KV_LORA, B_HEADS * (B_NOPE_DIM + B_V_DIM))),
            tab, tab, tab, tab, tab, tab,
        ],
        out_specs=(pl.BlockSpec((1, A_HEADS, qa_blocks, LANES, 2 * Q_TILE_A),
                                lambda i, j: (i, 0, j, 0, 0)),
                   head(LANES), head_t(A_V_DIM), head_t(B_QK_PAD), head(B_QK_PAD),
                   head_t(B_V_DIM),
                   pl.BlockSpec((1, tr, A_WIDTH + B_WIDTH), lambda i, j: (i, j, 0)),
                   norm_tiles(2), norm_tiles(1)),
        compiler_params=pltpu.CompilerParams(
            dimension_semantics=("arbitrary", "arbitrary"), vmem_limit_bytes=VMEM_LIMIT),
        name="in_proj",
    )(x, mod, gpre, win, gcq, wuq, gckv, wukv, *tabs)


def _key_tile(k_ref, t):
    return k_ref[0, 0, pl.ds(pl.multiple_of(t * KV_TILE, KV_TILE), KV_TILE), :]


def _col_bound(qt, kmax_tiles, cols_per_map):
    qsq = qt.astype(jnp.float32)
    qsq = (qsq * qsq).astype(jnp.bfloat16)
    ones = jnp.ones((2 * SUBLANES, qt.shape[0]), jnp.bfloat16)
    q2 = _bdot(ones, qsq)[0:SUBLANES]
    kmax = jnp.concatenate(
        [t for t in kmax_tiles for _ in range(cols_per_map // LANES)], axis=1)
    return jnp.sqrt(q2 * kmax) * BOUND_MARGIN


def _running_max_pass(qt, k_ref, vt_ref, m_ref, l_ref, acc_ref):
    n = qt.shape[1]
    n_kv = k_ref.shape[2] // KV_TILE
    m_ref[...] = jnp.full(m_ref.shape, -jnp.inf, jnp.float32)
    l_ref[...] = jnp.zeros(l_ref.shape, jnp.float32)
    acc_ref[...] = jnp.zeros(acc_ref.shape, jnp.float32)

    def body(t, carry):
        s = _bdot(_key_tile(k_ref, t), qt).reshape(KV_TILE // SUBLANES, SUBLANES, n)
        m_old = m_ref[...]
        m_col = jnp.max(jnp.max(s, axis=0), axis=0, keepdims=True)
        m_new = jnp.maximum(m_old, m_col)
        alpha = jnp.exp2(m_old - m_new)
        p = jnp.exp2(s - m_new)
        l_ref[...] = alpha * l_ref[...] + jnp.sum(p, axis=0)
        pv = _bdot(vt_ref[0, 0, t], p.reshape(KV_TILE, n).astype(jnp.bfloat16))
        acc_ref[...] = alpha[0:1] * acc_ref[...] + pv
        m_ref[...] = m_new
        return carry

    lax.fori_loop(0, n_kv, body, 0)


def _attention_chunk(q_tile, finish, tq, k_ref, vt_ref, kn_ref,
                     bound_ref, pa_ref, pb_ref, m_ref, l_ref, acc_ref):
    nq, _, n = bound_ref.shape
    n_grp = k_ref.shape[2] // (KV_TILE * KV_UNROLL)
    total = n_grp * nq
    log_nq = nq.bit_length() - 1
    assert nq == 1 << log_nq and total % 2 == 0 and total >= 2

    kmax_tiles = [jnp.max(kn_ref[0, 0, :, m], axis=0) for m in range(kn_ref.shape[3])]
    for i in range(nq):
        bound_ref[i] = _col_bound(q_tile(i), kmax_tiles, tq)
    l_ref[...] = jnp.zeros(l_ref.shape, jnp.float32)
    acc_ref[...] = jnp.zeros(acc_ref.shape, jnp.float32)

    def numerators(step, p_ref):
        g, i = step >> log_nq, step & (nq - 1)
        qt = q_tile(i)
        bound = bound_ref[i]
        lsum = l_ref[i]
        rows = KV_UNROLL * KV_TILE
        keys = k_ref[0, 0, pl.ds(pl.multiple_of(g * rows, rows), rows), :]
        s = _bdot(keys, qt)
        p = jnp.exp2(s.reshape(rows // SUBLANES, SUBLANES, n) - bound)
        p_ref[...] = p.reshape(rows, n).astype(jnp.bfloat16)
        l_ref[i] = lsum + jnp.sum(p, axis=0)

    def weighted_values(step, p_ref):
        g, i = step >> log_nq, step & (nq - 1)
        vt = jnp.concatenate([vt_ref[0, 0, g * KV_UNROLL + u] for u in range(KV_UNROLL)], axis=1)
        acc_ref[i] += _bdot(vt, p_ref[...])

    p_refs = (pa_ref, pb_ref)

    def pipeline_step(step, parity):
        numerators(step + 1, p_refs[1 - parity])
        weighted_values(step, p_refs[parity])

    numerators(0, pa_ref)
    unroll = PIPE_BODY_COLUMNS // n
    assert unroll % 2 == 0
    n_body = (total - 1) // unroll

    def body(t, carry):
        for h in range(unroll):
            pipeline_step(unroll * t + h, h % 2)
        return carry

    lax.fori_loop(0, n_body, body, 0)
    for step in range(n_body * unroll, total - 1):
        pipeline_step(step, step % 2)
    weighted_values(total - 1, p_refs[(total - 1) % 2])

    def usable(l_part):
        l_col = jnp.sum(l_part, axis=-2)
        return (jnp.min(l_col) >= L_MIN) & (jnp.max(l_col) <= L_MAX)

    def recompute_if_unusable(i, carry):
        @pl.when(jnp.logical_not(usable(l_ref[i])))
        def _():
            _running_max_pass(q_tile(i), k_ref, vt_ref, m_ref, l_ref.at[i], acc_ref.at[i])

        return carry

    @pl.when(jnp.logical_not(usable(l_ref[...])))
    def _():
        lax.fori_loop(0, nq, recompute_if_unusable, 0)

    def finish_tiles(t, carry):
        for h in range(FINISH_UNROLL):
            i = FINISH_UNROLL * t + h
            finish(i, acc_ref[i], jnp.sum(l_ref[i], axis=0, keepdims=True))
        return carry

    assert nq % FINISH_UNROLL == 0
    lax.fori_loop(0, nq // FINISH_UNROLL, finish_tiles, 0)


def _rows(i, tq):
    return pl.ds(pl.multiple_of(i * tq, tq), tq)


def _attn_a_kernel(par_ref, q_ref, k_ref, vt_ref, kn_ref, sg_ref, o_ref,
                   bound_ref, pa_ref, pb_ref, m_ref, l_ref, acc_ref, *, lambda_init):
    tq = Q_TILE_A

    def q_tile(i):
        return q_ref[0, 0, i]

    lam_v = par_ref[0:4, 0:A_QK_DIM]
    gsub = par_ref[SUBLANES:SUBLANES + 1, :]
    lam = (jnp.exp(jnp.sum(lam_v[0:1] * lam_v[1:2], axis=-1, keepdims=True))
           - jnp.exp(jnp.sum(lam_v[2:3] * lam_v[3:4], axis=-1, keepdims=True))
           + lambda_init)

    def finish(i, acc, l):
        ot = acc * (1.0 / l)
        o = (ot[:, 0:tq] - lam * ot[:, tq:2 * tq]).T
        o = _rms(o, gsub) * (1.0 - lambda_init)
        o_ref[0, _rows(i, tq), :] = (o * sg_ref[0, _rows(i, tq), :]).astype(o_ref.dtype)

    _attention_chunk(q_tile, finish, tq, k_ref, vt_ref, kn_ref,
                     bound_ref, pa_ref, pb_ref, m_ref, l_ref, acc_ref)


def _attn_b_kernel(q_ref, k_ref, vt_ref, kn_ref, sg_ref, o_ref,
                   bound_ref, pa_ref, pb_ref, m_ref, l_ref, acc_ref):
    tq = Q_TILE_B

    def q_tile(i):
        return q_ref[0, 0, i]

    def finish(i, acc, l):
        o = (acc * (1.0 / l)).T
        o_ref[0, _rows(i, tq), :] = (o * sg_ref[0, _rows(i, tq), :]).astype(o_ref.dtype)

    _attention_chunk(q_tile, finish, tq, k_ref, vt_ref, kn_ref,
                     bound_ref, pa_ref, pb_ref, m_ref, l_ref, acc_ref)


def _query_chunk(s, dq, dv, tq, n):
    group = KV_UNROLL * KV_TILE
    for qc in Q_CHUNKS:
        if s % qc:
            continue
        windows = 2 * (s * dq * 2 + s * dv * 2 + (qc // tq) * dq * n * 2 + qc * dv * 4 + qc * dv * 2)
        scratch = (qc // tq) * (dv + 2 * SUBLANES) * n * 4 + 2 * group * n * 2
        step_temps = 2 * group * n * 4
        if windows + scratch + step_temps <= ATTN_VMEM_BUDGET:
            return qc
    raise ValueError(f"no query chunk fits VMEM for sequence length {s}")


def _attn_call(kern, name, extra_in, extra_specs, q, k, vt, kn, sg, tq, col0):
    b, nh, s, dq = k.shape
    dv = vt.shape[3]
    n_maps = kn.shape[3]
    n = q.shape[4]
    qc = _query_chunk(s, dq, dv, tq, n)
    nq = qc // tq
    assert q.shape == (b, nh, s // tq, dq, n) and n % (n_maps * LANES) == 0
    assert kn.shape == (b, nh, s // KV_TILE, n_maps, SUBLANES, LANES)
    return pl.pallas_call(
        kern,
        out_shape=jax.ShapeDtypeStruct((b, s, nh * dv), jnp.bfloat16),
        grid=(b, nh, s // qc),
        in_specs=extra_specs + [
            pl.BlockSpec((1, 1, nq, dq, n), lambda b_, h, c: (b_, h, c, 0, 0)),
            pl.BlockSpec((1, 1, s, dq), lambda b_, h, c: (b_, h, 0, 0)),
            pl.BlockSpec((1, 1, s // KV_TILE, dv, KV_TILE), lambda b_, h, c: (b_, h, 0, 0, 0)),
            pl.BlockSpec((1, 1, s // KV_TILE, n_maps, SUBLANES, LANES),
                         lambda b_, h, c: (b_, h, 0, 0, 0, 0)),
            pl.BlockSpec((1, qc, dv), lambda b_, h, c: (b_, c, col0 + h)),
        ],
        out_specs=pl.BlockSpec((1, qc, dv), lambda b_, h, c: (b_, c, h)),
        scratch_shapes=[
            pltpu.VMEM((nq, SUBLANES, n), jnp.float32),
            pltpu.VMEM((KV_UNROLL * KV_TILE, n), jnp.bfloat16),
            pltpu.VMEM((KV_UNROLL * KV_TILE, n), jnp.bfloat16),
            pltpu.VMEM((SUBLANES, n), jnp.float32),
            pltpu.VMEM((nq, SUBLANES, n), jnp.float32),
            pltpu.VMEM((nq, dv, n), jnp.float32),
        ],
        compiler_params=pltpu.CompilerParams(
            dimension_semantics=("arbitrary", "arbitrary", "arbitrary"),
            vmem_limit_bytes=VMEM_LIMIT),
        name=name,
    )(*extra_in, q, k, vt, kn, sg)


def _attn_a(lam_vecs, gsub, qa, ka, vta, kna, sg, lambda_init):
    par = jnp.zeros((4 * SUBLANES, LANES), jnp.float32)
    par = par.at[0:4, 0:A_QK_DIM].set(lam_vecs).at[SUBLANES:SUBLANES + 1, :].set(gsub)
    spec = pl.BlockSpec(par.shape, lambda b_, h, c: (0, 0))
    return _attn_call(functools.partial(_attn_a_kernel, lambda_init=lambda_init), "attn_a",
                      (par,), [spec], qa, ka, vta, kna, sg, Q_TILE_A, 0)


def _attn_b(qb, kb, vtb, knb, sg):
    return _attn_call(_attn_b_kernel, "attn_b", (), [], qb, kb, vtb, knb, sg, Q_TILE_B, A_HEADS)


def _out_proj_kernel(ya_ref, yb_ref, x_ref, mod_ref, wout_ref, gpost_ref, o_ref):
    out = _bdot(jnp.concatenate([ya_ref[0], yb_ref[0]], axis=1), wout_ref[...])
    gate = mod_ref[0][0:1, 2 * D_MODEL:3 * D_MODEL]
    o_ref[0] = x_ref[0] + gate * _rms(out, gpost_ref[0:1, :])


def _out_proj(ya, yb, x, mod, wout, gpost):
    b, s, _ = x.shape
    tr = OUT_ROW_TILE
    return pl.pallas_call(
        _out_proj_kernel,
        out_shape=jax.ShapeDtypeStruct(x.shape, x.dtype),
        grid=(b, s // tr),
        in_specs=[
            pl.BlockSpec((1, tr, A_WIDTH), lambda i, j: (i, j, 0)),
            pl.BlockSpec((1, tr, B_WIDTH), lambda i, j: (i, j, 0)),
            pl.BlockSpec((1, tr, D_MODEL), lambda i, j: (i, j, 0)),
            pl.BlockSpec((1, SUBLANES, 3 * D_MODEL), lambda i, j: (i, 0, 0)),
            pl.BlockSpec((A_WIDTH + B_WIDTH, D_MODEL), lambda i, j: (0, 0)),
            pl.BlockSpec((SUBLANES, D_MODEL), lambda i, j: (0, 0)),
        ],
        out_specs=pl.BlockSpec((1, tr, D_MODEL), lambda i, j: (i, j, 0)),
        compiler_params=pltpu.CompilerParams(
            dimension_semantics=("arbitrary", "arbitrary"), vmem_limit_bytes=VMEM_LIMIT),
        name="out_proj",
    )(ya, yb, x, mod, wout, gpost)


def _rope_tables(seq, dim):
    half = dim // 2
    inv_freq = jnp.float32(ROPE_THETA) ** (-jnp.arange(0, dim, 2, dtype=jnp.float32) / dim)
    d = np.arange(LANES) % 64
    in_span, lo = d < dim, d < half
    freq = jnp.where(in_span, inv_freq[d % half], 0.0)
    ang = jnp.arange(seq, dtype=jnp.float32)[:, None] * freq[None, :]
    cos, sin = jnp.cos(ang), jnp.sin(ang)
    return cos, jnp.where(lo, -sin, 0.0), jnp.where(in_span & ~lo, sin, 0.0)


def _layout_w_in(w_in):
    aq, ak, av, ag, cq, ckv, kr, bg = jnp.split(
        w_in, [512, 1024, 1536, 2048, 2432, 2688, 2752], axis=1)
    kr = jnp.pad(kr, ((0, 0), (0, LANES - B_ROPE_DIM)))
    return jnp.concatenate([aq, ak, av, ag, cq, kr, ckv, bg], axis=1).astype(jnp.bfloat16)


def _layout_w_uq(w_uq):
    w = w_uq.reshape(Q_LORA, B_HEADS, B_NOPE_DIM + B_ROPE_DIM)
    nope = w[:, :, :B_NOPE_DIM].reshape(Q_LORA, B_HEADS * B_NOPE_DIM)
    rope = jnp.pad(w[:, :, B_NOPE_DIM:], ((0, 0), (0, 0), (0, LANES - B_ROPE_DIM)))
    return jnp.concatenate([nope, rope.reshape(Q_LORA, B_HEADS * LANES)], axis=1).astype(jnp.bfloat16)


def _encoder_layer(x, mod, layer, tabs, g_pre, w_in, lam_vecs, g_subln, g_cq, w_uq, g_ckv, w_ukv,
                   w_out, g_post):
    lambda_init = 0.8 - 0.6 * math.exp(-0.3 * layer)
    qa, ka, vta, qb, kb, vtb, sg, kna, knb = _in_proj(
        x, mod, g_pre, w_in, g_cq, w_uq, g_ckv, w_ukv, tabs)
    ya = _attn_a(lam_vecs, g_subln, qa, ka, vta, kna, sg, lambda_init)
    yb = _attn_b(qb, kb, vtb, knb, sg)
    return _out_proj(ya, yb, x, mod, w_out, g_post)


def kernel(x_prompt, x_sample, c_prompt, c_sample, w_ada, b_ada, g_pre, w_in, lambda_q1, lambda_k1,
           lambda_q2, lambda_k2, g_subln, g_cq, w_uq, g_ckv, w_ukv, w_out, g_post):
    depth = w_ada.shape[0]
    nb_p, nb_s = c_prompt.shape[0], c_sample.shape[0]
    pad_rows = -(nb_p + nb_s) % 8
    c_all = jnp.concatenate(
        [c_prompt, c_sample, jnp.zeros((pad_rows, D_MODEL), jnp.float32)], axis=0)
    max_seq = max(x_prompt.shape[1], x_sample.shape[1])
    tabs = _rope_tables(max_seq, A_ROPE_DIM) + _rope_tables(max_seq, B_ROPE_DIM)
    y_p, y_s = x_prompt, x_sample
    for layer in range(depth):
        mod = _adaln_mod(c_all, w_ada[layer], b_ada[layer])
        rows = lambda v, r: jnp.broadcast_to(v.reshape(1, -1), (r, v.size))
        mod_p = jnp.broadcast_to(mod[:nb_p, None, :], (nb_p, SUBLANES, 3 * D_MODEL))
        mod_s = jnp.broadcast_to(mod[nb_p:nb_p + nb_s, None, :], (nb_s, SUBLANES, 3 * D_MODEL))
        lam_vecs = jnp.stack([lambda_q1[layer], lambda_k1[layer], lambda_q2[layer], lambda_k2[layer]])
        params = (rows(g_pre[layer], SUBLANES), _layout_w_in(w_in[layer]), lam_vecs,
                  g_subln[layer].reshape(1, -1), rows(g_cq[layer], 4 * SUBLANES),
                  _layout_w_uq(w_uq[layer]), rows(g_ckv[layer], 2 * SUBLANES),
                  w_ukv[layer].astype(jnp.bfloat16), w_out[layer].astype(jnp.bfloat16),
                  rows(g_post[layer], SUBLANES))
        y_p = _encoder_layer(y_p, mod_p, layer, tabs, *params)
        y_s = _encoder_layer(y_s, mod_s, layer, tabs, *params)
    return (y_p, y_s)
```

```python
import functools
import math

import numpy as np
import jax
import jax.numpy as jnp
from jax import lax
from jax.experimental import pallas as pl
from jax.experimental.pallas import tpu as pltpu

D_MODEL = 1024
A_HEADS = 4
A_QK_DIM = 64
A_V_DIM = 128
A_WIDTH = A_HEADS * A_V_DIM
A_ROPE_DIM = A_QK_DIM // 4
B_HEADS = 4
B_NOPE_DIM = 128
B_ROPE_DIM = 64
B_V_DIM = 128
B_WIDTH = B_HEADS * B_V_DIM
B_QK_PAD = 256
Q_LORA = 384
KV_LORA = 256
ROPE_THETA = 500000.0
NORM_EPS = 1e-6
LOG2E = 1.4426950408889634

LANES = 128
SUBLANES = 8
VMEM_LIMIT = 56 * 1024 * 1024

ROW_TILE = 512
OUT_ROW_TILE = 1024
KV_TILE = ROW_TILE
KV_UNROLL = 2
Q_TILE_A = 512
Q_TILE_B = 512
Q_CHUNKS = (8192, 4096, 2048)
ATTN_VMEM_BUDGET = 48 * 1024 * 1024
PIPE_BODY_COLUMNS = 64 * 512
FINISH_UNROLL = 4

L_MIN = 2.0 ** -80
L_MAX = 2.0 ** 100
BOUND_MARGIN = 1.0 + 2.0 ** -5

SEG_AQ = 0
SEG_AK = 512
SEG_AV = 1024
SEG_AG = 1536
SEG_CQ = 2048
SEG_KR = 2432
SEG_CKV = 2560
SEG_BG = 2816
IN_COLS_PAD = 3328


def _silu(x):
    return x * (1.0 / (1.0 + jnp.exp(-x)))


def _rms(x, g):
    return x * lax.rsqrt(jnp.mean(x * x, axis=-1, keepdims=True) + NORM_EPS) * g


def _bdot(a, b):
    return jnp.dot(a, b, preferred_element_type=jnp.float32)


def _mod_kernel(c_ref, w_ref, b_ref, o_ref):
    c = _silu(c_ref[...]).astype(jnp.bfloat16)
    o_ref[...] = _bdot(c, w_ref[...].astype(jnp.bfloat16)) + b_ref[...]


def _adaln_mod(c_all, w_ada, b_ada):
    rows = c_all.shape[0]
    n_blk = 1024
    return pl.pallas_call(
        _mod_kernel,
        out_shape=jax.ShapeDtypeStruct((rows, 3 * D_MODEL), jnp.float32),
        grid=(3 * D_MODEL // n_blk,),
        in_specs=[
            pl.BlockSpec((rows, D_MODEL), lambda j: (0, 0)),
            pl.BlockSpec((D_MODEL, n_blk), lambda j: (0, j)),
            pl.BlockSpec((1, n_blk), lambda j: (0, j)),
        ],
        out_specs=pl.BlockSpec((rows, n_blk), lambda j: (0, j)),
        compiler_params=pltpu.CompilerParams(
            dimension_semantics=("arbitrary",), vmem_limit_bytes=VMEM_LIMIT),
        name="adaln_mod",
    )(c_all, w_ada, b_ada.reshape(1, -1))


def _rope(x, c, s_lo, s_hi, half):
    return (x * c + pltpu.roll(x, LANES - half, 1) * s_lo
            + pltpu.roll(x, half, 1) * s_hi)


def _in_proj_kernel(x_ref, mod_ref, gpre_ref, win_ref, gcq_ref, wuq_ref, gckv_ref, wukv_ref,
                    ca_ref, sa_lo_ref, sa_hi_ref, cb_ref, sb_lo_ref, sb_hi_ref,
                    qa_ref, ka_ref, va_ref, qb_ref, kb_ref, vb_ref, sg_ref, kna_ref, knb_ref,
                    *, scale_a, scale_b):
    mod = mod_ref[0][0:1, :]
    shift = mod[:, 0:D_MODEL]
    scale = mod[:, D_MODEL:2 * D_MODEL]
    half_a = A_ROPE_DIM // 2
    half_b = B_ROPE_DIM // 2
    bf = jnp.bfloat16

    h = (_rms(x_ref[0], gpre_ref[0:1, :]) * (1.0 + scale) + shift).astype(bf)

    def proj(lo, width):
        return _bdot(h, win_ref[:, lo:lo + width])

    ca, sa_lo, sa_hi = ca_ref[...], sa_lo_ref[...], sa_hi_ref[...]
    cb, sb_lo, sb_hi = cb_ref[...], sb_lo_ref[...], sb_hi_ref[...]

    cq_kr = proj(SEG_CQ, SEG_CKV - SEG_CQ)
    cq = _rms(cq_kr[:, 0:Q_LORA], gcq_ref[0:1, :]).astype(bf)
    kr = _rope(cq_kr[:, SEG_KR - SEG_CQ:], cb, sb_lo, sb_hi, half_b).astype(bf)
    ckv = _rms(proj(SEG_CKV, KV_LORA), gckv_ref[0:1, :]).astype(bf)

    def max_sq_norm(k, sel):
        n2 = _bdot(k * k, sel)
        part = jnp.max(n2.reshape(n2.shape[0] // SUBLANES, SUBLANES, LANES), axis=0)
        return jnp.broadcast_to(jnp.max(part, axis=0, keepdims=True), (SUBLANES, LANES))

    aq = proj(SEG_AQ, 512)
    ak = proj(SEG_AK, 512)
    av = proj(SEG_AV, 512)
    feat = lax.broadcasted_iota(jnp.int32, (LANES, Q_TILE_A), 0)
    sel_r = lax.broadcasted_iota(jnp.int32, (LANES, LANES), 0)
    sel_c = lax.broadcasted_iota(jnp.int32, (LANES, LANES), 1)
    same_map = (((sel_r < A_QK_DIM) & (sel_c < A_QK_DIM))
                | ((sel_r >= A_QK_DIM) & (sel_c >= A_QK_DIM)))
    sel_maps = jnp.where(same_map, 1.0, 0.0).astype(bf)
    lane = lax.broadcasted_iota(jnp.int32, (SUBLANES, LANES), 1)
    for hd in range(A_HEADS):
        sl = slice(hd * LANES, (hd + 1) * LANES)
        qt = (_rope(aq[:, sl], ca, sa_lo, sa_hi, half_a) * scale_a).T.astype(bf)
        for blk in range(ROW_TILE // Q_TILE_A):
            q_blk = qt[:, blk * Q_TILE_A:(blk + 1) * Q_TILE_A]
            zero = jnp.zeros_like(q_blk)
            qa_ref[0, hd, blk, :, 0:Q_TILE_A] = jnp.where(feat < A_QK_DIM, q_blk, zero)
            qa_ref[0, hd, blk, :, Q_TILE_A:2 * Q_TILE_A] = jnp.where(feat >= A_QK_DIM, q_blk, zero)
        k = _rope(ak[:, sl], ca, sa_lo, sa_hi, half_a).astype(bf)
        ka_ref[0, hd] = k
        both = max_sq_norm(k, sel_maps)
        swapped = pltpu.roll(both, A_QK_DIM, 1)
        kna_ref[0, hd, 0, 0] = jnp.where(lane < A_QK_DIM, both, swapped)
        kna_ref[0, hd, 0, 1] = jnp.where(lane < A_QK_DIM, swapped, both)
        va_ref[0, hd, 0] = av[:, sl].T.astype(bf)

    q = _bdot(cq, wuq_ref[...])
    kv = _bdot(ckv, wukv_ref[...])
    sel_all = jnp.ones((B_QK_PAD, LANES), bf)
    for hd in range(B_HEADS):
        qn = q[:, hd * LANES:(hd + 1) * LANES]
        qr = q[:, (B_HEADS + hd) * LANES:(B_HEADS + hd + 1) * LANES]
        qb_ref[0, hd, 0, 0:LANES, :] = (qn * scale_b).T.astype(bf)
        qb_ref[0, hd, 0, LANES:2 * LANES, :] = (
            _rope(qr, cb, sb_lo, sb_hi, half_b) * scale_b).T.astype(bf)
        kn = kv[:, 2 * hd * LANES:(2 * hd + 1) * LANES].astype(bf)
        kb_ref[0, hd, :, 0:LANES] = kn
        kb_ref[0, hd, :, LANES:2 * LANES] = kr
        knb_ref[0, hd, 0, 0] = max_sq_norm(jnp.concatenate([kn, kr], axis=1), sel_all)
        vb_ref[0, hd, 0] = kv[:, (2 * hd + 1) * LANES:(2 * hd + 2) * LANES].T.astype(bf)

    sg_ref[0, :, 0:A_WIDTH] = _silu(proj(SEG_AG, A_WIDTH))
    sg_ref[0, :, A_WIDTH:A_WIDTH + B_WIDTH] = _silu(proj(SEG_BG, B_WIDTH))


def _in_proj(x, mod, gpre, win, gcq, wuq, gckv, wukv, tabs):
    b, s, _ = x.shape
    tr = ROW_TILE
    const = lambda shape: pl.BlockSpec(shape, lambda i, j: (0,) * len(shape))
    tab = pl.BlockSpec((tr, LANES), lambda i, j: (j, 0))
    head = lambda d: pl.BlockSpec((1, A_HEADS, tr, d), lambda i, j: (i, 0, j, 0))
    head_t = lambda d: pl.BlockSpec((1, A_HEADS, 1, d, tr), lambda i, j: (i, 0, j, 0, 0))
    norm_tiles = lambda m: pl.BlockSpec((1, A_HEADS, 1, m, SUBLANES, LANES),
                                        lambda i, j: (i, 0, j, 0, 0, 0))
    qa_blocks = tr // Q_TILE_A
    assert tr == Q_TILE_B and tr == qa_blocks * Q_TILE_A
    bf = jnp.bfloat16
    kern = functools.partial(_in_proj_kernel,
                             scale_a=A_QK_DIM ** -0.5 * LOG2E,
                             scale_b=(B_NOPE_DIM + B_ROPE_DIM) ** -0.5 * LOG2E)
    return pl.pallas_call(
        kern,
        out_shape=(
            jax.ShapeDtypeStruct((b, A_HEADS, s // Q_TILE_A, LANES, 2 * Q_TILE_A), bf),
            jax.ShapeDtypeStruct((b, A_HEADS, s, LANES), bf),
            jax.ShapeDtypeStruct((b, A_HEADS, s // tr, A_V_DIM, tr), bf),
            jax.ShapeDtypeStruct((b, B_HEADS, s // tr, B_QK_PAD, tr), bf),
            jax.ShapeDtypeStruct((b, B_HEADS, s, B_QK_PAD), bf),
            jax.ShapeDtypeStruct((b, B_HEADS, s // tr, B_V_DIM, tr), bf),
            jax.ShapeDtypeStruct((b, s, A_WIDTH + B_WIDTH), jnp.float32),
            jax.ShapeDtypeStruct((b, A_HEADS, s // tr, 2, SUBLANES, LANES), jnp.float32),
            jax.ShapeDtypeStruct((b, B_HEADS, s // tr, 1, SUBLANES, LANES), jnp.float32),
        ),
        grid=(b, s // tr),
        in_specs=[
            pl.BlockSpec((1, tr, D_MODEL), lambda i, j: (i, j, 0)),
            pl.BlockSpec((1, SUBLANES, 3 * D_MODEL), lambda i, j: (i, 0, 0)),
            const((SUBLANES, D_MODEL)),
            const((D_MODEL, IN_COLS_PAD)),
            const((4 * SUBLANES, Q_LORA)),
            const((Q_LORA, 2 * B_HEADS * LANES)),
            const((2 * SUBLANES, KV_LORA)),
            const((KV_LORA, B_HEADS * (B_NOPE_DIM + B_V_DIM))),
            tab, tab, tab, tab, tab, tab,
        ],
        out_specs=(pl.BlockSpec((1, A_HEADS, qa_blocks, LANES, 2 * Q_TILE_A),
                                lambda i, j: (i, 0, j, 0, 0)),
                   head(LANES), head_t(A_V_DIM), head_t(B_QK_PAD), head(B_QK_PAD),
                   head_t(B_V_DIM),
                   pl.BlockSpec((1, tr, A_WIDTH + B_WIDTH), lambda i, j: (i, j, 0)),
                   norm_tiles(2), norm_tiles(1)),
        compiler_params=pltpu.CompilerParams(
            dimension_semantics=("arbitrary", "arbitrary"), vmem_limit_bytes=VMEM_LIMIT),
        name="in_proj",
    )(x, mod, gpre, win, gcq, wuq, gckv, wukv, *tabs)


def _key_tile(k_ref, t):
    return k_ref[0, 0, pl.ds(pl.multiple_of(t * KV_TILE, KV_TILE), KV_TILE), :]


def _col_bound(qt, kmax_tiles, cols_per_map):
    qsq = qt.astype(jnp.float32)
    qsq = (qsq * qsq).astype(jnp.bfloat16)
    ones = jnp.ones((2 * SUBLANES, qt.shape[0]), jnp.bfloat16)
    q2 = _bdot(ones, qsq)[0:SUBLANES]
    kmax = jnp.concatenate(
        [t for t in kmax_tiles for _ in range(cols_per_map // LANES)], axis=1)
    return jnp.sqrt(q2 * kmax) * BOUND_MARGIN


def _running_max_pass(qt, k_ref, vt_ref, m_ref, l_ref, acc_ref):
    n = qt.shape[1]
    n_kv = k_ref.shape[2] // KV_TILE
    m_ref[...] = jnp.full(m_ref.shape, -jnp.inf, jnp.float32)
    l_ref[...] = jnp.zeros(l_ref.shape, jnp.float32)
    acc_ref[...] = jnp.zeros(acc_ref.shape, jnp.float32)

    def body(t, carry):
        s = _bdot(_key_tile(k_ref, t), qt).reshape(KV_TILE // SUBLANES, SUBLANES, n)
        m_old = m_ref[...]
        m_col = jnp.max(jnp.max(s, axis=0), axis=0, keepdims=True)
        m_new = jnp.maximum(m_old, m_col)
        alpha = jnp.exp2(m_old - m_new)
        p = jnp.exp2(s - m_new)
        l_ref[...] = alpha * l_ref[...] + jnp.sum(p, axis=0)
        pv = _bdot(vt_ref[0, 0, t], p.reshape(KV_TILE, n).astype(jnp.bfloat16))
        acc_ref[...] = alpha[0:1] * acc_ref[...] + pv
        m_ref[...] = m_new
        return carry

    lax.fori_loop(0, n_kv, body, 0)


def _attention_chunk(q_tile, finish, tq, k_ref, vt_ref, kn_ref,
                     bound_ref, pa_ref, pb_ref, m_ref, l_ref, acc_ref):
    nq, _, n = bound_ref.shape
    n_grp = k_ref.shape[2] // (KV_TILE * KV_UNROLL)
    total = n_grp * nq
    log_nq = nq.bit_length() - 1
    assert nq == 1 << log_nq and total % 2 == 0 and total >= 2

    kmax_tiles = [jnp.max(kn_ref[0, 0, :, m], axis=0) for m in range(kn_ref.shape[3])]
    for i in range(nq):
        bound_ref[i] = _col_bound(q_tile(i), kmax_tiles, tq)
    l_ref[...] = jnp.zeros(l_ref.shape, jnp.float32)
    acc_ref[...] = jnp.zeros(acc_ref.shape, jnp.float32)

    def numerators(step, p_ref):
        g, i = step >> log_nq, step & (nq - 1)
        qt = q_tile(i)
        bound = bound_ref[i]
        lsum = l_ref[i]
        rows = KV_UNROLL * KV_TILE
        keys = k_ref[0, 0, pl.ds(pl.multiple_of(g * rows, rows), rows), :]
        s = _bdot(keys, qt)
        p = jnp.exp2(s.reshape(rows // SUBLANES, SUBLANES, n) - bound)
        p_ref[...] = p.reshape(rows, n).astype(jnp.bfloat16)
        l_ref[i] = lsum + jnp.sum(p, axis=0)

    def weighted_values(step, p_ref):
        g, i = step >> log_nq, step & (nq - 1)
        vt = jnp.concatenate([vt_ref[0, 0, g * KV_UNROLL + u] for u in range(KV_UNROLL)], axis=1)
        acc_ref[i] += _bdot(vt, p_ref[...])

    p_refs = (pa_ref, pb_ref)

    def pipeline_step(step, parity):
        numerators(step + 1, p_refs[1 - parity])
        weighted_values(step, p_refs[parity])

    numerators(0, pa_ref)
    unroll = PIPE_BODY_COLUMNS // n
    assert unroll % 2 == 0
    n_body = (total - 1) // unroll

    def body(t, carry):
        for h in range(unroll):
            pipeline_step(unroll * t + h, h % 2)
        return carry

    lax.fori_loop(0, n_body, body, 0)
    for step in range(n_body * unroll, total - 1):
        pipeline_step(step, step % 2)
    weighted_values(total - 1, p_refs[(total - 1) % 2])

    def usable(l_part):
        l_col = jnp.sum(l_part, axis=-2)
        return (jnp.min(l_col) >= L_MIN) & (jnp.max(l_col) <= L_MAX)

    def recompute_if_unusable(i, carry):
        @pl.when(jnp.logical_not(usable(l_ref[i])))
        def _():
            _running_max_pass(q_tile(i), k_ref, vt_ref, m_ref, l_ref.at[i], acc_ref.at[i])

        return carry

    @pl.when(jnp.logical_not(usable(l_ref[...])))
    def _():
        lax.fori_loop(0, nq, recompute_if_unusable, 0)

    def finish_tiles(t, carry):
        for h in range(FINISH_UNROLL):
            i = FINISH_UNROLL * t + h
            finish(i, acc_ref[i], jnp.sum(l_ref[i], axis=0, keepdims=True))
        return carry

    assert nq % FINISH_UNROLL == 0
    lax.fori_loop(0, nq // FINISH_UNROLL, finish_tiles, 0)


def _rows(i, tq):
    return pl.ds(pl.multiple_of(i * tq, tq), tq)


def _attn_a_kernel(par_ref, q_ref, k_ref, vt_ref, kn_ref, sg_ref, o_ref,
                   bound_ref, pa_ref, pb_ref, m_ref, l_ref, acc_ref, *, lambda_init):
    tq = Q_TILE_A

    def q_tile(i):
        return q_ref[0, 0, i]

    lam_v = par_ref[0:4, 0:A_QK_DIM]
    gsub = par_ref[SUBLANES:SUBLANES + 1, :]
    lam = (jnp.exp(jnp.sum(lam_v[0:1] * lam_v[1:2], axis=-1, keepdims=True))
           - jnp.exp(jnp.sum(lam_v[2:3] * lam_v[3:4], axis=-1, keepdims=True))
           + lambda_init)

    def finish(i, acc, l):
        ot = acc * (1.0 / l)
        o = (ot[:, 0:tq] - lam * ot[:, tq:2 * tq]).T
        o = _rms(o, gsub) * (1.0 - lambda_init)
        o_ref[0, _rows(i, tq), :] = (o * sg_ref[0, _rows(i, tq), :]).astype(o_ref.dtype)

    _attention_chunk(q_tile, finish, tq, k_ref, vt_ref, kn_ref,
                     bound_ref, pa_ref, pb_ref, m_ref, l_ref, acc_ref)


def _attn_b_kernel(q_ref, k_ref, vt_ref, kn_ref, sg_ref, o_ref,
                   bound_ref, pa_ref, pb_ref, m_ref, l_ref, acc_ref):
    tq = Q_TILE_B

    def q_tile(i):
        return q_ref[0, 0, i]

    def finish(i, acc, l):
        o = (acc * (1.0 / l)).T
        o_ref[0, _rows(i, tq), :] = (o * sg_ref[0, _rows(i, tq), :]).astype(o_ref.dtype)

    _attention_chunk(q_tile, finish, tq, k_ref, vt_ref, kn_ref,
                     bound_ref, pa_ref, pb_ref, m_ref, l_ref, acc_ref)


def _query_chunk(s, dq, dv, tq, n):
    group = KV_UNROLL * KV_TILE
    for qc in Q_CHUNKS:
        if s % qc:
            continue
        windows = 2 * (s * dq * 2 + s * dv * 2 + (qc // tq) * dq * n * 2 + qc * dv * 4 + qc * dv * 2)
        scratch = (qc // tq) * (dv + 2 * SUBLANES) * n * 4 + 2 * group * n * 2
        step_temps = 2 * group * n * 4
        if windows + scratch + step_temps <= ATTN_VMEM_BUDGET:
            return qc
    raise ValueError(f"no query chunk fits VMEM for sequence length {s}")


def _attn_call(kern, name, extra_in, extra_specs, q, k, vt, kn, sg, tq, col0):
    b, nh, s, dq = k.shape
    dv = vt.shape[3]
    n_maps = kn.shape[3]
    n = q.shape[4]
    qc = _query_chunk(s, dq, dv, tq, n)
    nq = qc // tq
    assert q.shape == (b, nh, s // tq, dq, n) and n % (n_maps * LANES) == 0
    assert kn.shape == (b, nh, s // KV_TILE, n_maps, SUBLANES, LANES)
    return pl.pallas_call(
        kern,
        out_shape=jax.ShapeDtypeStruct((b, s, nh * dv), jnp.bfloat16),
        grid=(b, nh, s // qc),
        in_specs=extra_specs + [
            pl.BlockSpec((1, 1, nq, dq, n), lambda b_, h, c: (b_, h, c, 0, 0)),
            pl.BlockSpec((1, 1, s, dq), lambda b_, h, c: (b_, h, 0, 0)),
            pl.BlockSpec((1, 1, s // KV_TILE, dv, KV_TILE), lambda b_, h, c: (b_, h, 0, 0, 0)),
            pl.BlockSpec((1, 1, s // KV_TILE, n_maps, SUBLANES, LANES),
                         lambda b_, h, c: (b_, h, 0, 0, 0, 0)),
            pl.BlockSpec((1, qc, dv), lambda b_, h, c: (b_, c, col0 + h)),
        ],
        out_specs=pl.BlockSpec((1, qc, dv), lambda b_, h, c: (b_, c, h)),
        scratch_shapes=[
            pltpu.VMEM((nq, SUBLANES, n), jnp.float32),
            pltpu.VMEM((KV_UNROLL * KV_TILE, n), jnp.bfloat16),
            pltpu.VMEM((KV_UNROLL * KV_TILE, n), jnp.bfloat16),
            pltpu.VMEM((SUBLANES, n), jnp.float32),
            pltpu.VMEM((nq, SUBLANES, n), jnp.float32),
            pltpu.VMEM((nq, dv, n), jnp.float32),
        ],
        compiler_params=pltpu.CompilerParams(
            dimension_semantics=("arbitrary", "arbitrary", "arbitrary"),
            vmem_limit_bytes=VMEM_LIMIT),
        name=name,
    )(*extra_in, q, k, vt, kn, sg)


def _attn_a(lam_vecs, gsub, qa, ka, vta, kna, sg, lambda_init):
    par = jnp.zeros((4 * SUBLANES, LANES), jnp.float32)
    par = par.at[0:4, 0:A_QK_DIM].set(lam_vecs).at[SUBLANES:SUBLANES + 1, :].set(gsub)
    spec = pl.BlockSpec(par.shape, lambda b_, h, c: (0, 0))
    return _attn_call(functools.partial(_attn_a_kernel, lambda_init=lambda_init), "attn_a",
                      (par,), [spec], qa, ka, vta, kna, sg, Q_TILE_A, 0)


def _attn_b(qb, kb, vtb, knb, sg):
    return _attn_call(_attn_b_kernel, "attn_b", (), [], qb, kb, vtb, knb, sg, Q_TILE_B, A_HEADS)


def _out_proj_kernel(ya_ref, yb_ref, x_ref, mod_ref, wout_ref, gpost_ref, o_ref):
    out = _bdot(jnp.concatenate([ya_ref[0], yb_ref[0]], axis=1), wout_ref[...])
    gate = mod_ref[0][0:1, 2 * D_MODEL:3 * D_MODEL]
    o_ref[0] = x_ref[0] + gate * _rms(out, gpost_ref[0:1, :])


def _out_proj(ya, yb, x, mod, wout, gpost):
    b, s, _ = x.shape
    tr = OUT_ROW_TILE
    return pl.pallas_call(
        _out_proj_kernel,
        out_shape=jax.ShapeDtypeStruct(x.shape, x.dtype),
        grid=(b, s // tr),
        in_specs=[
            pl.BlockSpec((1, tr, A_WIDTH), lambda i, j: (i, j, 0)),
            pl.BlockSpec((1, tr, B_WIDTH), lambda i, j: (i, j, 0)),
            pl.BlockSpec((1, tr, D_MODEL), lambda i, j: (i, j, 0)),
            pl.BlockSpec((1, SUBLANES, 3 * D_MODEL), lambda i, j: (i, 0, 0)),
            pl.BlockSpec((A_WIDTH + B_WIDTH, D_MODEL), lambda i, j: (0, 0)),
            pl.BlockSpec((SUBLANES, D_MODEL), lambda i, j: (0, 0)),
        ],
        out_specs=pl.BlockSpec((1, tr, D_MODEL), lambda i, j: (i, j, 0)),
        compiler_params=pltpu.CompilerParams(
            dimension_semantics=("arbitrary", "arbitrary"), vmem_limit_bytes=VMEM_LIMIT),
        name="out_proj",
    )(ya, yb, x, mod, wout, gpost)


def _rope_tables(seq, dim):
    half = dim // 2
    inv_freq = jnp.float32(ROPE_THETA) ** (-jnp.arange(0, dim, 2, dtype=jnp.float32) / dim)
    d = np.arange(LANES) % 64
    in_span, lo = d < dim, d < half
    freq = jnp.where(in_span, inv_freq[d % half], 0.0)
    ang = jnp.arange(seq, dtype=jnp.float32)[:, None] * freq[None, :]
    cos, sin = jnp.cos(ang), jnp.sin(ang)
    return cos, jnp.where(lo, -sin, 0.0), jnp.where(in_span & ~lo, sin, 0.0)


def _layout_w_in(w_in):
    aq, ak, av, ag, cq, ckv, kr, bg = jnp.split(
        w_in, [512, 1024, 1536, 2048, 2432, 2688, 2752], axis=1)
    kr = jnp.pad(kr, ((0, 0), (0, LANES - B_ROPE_DIM)))
    return jnp.concatenate([aq, ak, av, ag, cq, kr, ckv, bg], axis=1).astype(jnp.bfloat16)


def _layout_w_uq(w_uq):
    w = w_uq.reshape(Q_LORA, B_HEADS, B_NOPE_DIM + B_ROPE_DIM)
    nope = w[:, :, :B_NOPE_DIM].reshape(Q_LORA, B_HEADS * B_NOPE_DIM)
    rope = jnp.pad(w[:, :, B_NOPE_DIM:], ((0, 0), (0, 0), (0, LANES - B_ROPE_DIM)))
    return jnp.concatenate([nope, rope.reshape(Q_LORA, B_HEADS * LANES)], axis=1).astype(jnp.bfloat16)


def _encoder_layer(x, mod, layer, tabs, g_pre, w_in, lam_vecs, g_subln, g_cq, w_uq, g_ckv, w_ukv,
                   w_out, g_post):
    lambda_init = 0.8 - 0.6 * math.exp(-0.3 * layer)
    qa, ka, vta, qb, kb, vtb, sg, kna, knb = _in_proj(
        x, mod, g_pre, w_in, g_cq, w_uq, g_ckv, w_ukv, tabs)
    ya = _attn_a(lam_vecs, g_subln, qa, ka, vta, kna, sg, lambda_init)
    yb = _attn_b(qb, kb, vtb, knb, sg)
    return _out_proj(ya, yb, x, mod, w_out, g_post)


def kernel(x_prompt, x_sample, c_prompt, c_sample, w_ada, b_ada, g_pre, w_in, lambda_q1, lambda_k1,
           lambda_q2, lambda_k2, g_subln, g_cq, w_uq, g_ckv, w_ukv, w_out, g_post):
    depth = w_ada.shape[0]
    nb_p, nb_s = c_prompt.shape[0], c_sample.shape[0]
    pad_rows = -(nb_p + nb_s) % 8
    c_all = jnp.concatenate(
        [c_prompt, c_sample, jnp.zeros((pad_rows, D_MODEL), jnp.float32)], axis=0)
    max_seq = max(x_prompt.shape[1], x_sample.shape[1])
    tabs = _rope_tables(max_seq, A_ROPE_DIM) + _rope_tables(max_seq, B_ROPE_DIM)
    y_p, y_s = x_prompt, x_sample
    for layer in range(depth):
        mod = _adaln_mod(c_all, w_ada[layer], b_ada[layer])
        rows = lambda v, r: jnp.broadcast_to(v.reshape(1, -1), (r, v.size))
        mod_p = jnp.broadcast_to(mod[:nb_p, None, :], (nb_p, SUBLANES, 3 * D_MODEL))
        mod_s = jnp.broadcast_to(mod[nb_p:nb_p + nb_s, None, :], (nb_s, SUBLANES, 3 * D_MODEL))
        lam_vecs = jnp.stack([lambda_q1[layer], lambda_k1[layer], lambda_q2[layer], lambda_k2[layer]])
        params = (rows(g_pre[layer], SUBLANES), _layout_w_in(w_in[layer]), lam_vecs,
                  g_subln[layer].reshape(1, -1), rows(g_cq[layer], 4 * SUBLANES),
                  _layout_w_uq(w_uq[layer]), rows(g_ckv[layer], 2 * SUBLANES),
                  w_ukv[layer].astype(jnp.bfloat16), w_out[layer].astype(jnp.bfloat16),
                  rows(g_post[layer], SUBLANES))
        y_p = _encoder_layer(y_p, mod_p, layer, tabs, *params)
        y_s = _encoder_layer(y_s, mod_s, layer, tabs, *params)
    return (y_p, y_s)
```
